```python
import jax, jax.numpy as jnp
from jax import lax
import numpy as np

D_MODEL = 1024
BATCH = 8
SEQ = 2048
DEPTH = 1
DEC_BATCH = 128
DEC_SEQ = 8
PAST_LEN = 16384
PAGE_SIZE = 128

MIX_WIDTH = D_MODEL
A_WIDTH = MIX_WIDTH // 2
A_GROUPS = 4
A_GROUP_DIM = A_WIDTH // A_GROUPS
A_CHUNK = 128
B_WIDTH = MIX_WIDTH - A_WIDTH
B_HEADS = 4
B_DV = B_WIDTH // B_HEADS
B_DK = B_DV // 2
B_LOWRANK = 16
B_GATE_NORMALIZER = 16.0
B_CHUNK = 64
D_FF = 2816
EPS = 1e-6
IN_SIZES = (A_WIDTH, A_WIDTH, B_HEADS * B_DK, B_HEADS * B_DK, B_WIDTH, B_WIDTH, B_LOWRANK)
IN_COLS = sum(IN_SIZES)

kernel_name = 'hymba_chunkmlp_gla_macaron_step'


def _rmsnorm(x, g):
    xf = x.astype(jnp.float32)
    xf = xf * lax.rsqrt(jnp.mean(xf * xf, axis=-1, keepdims=True) + EPS)
    return (xf * g.astype(jnp.float32)).astype(x.dtype)


def _swiglu(x, w_in, w_out):
    gate, up = jnp.split(x @ w_in, 2, axis=-1)
    return (jax.nn.silu(gate) * up) @ w_out


def _chunk_spatial_gate(u, v, w_s, b_s):
    bsz, t, g, c = v.shape
    tp = -(-t // A_CHUNK) * A_CHUNK
    if tp != t:
        v = jnp.pad(v, ((0, 0), (0, tp - t), (0, 0), (0, 0)))
    vc = v.reshape(bsz, tp // A_CHUNK, A_CHUNK, g, c)
    causal = jnp.tril(jnp.ones((A_CHUNK, A_CHUNK), dtype=bool))
    w = jnp.where(causal[None], w_s, jnp.zeros((), w_s.dtype))
    z = jnp.einsum('gts,bnsgc->bntgc', w, vc) + b_s.T[None, None, :, :, None]
    z = z.reshape(bsz, tp, g, c)[:, :t]
    return u * z


def _gla(q, k, v, log_a, s0):
    bsz, t, h, dk = q.shape
    dv = v.shape[-1]
    c = min(B_CHUNK, t)
    tp = -(-t // c) * c
    if tp != t:
        pw = ((0, 0), (0, tp - t), (0, 0), (0, 0))
        q, k, v, log_a = (jnp.pad(a, pw) for a in (q, k, v, log_a))
    n = tp // c
    f32 = jnp.float32
    qf = q.astype(f32).reshape(bsz, n, c, h, dk)
    kf = k.astype(f32).reshape(bsz, n, c, h, dk)
    vf = v.astype(f32).reshape(bsz, n, c, h, dv)
    b = jnp.cumsum(log_a.astype(f32).reshape(bsz, n, c, h, dk), axis=2)
    b_ref = b[:, :, c // 2][:, :, None]
    b_last = b[:, :, c - 1]
    causal = jnp.tril(jnp.ones((c, c), dtype=bool))
    scores = jnp.einsum('bnthd,bnshd->bnhts', qf * jnp.exp(b - b_ref), kf * jnp.exp(b_ref - b))
    scores = jnp.where(causal, scores, 0.0)
    o_intra = jnp.einsum('bnhts,bnshv->bnthv', scores, vf)
    chunk_upd = jnp.einsum('bnshd,bnshv->bnhdv', kf * jnp.exp(b_last[:, :, None] - b), vf)
    chunk_decay = jnp.exp(b_last)

    def step(s, xs):
        dec, upd = xs
        return dec[..., None] * s + upd, s

    s_final, s_prev = lax.scan(step, s0.astype(f32),
                               (jnp.moveaxis(chunk_decay, 1, 0), jnp.moveaxis(chunk_upd, 1, 0)))
    s_prev = jnp.moveaxis(s_prev, 0, 1)
    o_inter = jnp.einsum('bnthd,bnhdv->bnthv', qf * jnp.exp(b), s_prev)
    o = (o_intra + o_inter).reshape(bsz, tp, h, dv)[:, :t]
    return o, s_final


def _mixer(h, s0, w_in, a_ws, a_bs, a_vnorm, a_onorm, b_wa2, b_ba, b_onorm, w_out):
    bsz, t, _ = h.shape
    p = h @ w_in
    pu, pv, pq, pk, pvb, pr, plr = jnp.split(p, np.cumsum(IN_SIZES)[:-1].tolist(), axis=-1)
    u = jax.nn.gelu(pu).reshape(bsz, t, A_GROUPS, A_GROUP_DIM)
    va = _rmsnorm(jax.nn.gelu(pv).reshape(bsz, t, A_GROUPS, A_GROUP_DIM), a_vnorm)
    ya = _chunk_spatial_gate(u, va, a_ws, a_bs).reshape(bsz, t, A_WIDTH)
    ya = _rmsnorm(ya, a_onorm)
    q = pq.reshape(bsz, t, B_HEADS, B_DK) * (B_DK ** -0.5)
    k = pk.reshape(bsz, t, B_HEADS, B_DK)
    vb = pvb.reshape(bsz, t, B_HEADS, B_DV)
    log_a = jax.nn.log_sigmoid((plr @ b_wa2 + b_ba).astype(jnp.float32)) / B_GATE_NORMALIZER
    o, s_new = _gla(q, k, vb, log_a.reshape(bsz, t, B_HEADS, B_DK), s0)
    yb = _rmsnorm(o, b_onorm) * jax.nn.silu(pr).reshape(bsz, t, B_HEADS, B_DV)
    y = jnp.concatenate([ya, yb.reshape(bsz, t, B_WIDTH).astype(ya.dtype)], axis=-1) @ w_out
    return y, s_new, va.reshape(bsz, t, A_WIDTH)


def _layer(x, s0, ffn1_norm, ffn1_w_in, ffn1_w_out, mix_norm, w_in, a_ws, a_bs, a_vnorm,
           a_onorm, b_wa2, b_ba, b_onorm, w_out, ffn2_norm, ffn2_w_in, ffn2_w_out):
    x = x + 0.5 * _swiglu(_rmsnorm(x, ffn1_norm), ffn1_w_in, ffn1_w_out)
    y, s_new, va = _mixer(_rmsnorm(x, mix_norm), s0, w_in, a_ws, a_bs, a_vnorm, a_onorm,
                          b_wa2, b_ba, b_onorm, w_out)
    x = x + y
    x = x + 0.5 * _swiglu(_rmsnorm(x, ffn2_norm), ffn2_w_in, ffn2_w_out)
    return x, s_new, va


def setup_inputs(seed: int = 0) -> dict:
    key = jax.random.key(seed)
    ks = jax.random.split(key, 24)
    nrm = lambda k, shape, s: jax.random.normal(k, shape, jnp.float32) * s
    gain = lambda k, shape: 1.0 + 0.01 * jax.random.normal(k, shape, jnp.float32)
    L = DEPTH
    return {
        'x_prompt': nrm(ks[0], (BATCH, SEQ, D_MODEL), 1.0),
        'x_sample': nrm(ks[1], (DEC_BATCH, DEC_SEQ, D_MODEL), 1.0),
        'state_gla': nrm(ks[2], (L, DEC_BATCH, B_HEADS, B_DK, B_DV), 1.0),
        'ffn1_norm': gain(ks[3], (L, D_MODEL)),
        'ffn1_w_in': nrm(ks[4], (L, D_MODEL, 2 * D_FF), D_MODEL ** -0.5),
        'ffn1_w_out': nrm(ks[5], (L, D_FF, D_MODEL), D_FF ** -0.5),
        'mix_norm': gain(ks[6], (L, D_MODEL)),
        'w_in': nrm(ks[7], (L, D_MODEL, IN_COLS), D_MODEL ** -0.5),
        'a_ws': nrm(ks[8], (L, A_GROUPS, A_CHUNK, A_CHUNK), A_CHUNK ** -0.5),
        'a_bs': 1.0 + 0.1 * jax.random.normal(ks[9], (L, A_GROUPS, A_CHUNK), jnp.float32),
        'a_vnorm': gain(ks[10], (L, A_GROUPS, A_GROUP_DIM)),
        'a_onorm': gain(ks[11], (L, A_WIDTH)),
        'b_wa2': nrm(ks[12], (L, B_LOWRANK, B_HEADS * B_DK), B_LOWRANK ** -0.5),
        'b_ba': nrm(ks[13], (L, B_HEADS * B_DK), 0.1),
        'b_onorm': gain(ks[14], (L, B_HEADS, B_DV)),
        'w_out': nrm(ks[15], (L, MIX_WIDTH, D_MODEL), MIX_WIDTH ** -0.5),
        'ffn2_norm': gain(ks[16], (L, D_MODEL)),
        'ffn2_w_in': nrm(ks[17], (L, D_MODEL, 2 * D_FF), D_MODEL ** -0.5),
        'ffn2_w_out': nrm(ks[18], (L, D_FF, D_MODEL), D_FF ** -0.5),
        'final_norm': gain(ks[19], (D_MODEL,)),
    }


def reference(x_prompt, x_sample, state_gla, ffn1_norm, ffn1_w_in, ffn1_w_out, mix_norm, w_in,
              a_ws, a_bs, a_vnorm, a_onorm, b_wa2, b_ba, b_onorm, w_out, ffn2_norm, ffn2_w_in,
              ffn2_w_out, final_norm):
    yp, ys = x_prompt, x_sample
    s_prompt, s_sample, v_sample = [], [], []
    for i in range(DEPTH):
        w = (ffn1_norm[i], ffn1_w_in[i], ffn1_w_out[i], mix_norm[i], w_in[i], a_ws[i], a_bs[i],
             a_vnorm[i], a_onorm[i], b_wa2[i], b_ba[i], b_onorm[i], w_out[i], ffn2_norm[i],
             ffn2_w_in[i], ffn2_w_out[i])
        s0_prompt = jnp.zeros((x_prompt.shape[0], B_HEADS, B_DK, B_DV), jnp.float32)
        yp, sp, _ = _layer(yp, s0_prompt, *w)
        ys, ss, vs = _layer(ys, state_gla[i], *w)
        s_prompt.append(sp)
        s_sample.append(ss)
        v_sample.append(vs)
    y_prompt = _rmsnorm(yp, final_norm)
    y_sample = _rmsnorm(ys, final_norm)
    return (y_prompt, y_sample, jnp.stack(s_prompt), jnp.stack(s_sample), jnp.stack(v_sample))
```

```python
import functools

import numpy as np
import jax
import jax.numpy as jnp
from jax import lax
from jax.experimental import pallas as pl
from jax.experimental.pallas import tpu as pltpu

D_MODEL = 1024
D_FF = 2816
A_WIDTH = 512
A_GROUPS = 4
A_CHUNK = 128
B_HEADS = 4
B_DK = 64
B_DV = 128
B_LOWRANK = 16
B_GATE_NORMALIZER = 16.0
B_CHUNK = 64
EPS = 1e-6

OFF_U, OFF_V, OFF_Q, OFF_K, OFF_VB, OFF_R, OFF_LR = 0, 512, 1024, 1280, 1536, 2048, 2560
IN_COLS = 2576
P_COLS = 2688

LANES = 128
MXU_DIM = 256
ROWS = 128
FF_CHUNK = MXU_DIM
N_FF_CHUNKS = D_FF // FF_CHUNK
VMEM_LIMIT = 56 * 1024 * 1024

BF = jnp.bfloat16
F32 = jnp.float32


def _dot(a, b):
    return jnp.dot(a, b, preferred_element_type=F32)


def _dot_nt(a, b):
    return lax.dot_general(a, b, (((1,), (1,)), ((), ())), preferred_element_type=F32)


def _rmsnorm(x, g):
    return (x * lax.rsqrt(jnp.mean(x * x, axis=-1, keepdims=True) + EPS)) * g


def _gelu_tanh(x):
    c = np.float32(np.sqrt(2.0 / np.pi))
    return x * (0.5 * (1.0 + jnp.tanh(c * (x + np.float32(0.044715) * (x * x * x)))))


def _silu(x):
    return x * (1.0 / (1.0 + jnp.exp(-x)))


def _log_sigmoid(x):
    return -(jnp.maximum(-x, 0.0) + jnp.log1p(jnp.exp(-jnp.abs(x))))


def _swiglu_acc(h, wg_ref, wu_ref, wo_ref, acc_ref):
    for j in range(N_FF_CHUNKS):
        cs = slice(j * FF_CHUNK, (j + 1) * FF_CHUNK)
        gate = _dot(h, wg_ref[:, cs])
        up = _dot(h, wu_ref[:, cs])
        act = (_silu(gate) * up).astype(BF)
        part = _dot(act, wo_ref[cs, :])
        if j == 0:
            acc_ref[...] = part
        else:
            acc_ref[...] += part


def _ffn1_kernel(x_ref, g1_ref, wg_ref, wu_ref, wo_ref, g2_ref, wp_ref, x1_ref, p_ref, acc_ref):
    x = x_ref[...]
    h = _rmsnorm(x, g1_ref[...]).astype(BF)
    _swiglu_acc(h, wg_ref, wu_ref, wo_ref, acc_ref)
    x1 = x + 0.5 * acc_ref[...]
    x1_ref[...] = x1
    h2 = _rmsnorm(x1, g2_ref[...]).astype(BF)
    p_ref[...] = _dot(h2, wp_ref[...])


def _resident(shape):
    nd = len(shape)
    return pl.BlockSpec(shape, lambda i, _nd=nd: (0,) * _nd, pipeline_mode=pl.Buffered(1))


def _ffn1_call(x, g1, wg, wu, wo, g2, wp, tm):
    t = x.shape[0]
    row = lambda i: (i, 0)
    return pl.pallas_call(
        _ffn1_kernel,
        grid=(t // tm,),
        in_specs=[
            pl.BlockSpec((tm, D_MODEL), row),
            _resident((1, D_MODEL)),
            _resident((D_MODEL, D_FF)),
            _resident((D_MODEL, D_FF)),
            _resident((D_FF, D_MODEL)),
            _resident((1, D_MODEL)),
            _resident((D_MODEL, P_COLS)),
        ],
        out_specs=[
            pl.BlockSpec((tm, D_MODEL), row),
            pl.BlockSpec((tm, P_COLS), row),
        ],
        out_shape=[
            jax.ShapeDtypeStruct((t, D_MODEL), F32),
            jax.ShapeDtypeStruct((t, P_COLS), F32),
        ],
        scratch_shapes=[pltpu.VMEM((tm, D_MODEL), F32)],
        compiler_params=pltpu.CompilerParams(
            dimension_semantics=("arbitrary",), vmem_limit_bytes=VMEM_LIMIT),
        name="ffn1_inproj",
    )(x, g1, wg, wu, wo, g2, wp)


def _ffn2_kernel(x1_ref, y_ref, wout_ref, g_ref, wg_ref, wu_ref, wo_ref, gf_ref, o_ref, acc_ref):
    x2 = x1_ref[...] + _dot(y_ref[...], wout_ref[...])
    h = _rmsnorm(x2, g_ref[...]).astype(BF)
    _swiglu_acc(h, wg_ref, wu_ref, wo_ref, acc_ref)
    x3 = x2 + 0.5 * acc_ref[...]
    o_ref[...] = _rmsnorm(x3, gf_ref[...])


def _ffn2_call(x1, y, wout, g, wg, wu, wo, gf, tm):
    t = x1.shape[0]
    row = lambda i: (i, 0)
    return pl.pallas_call(
        _ffn2_kernel,
        grid=(t // tm,),
        in_specs=[
            pl.BlockSpec((tm, D_MODEL), row),
            pl.BlockSpec((tm, D_MODEL), row),
            _resident((D_MODEL, D_MODEL)),
            _resident((1, D_MODEL)),
            _resident((D_MODEL, D_FF)),
            _resident((D_MODEL, D_FF)),
            _resident((D_FF, D_MODEL)),
            _resident((1, D_MODEL)),
        ],
        out_specs=pl.BlockSpec((tm, D_MODEL), row),
        out_shape=jax.ShapeDtypeStruct((t, D_MODEL), F32),
        scratch_shapes=[pltpu.VMEM((tm, D_MODEL), F32)],
        compiler_params=pltpu.CompilerParams(
            dimension_semantics=("arbitrary",), vmem_limit_bytes=VMEM_LIMIT),
        name="outproj_ffn2",
    )(x1, y, wout, g, wg, wu, wo, gf)


def _cum_matrices(chunk):
    r = np.arange(ROWS)[:, None]
    k = np.arange(ROWS)[None, :]
    same = (r // chunk) == (k // chunk)
    m_cum = same & (k <= r)
    m_ref = same & ((k % chunk) <= (chunk // 2))
    m_last = same
    m = np.concatenate([m_cum, m_ref, m_last], axis=0).astype(np.float32)
    return np.concatenate([m, m], axis=1)


def _mix_block(p_ref, r0, chunk, consts, get_state, put_state):
    (wmix_ref, bias_ref, vnorm_ref, anorm_ref, wa2_ref, ba_ref, onorm_ref, cum_ref) = consts
    n_chunks = ROWS // chunk
    shift = int(np.log2(chunk))
    rows = pl.ds(r0, ROWS)

    def cols(off, width):
        return p_ref[rows, off:off + width]

    ya = []
    va = []
    ssq = None
    for g in range(A_GROUPS):
        gs = slice(g * LANES, (g + 1) * LANES)
        u = _gelu_tanh(cols(OFF_U + g * LANES, LANES))
        v = _rmsnorm(_gelu_tanh(cols(OFF_V + g * LANES, LANES)), vnorm_ref[:, gs])
        z = _dot(wmix_ref[g], v.astype(BF)) + bias_ref[g]
        yg = u * z
        s = jnp.sum(yg * yg, axis=-1, keepdims=True)
        ssq = s if ssq is None else ssq + s
        ya.append(yg)
        va.append(v)
    inv = lax.rsqrt(ssq * (1.0 / A_WIDTH) + EPS)
    ya = [(ya[g] * inv) * anorm_ref[:, g * LANES:(g + 1) * LANES] for g in range(A_GROUPS)]

    z = _dot(cols(OFF_LR, LANES).astype(BF), wa2_ref[...]) + ba_ref[...]
    la = _log_sigmoid(z) * (1.0 / B_GATE_NORMALIZER)
    la_hi = la.astype(BF)
    la_lo = (la - la_hi.astype(F32)).astype(BF)
    sums = _dot(cum_ref[...], jnp.concatenate([la_hi, la_lo], axis=0))
    b = sums[0:ROWS]
    b_ref = sums[ROWS:2 * ROWS]
    b_last = sums[2 * ROWS:3 * ROWS]

    ri = lax.broadcasted_iota(jnp.int32, (ROWS, ROWS), 0)
    ci = lax.broadcasted_iota(jnp.int32, (ROWS, ROWS), 1)
    rchunk = lax.shift_right_logical(ri, shift)
    cchunk = lax.shift_right_logical(ci, shift)
    causal = (rchunk == cchunk) & (ci <= ri)
    lane = lax.broadcasted_iota(jnp.int32, (ROWS, LANES), 1)
    head0 = lane < B_DK
    bd_r = lax.broadcasted_iota(jnp.int32, (2 * B_DK, 2 * B_DV), 0)
    bd_c = lax.broadcasted_iota(jnp.int32, (2 * B_DK, 2 * B_DV), 1)
    blockdiag = (bd_r < B_DK) == (bd_c < B_DV)

    yb = []
    for pair in range(B_HEADS // 2):
        ls = slice(pair * LANES, (pair + 1) * LANES)
        q = cols(OFF_Q + pair * LANES, LANES) * np.float32(B_DK ** -0.5)
        k = cols(OFF_K + pair * LANES, LANES)
        vv = cols(OFF_VB + pair * 2 * B_DV, 2 * B_DV)
        vv_bf = vv.astype(BF)
        bp, brp, blp = b[:, ls], b_ref[:, ls], b_last[:, ls]
        qs = q * jnp.exp(bp - brp)
        ks = (k * jnp.exp(brp - bp)).astype(BF)
        kl = k * jnp.exp(blp - bp)
        qb = (q * jnp.exp(bp)).astype(BF)

        o_heads = []
        for hh in range(2):
            msk = head0 if hh == 0 else jnp.logical_not(head0)
            qh = jnp.where(msk, qs, 0.0).astype(BF)
            sc = jnp.where(causal, _dot_nt(qh, ks), 0.0).astype(BF)
            o_heads.append(_dot(sc, vv_bf[:, hh * B_DV:(hh + 1) * B_DV]))
        o = jnp.concatenate(o_heads, axis=1)

        kl_t = kl.T
        bl_t = blp.T
        o_inter = []
        for j in range(n_chunks):
            s_prev = get_state(pair, j)
            o_inter.append(_dot(qb[j * chunk:(j + 1) * chunk], s_prev.astype(BF)))
            in_chunk = cchunk == j
            upd = _dot(jnp.where(in_chunk, kl_t, 0.0).astype(BF), vv_bf)
            dec = jnp.exp(bl_t[:, j * chunk:j * chunk + 1])
            put_state(pair, j, dec * s_prev + jnp.where(blockdiag, upd, 0.0))
        o = o + jnp.concatenate(o_inter, axis=0)

        for hh in range(2):
            h_idx = 2 * pair + hh
            hs = slice(h_idx * B_DV, (h_idx + 1) * B_DV)
            oh = _rmsnorm(o[:, hh * B_DV:(hh + 1) * B_DV], onorm_ref[:, hs])
            yb.append(oh * _silu(cols(OFF_R + h_idx * B_DV, B_DV)))

    y = jnp.concatenate(ya + yb, axis=1).astype(BF)
    return y, jnp.concatenate(va, axis=1)


def _blockdiag_state(s0, s1):
    zero = jnp.zeros_like(s0)
    return jnp.concatenate(
        [jnp.concatenate([s0, zero], axis=1), jnp.concatenate([zero, s1], axis=1)], axis=0)


def _mix_prompt_kernel(p_ref, wmix_ref, bias_ref, vnorm_ref, anorm_ref, wa2_ref, ba_ref,
                       onorm_ref, cum_ref, y_ref, st_ref, s_scr, *, n_sub):
    @pl.when(pl.program_id(1) == 0)
    def _():
        s_scr[...] = jnp.zeros_like(s_scr)

    consts = (wmix_ref, bias_ref, vnorm_ref, anorm_ref, wa2_ref, ba_ref, onorm_ref, cum_ref)

    def body(i, carry):
        r0 = pl.multiple_of(i * ROWS, ROWS)
        cur = {}

        def get_state(pair, j):
            return cur[pair] if pair in cur else s_scr[pair]

        def put_state(pair, j, s):
            cur[pair] = s

        y, _ = _mix_block(p_ref, r0, B_CHUNK, consts, get_state, put_state)
        y_ref[pl.ds(r0, ROWS), :] = y
        for pair in range(B_HEADS // 2):
            s_scr[pair] = cur[pair]
        return carry

    lax.fori_loop(0, n_sub, body, 0)
    for pair in range(B_HEADS // 2):
        s = s_scr[pair]
        st_ref[0, 2 * pair] = s[0:B_DK, 0:B_DV]
        st_ref[0, 2 * pair + 1] = s[B_DK:2 * B_DK, B_DV:2 * B_DV]


def _mix_sample_kernel(p_ref, s0_ref, wmix_ref, bias_ref, vnorm_ref, anorm_ref, wa2_ref, ba_ref,
                       onorm_ref, cum_ref, y_ref, va_ref, st_ref, *, chunk):
    consts = (wmix_ref, bias_ref, vnorm_ref, anorm_ref, wa2_ref, ba_ref, onorm_ref, cum_ref)

    def get_state(pair, j):
        return _blockdiag_state(s0_ref[j, 2 * pair], s0_ref[j, 2 * pair + 1])

    def put_state(pair, j, s):
        st_ref[j, 2 * pair] = s[0:B_DK, 0:B_DV]
        st_ref[j, 2 * pair + 1] = s[B_DK:2 * B_DK, B_DV:2 * B_DV]

    y, va = _mix_block(p_ref, 0, chunk, consts, get_state, put_state)
    y_ref[...] = y
    va_ref[...] = va


def _const_spec(shape, grid_rank):
    nd = len(shape)
    if grid_rank == 1:
        return pl.BlockSpec(shape, lambda i, _nd=nd: (0,) * _nd)
    return pl.BlockSpec(shape, lambda i, j, _nd=nd: (0,) * _nd)


def _mix_const_specs(grid_rank):
    return [
        _const_spec((A_GROUPS, ROWS, ROWS), grid_rank),
        _const_spec((A_GROUPS, ROWS, LANES), grid_rank),
        _const_spec((1, A_WIDTH), grid_rank),
        _const_spec((1, A_WIDTH), grid_rank),
        _const_spec((LANES, B_HEADS * B_DK), grid_rank),
        _const_spec((1, B_HEADS * B_DK), grid_rank),
        _const_spec((1, B_HEADS * B_DV), grid_rank),
        _const_spec((3 * ROWS, 2 * ROWS), grid_rank),
    ]


def _mix_prompt_call(p, consts, batch, seq, lblk):
    n_seq_blk = seq // lblk
    return pl.pallas_call(
        functools.partial(_mix_prompt_kernel, n_sub=lblk // ROWS),
        grid=(batch, n_seq_blk),
        in_specs=[pl.BlockSpec((lblk, P_COLS), lambda b, s: (b * n_seq_blk + s, 0))]
        + _mix_const_specs(2),
        out_specs=[
            pl.BlockSpec((lblk, D_MODEL), lambda b, s: (b * n_seq_blk + s, 0)),
            pl.BlockSpec((1, B_HEADS, B_DK, B_DV), lambda b, s: (b, 0, 0, 0)),
        ],
        out_shape=[
            jax.ShapeDtypeStruct((batch * seq, D_MODEL), BF),
            jax.ShapeDtypeStruct((batch, B_HEADS, B_DK, B_DV), F32),
        ],
        scratch_shapes=[pltpu.VMEM((B_HEADS // 2, 2 * B_DK, 2 * B_DV), F32)],
        compiler_params=pltpu.CompilerParams(
            dimension_semantics=("arbitrary", "arbitrary"), vmem_limit_bytes=VMEM_LIMIT),
        name="mix_prompt",
    )(p, *consts)


def _mix_sample_call(p, s0, consts, n_seq, seq):
    per_blk = ROWS // seq
    return pl.pallas_call(
        functools.partial(_mix_sample_kernel, chunk=seq),
        grid=(n_seq // per_blk,),
        in_specs=[
            pl.BlockSpec((ROWS, P_COLS), lambda i: (i, 0)),
            pl.BlockSpec((per_blk, B_HEADS, B_DK, B_DV), lambda i: (i, 0, 0, 0)),
        ] + _mix_const_specs(1),
        out_specs=[
            pl.BlockSpec((ROWS, D_MODEL), lambda i: (i, 0)),
            pl.BlockSpec((ROWS, A_WIDTH), lambda i: (i, 0)),
            pl.BlockSpec((per_blk, B_HEADS, B_DK, B_DV), lambda i: (i, 0, 0, 0)),
        ],
        out_shape=[
            jax.ShapeDtypeStruct((n_seq * seq, D_MODEL), BF),
            jax.ShapeDtypeStruct((n_seq * seq, A_WIDTH), F32),
            jax.ShapeDtypeStruct((n_seq, B_HEADS, B_DK, B_DV), F32),
        ],
        compiler_params=pltpu.CompilerParams(
            dimension_semantics=("arbitrary",), vmem_limit_bytes=VMEM_LIMIT),
        name="mix_sample",
    )(p, s0, *consts)


def _mix_consts(a_ws, a_bs, a_vnorm, a_onorm, b_wa2, b_ba, b_onorm, chunk, seq_len):
    n_rep = ROWS // seq_len
    tril = jnp.tril(jnp.ones((seq_len, seq_len), dtype=bool))
    w = jnp.where(tril[None], a_ws[:, :seq_len, :seq_len], 0.0)
    if n_rep > 1:
        w = jax.vmap(lambda wg: jnp.kron(jnp.eye(n_rep, dtype=wg.dtype), wg))(w)
    bias = jnp.tile(a_bs[:, :seq_len], (1, n_rep))
    bias = jnp.broadcast_to(bias[:, :, None], (A_GROUPS, ROWS, LANES))
    wa2 = jnp.zeros((LANES, B_HEADS * B_DK), F32).at[:B_LOWRANK].set(b_wa2)
    return (
        w.astype(BF),
        bias,
        a_vnorm.reshape(1, A_WIDTH),
        a_onorm.reshape(1, A_WIDTH),
        wa2.astype(BF),
        b_ba.reshape(1, B_HEADS * B_DK),
        b_onorm.reshape(1, B_HEADS * B_DV),
        jnp.asarray(_cum_matrices(chunk), dtype=BF),
    )


def kernel(x_prompt, x_sample, state_gla, ffn1_norm, ffn1_w_in, ffn1_w_out, mix_norm, w_in,
           a_ws, a_bs, a_vnorm, a_onorm, b_wa2, b_ba, b_onorm, w_out, ffn2_norm, ffn2_w_in,
           ffn2_w_out, final_norm):
    batch, seq, _ = x_prompt.shape
    n_dec, dec_seq, _ = x_sample.shape
    assert ffn1_norm.shape[0] == 1, "single-layer kernel"
    assert seq % A_CHUNK == 0 and ROWS % dec_seq == 0 and dec_seq <= B_CHUNK

    g1 = ffn1_norm[0].reshape(1, D_MODEL)
    wg1 = ffn1_w_in[0][:, :D_FF].astype(BF)
    wu1 = ffn1_w_in[0][:, D_FF:].astype(BF)
    wo1 = ffn1_w_out[0].astype(BF)
    g2 = mix_norm[0].reshape(1, D_MODEL)
    wp = jnp.pad(w_in[0], ((0, 0), (0, P_COLS - IN_COLS))).astype(BF)
    wout = w_out[0].astype(BF)
    g3 = ffn2_norm[0].reshape(1, D_MODEL)
    wg2 = ffn2_w_in[0][:, :D_FF].astype(BF)
    wu2 = ffn2_w_in[0][:, D_FF:].astype(BF)
    wo2 = ffn2_w_out[0].astype(BF)
    gf = final_norm.reshape(1, D_MODEL)
    mix_args = (a_ws[0], a_bs[0], a_vnorm[0], a_onorm[0], b_wa2[0], b_ba[0], b_onorm[0])
    consts_p = _mix_consts(*mix_args, chunk=B_CHUNK, seq_len=A_CHUNK)
    consts_s = _mix_consts(*mix_args, chunk=dec_seq, seq_len=dec_seq)

    tm = 512
    xp = x_prompt.reshape(batch * seq, D_MODEL)
    xs = x_sample.reshape(n_dec * dec_seq, D_MODEL)

    x1p, pp = _ffn1_call(xp, g1, wg1, wu1, wo1, g2, wp, tm)
    x1s, ps = _ffn1_call(xs, g1, wg1, wu1, wo1, g2, wp, tm)

    yp, st_p = _mix_prompt_call(pp, consts_p, batch, seq, lblk=512)
    ys, va_s, st_s = _mix_sample_call(ps, state_gla[0], consts_s, n_dec, dec_seq)

    out_p = _ffn2_call(x1p, yp, wout, g3, wg2, wu2, wo2, gf, tm)
    out_s = _ffn2_call(x1s, ys, wout, g3, wg2, wu2, wo2, gf, tm)

    return (
        out_p.reshape(batch, seq, D_MODEL),
        out_s.reshape(n_dec, dec_seq, D_MODEL),
        st_p[None],
        st_s[None],
        va_s.reshape(1, n_dec, dec_seq, A_WIDTH),
    )
```

```python
import functools

import numpy as np
import jax
import jax.numpy as jnp
from jax import lax
from jax.experimental import pallas as pl
from jax.experimental.pallas import tpu as pltpu

D_MODEL = 1024
D_FF = 2816
A_WIDTH = 512
A_GROUPS = 4
A_CHUNK = 128
B_HEADS = 4
B_DK = 64
B_DV = 128
B_LOWRANK = 16
B_GATE_NORMALIZER = 16.0
B_CHUNK = 64
EPS = 1e-6

OFF_U, OFF_V, OFF_Q, OFF_K, OFF_VB, OFF_R, OFF_LR = 0, 512, 1024, 1280, 1536, 2048, 2560
IN_COLS = 2576
P_COLS = 2688

LANES = 128
MXU_DIM = 256
ROWS = 128
FF_CHUNK = MXU_DIM
N_FF_CHUNKS = D_FF // FF_CHUNK
VMEM_LIMIT = 56 * 1024 * 1024

BF = jnp.bfloat16
F32 = jnp.float32


def _dot(a, b):
    return jnp.dot(a, b, preferred_element_type=F32)


def _dot_nt(a, b):
    return lax.dot_general(a, b, (((1,), (1,)), ((), ())), preferred_element_type=F32)


def _rmsnorm(x, g):
    return (x * lax.rsqrt(jnp.mean(x * x, axis=-1, keepdims=True) + EPS)) * g


def _gelu_tanh(x):
    c = np.float32(np.sqrt(2.0 / np.pi))
    ca = np.float32(np.sqrt(2.0 / np.pi) * 0.044715)
    t = jnp.tanh(x * (c + ca * (x * x)))
    hx = 0.5 * x
    return hx + hx * t


def _silu(x):
    return x * (1.0 / (1.0 + jnp.exp(-x)))


def _log_sigmoid(x):
    return jnp.minimum(x, 0.0) - jnp.log(1.0 + jnp.exp(-jnp.abs(x)))


def _swiglu_acc(h, wg_ref, wu_ref, wo_ref, acc_ref):
    for j in range(N_FF_CHUNKS):
        cs = slice(j * FF_CHUNK, (j + 1) * FF_CHUNK)
        gate = _dot(h, wg_ref[:, cs])
        up = _dot(h, wu_ref[:, cs])
        act = (_silu(gate) * up).astype(BF)
        part = _dot(act, wo_ref[cs, :])
        if j == 0:
            acc_ref[...] = part
        else:
            acc_ref[...] += part


def _ffn1_kernel(x_ref, g1_ref, wg_ref, wu_ref, wo_ref, g2_ref, wp_ref, x1_ref, p_ref, acc_ref):
    x = x_ref[...]
    h = _rmsnorm(x, g1_ref[...]).astype(BF)
    _swiglu_acc(h, wg_ref, wu_ref, wo_ref, acc_ref)
    x1 = x + 0.5 * acc_ref[...]
    x1_ref[...] = x1
    h2 = _rmsnorm(x1, g2_ref[...]).astype(BF)
    p_ref[...] = _dot(h2, wp_ref[...])


def _resident(shape):
    nd = len(shape)
    return pl.BlockSpec(shape, lambda i, _nd=nd: (0,) * _nd, pipeline_mode=pl.Buffered(1))


def _ffn1_call(x, g1, wg, wu, wo, g2, wp, tm):
    t = x.shape[0]
    row = lambda i: (i, 0)
    return pl.pallas_call(
        _ffn1_kernel,
        grid=(t // tm,),
        in_specs=[
            pl.BlockSpec((tm, D_MODEL), row),
            _resident((1, D_MODEL)),
            _resident((D_MODEL, D_FF)),
            _resident((D_MODEL, D_FF)),
            _resident((D_FF, D_MODEL)),
            _resident((1, D_MODEL)),
            _resident((D_MODEL, P_COLS)),
        ],
        out_specs=[
            pl.BlockSpec((tm, D_MODEL), row),
            pl.BlockSpec((tm, P_COLS), row),
        ],
        out_shape=[
            jax.ShapeDtypeStruct((t, D_MODEL), F32),
            jax.ShapeDtypeStruct((t, P_COLS), F32),
        ],
        scratch_shapes=[pltpu.VMEM((tm, D_MODEL), F32)],
        compiler_params=pltpu.CompilerParams(
            dimension_semantics=("arbitrary",), vmem_limit_bytes=VMEM_LIMIT),
        name="ffn1_inproj",
    )(x, g1, wg, wu, wo, g2, wp)


def _ffn2_kernel(x1_ref, y_ref, wout_ref, g_ref, wg_ref, wu_ref, wo_ref, gf_ref, o_ref, acc_ref):
    x2 = x1_ref[...] + _dot(y_ref[...], wout_ref[...])
    h = _rmsnorm(x2, g_ref[...]).astype(BF)
    _swiglu_acc(h, wg_ref, wu_ref, wo_ref, acc_ref)
    x3 = x2 + 0.5 * acc_ref[...]
    o_ref[...] = _rmsnorm(x3, gf_ref[...])


def _ffn2_call(x1, y, wout, g, wg, wu, wo, gf, tm):
    t = x1.shape[0]
    row = lambda i: (i, 0)
    return pl.pallas_call(
        _ffn2_kernel,
        grid=(t // tm,),
        in_specs=[
            pl.BlockSpec((tm, D_MODEL), row),
            pl.BlockSpec((tm, D_MODEL), row),
            _resident((D_MODEL, D_MODEL)),
            _resident((1, D_MODEL)),
            _resident((D_MODEL, D_FF)),
            _resident((D_MODEL, D_FF)),
            _resident((D_FF, D_MODEL)),
            _resident((1, D_MODEL)),
        ],
        out_specs=pl.BlockSpec((tm, D_MODEL), row),
        out_shape=jax.ShapeDtypeStruct((t, D_MODEL), F32),
        scratch_shapes=[pltpu.VMEM((tm, D_MODEL), F32)],
        compiler_params=pltpu.CompilerParams(
            dimension_semantics=("arbitrary",), vmem_limit_bytes=VMEM_LIMIT),
        name="outproj_ffn2",
    )(x1, y, wout, g, wg, wu, wo, gf)


def _cum_matrices(chunk):
    r = np.arange(ROWS)[:, None]
    k = np.arange(ROWS)[None, :]
    same = (r // chunk) == (k // chunk)
    m_cum = same & (k <= r)
    m_ref = same & ((k % chunk) <= (chunk // 2))
    m_last = same
    m = np.concatenate([m_cum, m_ref, m_last], axis=0).astype(np.float32)
    return np.concatenate([m, m], axis=1)


def _mix_block(p_ref, r0, chunk, consts, get_state, put_state):
    (wmix_ref, bias_ref, vnorm_ref, anorm_ref, wa2_ref, ba_ref, onorm_ref, cum_ref) = consts
    n_chunks = ROWS // chunk
    shift = int(np.log2(chunk))
    rows = pl.ds(r0, ROWS)

    def cols(off, width):
        return p_ref[rows, off:off + width]

    ya = []
    va = []
    ssq = None
    for g in range(A_GROUPS):
        gs = slice(g * LANES, (g + 1) * LANES)
        u = _gelu_tanh(cols(OFF_U + g * LANES, LANES))
        v = _rmsnorm(_gelu_tanh(cols(OFF_V + g * LANES, LANES)), vnorm_ref[:, gs])
        z = _dot(wmix_ref[g], v.astype(BF)) + bias_ref[g]
        yg = u * z
        s = jnp.sum(yg * yg, axis=-1, keepdims=True)
        ssq = s if ssq is None else ssq + s
        ya.append(yg)
        va.append(v)
    inv = lax.rsqrt(ssq * (1.0 / A_WIDTH) + EPS)
    ya = [(ya[g] * inv) * anorm_ref[:, g * LANES:(g + 1) * LANES] for g in range(A_GROUPS)]

    z = _dot(cols(OFF_LR, LANES).astype(BF), wa2_ref[...]) + ba_ref[...]
    la = _log_sigmoid(z) * (1.0 / B_GATE_NORMALIZER)
    la_hi = la.astype(BF)
    la_lo = (la - la_hi.astype(F32)).astype(BF)
    sums = _dot(cum_ref[...], jnp.concatenate([la_hi, la_lo], axis=0))
    b = sums[0:ROWS]
    b_ref = sums[ROWS:2 * ROWS]
    b_last = sums[2 * ROWS:3 * ROWS]

    ri = lax.broadcasted_iota(jnp.int32, (ROWS, ROWS), 0)
    ci = lax.broadcasted_iota(jnp.int32, (ROWS, ROWS), 1)
    rchunk = lax.shift_right_logical(ri, shift)
    cchunk = lax.shift_right_logical(ci, shift)
    causal = (rchunk == cchunk) & (ci <= ri)
    lane = lax.broadcasted_iota(jnp.int32, (ROWS, LANES), 1)
    head0 = lane < B_DK
    bd_r = lax.broadcasted_iota(jnp.int32, (2 * B_DK, 2 * B_DV), 0)
    bd_c = lax.broadcasted_iota(jnp.int32, (2 * B_DK, 2 * B_DV), 1)
    blockdiag = (bd_r < B_DK) == (bd_c < B_DV)

    yb = []
    for pair in range(B_HEADS // 2):
        ls = slice(pair * LANES, (pair + 1) * LANES)
        q = cols(OFF_Q + pair * LANES, LANES) * np.float32(B_DK ** -0.5)
        k = cols(OFF_K + pair * LANES, LANES)
        vv = cols(OFF_VB + pair * 2 * B_DV, 2 * B_DV)
        vv_bf = vv.astype(BF)
        bp, brp, blp = b[:, ls], b_ref[:, ls], b_last[:, ls]
        qs = q * jnp.exp(bp - brp)
        ks = (k * jnp.exp(brp - bp)).astype(BF)
        kl = k * jnp.exp(blp - bp)
        qb = (q * jnp.exp(bp)).astype(BF)

        o_heads = []
        for hh in range(2):
            msk = head0 if hh == 0 else jnp.logical_not(head0)
            qh = jnp.where(msk, qs, 0.0).astype(BF)
            sc = jnp.where(causal, _dot_nt(qh, ks), 0.0).astype(BF)
            o_heads.append(_dot(sc, vv_bf[:, hh * B_DV:(hh + 1) * B_DV]))
        o = jnp.concatenate(o_heads, axis=1)

        kl_t = kl.T
        bl_t = blp.T
        o_inter = []
        for j in range(n_chunks):
            s_prev = get_state(pair, j)
            o_inter.append(_dot(qb[j * chunk:(j + 1) * chunk], s_prev.astype(BF)))
            in_chunk = cchunk == j
            upd = _dot(jnp.where(in_chunk, kl_t, 0.0).astype(BF), vv_bf)
            dec = jnp.exp(bl_t[:, j * chunk:j * chunk + 1])
            put_state(pair, j, dec * s_prev + jnp.where(blockdiag, upd, 0.0))
        o = o + jnp.concatenate(o_inter, axis=0)

        for hh in range(2):
            h_idx = 2 * pair + hh
            hs = slice(h_idx * B_DV, (h_idx + 1) * B_DV)
            oh = _rmsnorm(o[:, hh * B_DV:(hh + 1) * B_DV], onorm_ref[:, hs])
            yb.append(oh * _silu(cols(OFF_R + h_idx * B_DV, B_DV)))

    y = jnp.concatenate(ya + yb, axis=1).astype(BF)
    return y, jnp.concatenate(va, axis=1)


def _blockdiag_state(s0, s1):
    zero = jnp.zeros_like(s0)
    return jnp.concatenate(
        [jnp.concatenate([s0, zero], axis=1), jnp.concatenate([zero, s1], axis=1)], axis=0)


def _mix_prompt_kernel(p_ref, wmix_ref, bias_ref, vnorm_ref, anorm_ref, wa2_ref, ba_ref,
                       onorm_ref, cum_ref, y_ref, st_ref, s_scr, *, n_sub):
    @pl.when(pl.program_id(1) == 0)
    def _():
        s_scr[...] = jnp.zeros_like(s_scr)

    consts = (wmix_ref, bias_ref, vnorm_ref, anorm_ref, wa2_ref, ba_ref, onorm_ref, cum_ref)

    cur = {pair: s_scr[pair] for pair in range(B_HEADS // 2)}

    def get_state(pair, j):
        return cur[pair]

    def put_state(pair, j, s):
        cur[pair] = s

    for i in range(n_sub):
        y, _ = _mix_block(p_ref, i * ROWS, B_CHUNK, consts, get_state, put_state)
        y_ref[i * ROWS:(i + 1) * ROWS, :] = y

    for pair in range(B_HEADS // 2):
        s = cur[pair]
        s_scr[pair] = s
        st_ref[0, 2 * pair] = s[0:B_DK, 0:B_DV]
        st_ref[0, 2 * pair + 1] = s[B_DK:2 * B_DK, B_DV:2 * B_DV]


def _mix_sample_kernel(p_ref, s0_ref, wmix_ref, bias_ref, vnorm_ref, anorm_ref, wa2_ref, ba_ref,
                       onorm_ref, cum_ref, y_ref, va_ref, st_ref, *, chunk):
    consts = (wmix_ref, bias_ref, vnorm_ref, anorm_ref, wa2_ref, ba_ref, onorm_ref, cum_ref)

    def get_state(pair, j):
        return _blockdiag_state(s0_ref[j, 2 * pair], s0_ref[j, 2 * pair + 1])

    def put_state(pair, j, s):
        st_ref[j, 2 * pair] = s[0:B_DK, 0:B_DV]
        st_ref[j, 2 * pair + 1] = s[B_DK:2 * B_DK, B_DV:2 * B_DV]

    y, va = _mix_block(p_ref, 0, chunk, consts, get_state, put_state)
    y_ref[...] = y
    va_ref[...] = va


def _const_spec(shape, grid_rank):
    nd = len(shape)
    if grid_rank == 1:
        return pl.BlockSpec(shape, lambda i, _nd=nd: (0,) * _nd)
    return pl.BlockSpec(shape, lambda i, j, _nd=nd: (0,) * _nd)


def _mix_const_specs(grid_rank):
    return [
        _const_spec((A_GROUPS, ROWS, ROWS), grid_rank),
        _const_spec((A_GROUPS, ROWS, LANES), grid_rank),
        _const_spec((1, A_WIDTH), grid_rank),
        _const_spec((1, A_WIDTH), grid_rank),
        _const_spec((LANES, B_HEADS * B_DK), grid_rank),
        _const_spec((1, B_HEADS * B_DK), grid_rank),
        _const_spec((1, B_HEADS * B_DV), grid_rank),
        _const_spec((3 * ROWS, 2 * ROWS), grid_rank),
    ]


def _mix_prompt_call(p, consts, batch, seq, lblk):
    n_seq_blk = seq // lblk
    return pl.pallas_call(
        functools.partial(_mix_prompt_kernel, n_sub=lblk // ROWS),
        grid=(batch, n_seq_blk),
        in_specs=[pl.BlockSpec((lblk, P_COLS), lambda b, s: (b * n_seq_blk + s, 0))]
        + _mix_const_specs(2),
        out_specs=[
            pl.BlockSpec((lblk, D_MODEL), lambda b, s: (b * n_seq_blk + s, 0)),
            pl.BlockSpec((1, B_HEADS, B_DK, B_DV), lambda b, s: (b, 0, 0, 0)),
        ],
        out_shape=[
            jax.ShapeDtypeStruct((batch * seq, D_MODEL), BF),
            jax.ShapeDtypeStruct((batch, B_HEADS, B_DK, B_DV), F32),
        ],
        scratch_shapes=[pltpu.VMEM((B_HEADS // 2, 2 * B_DK, 2 * B_DV), F32)],
        compiler_params=pltpu.CompilerParams(
            dimension_semantics=("arbitrary", "arbitrary"), vmem_limit_bytes=VMEM_LIMIT),
        name="mix_prompt",
    )(p, *consts)


def _mix_sample_call(p, s0, consts, n_seq, seq):
    per_blk = ROWS // seq
    return pl.pallas_call(
        functools.partial(_mix_sample_kernel, chunk=seq),
        grid=(n_seq // per_blk,),
        in_specs=[
            pl.BlockSpec((ROWS, P_COLS), lambda i: (i, 0)),
            pl.BlockSpec((per_blk, B_HEADS, B_DK, B_DV), lambda i: (i, 0, 0, 0)),
        ] + _mix_const_specs(1),
        out_specs=[
            pl.BlockSpec((ROWS, D_MODEL), lambda i: (i, 0)),
            pl.BlockSpec((ROWS, A_WIDTH), lambda i: (i, 0)),
            pl.BlockSpec((per_blk, B_HEADS, B_DK, B_DV), lambda i: (i, 0, 0, 0)),
        ],
        out_shape=[
            jax.ShapeDtypeStruct((n_seq * seq, D_MODEL), BF),
            jax.ShapeDtypeStruct((n_seq * seq, A_WIDTH), F32),
            jax.ShapeDtypeStruct((n_seq, B_HEADS, B_DK, B_DV), F32),
        ],
        compiler_params=pltpu.CompilerParams(
            dimension_semantics=("arbitrary",), vmem_limit_bytes=VMEM_LIMIT),
        name="mix_sample",
    )(p, s0, *consts)


def _mix_consts(a_ws, a_bs, a_vnorm, a_onorm, b_wa2, b_ba, b_onorm, chunk, seq_len):
    n_rep = ROWS // seq_len
    tril = jnp.tril(jnp.ones((seq_len, seq_len), dtype=bool))
    w = jnp.where(tril[None], a_ws[:, :seq_len, :seq_len], 0.0)
    if n_rep > 1:
        w = jax.vmap(lambda wg: jnp.kron(jnp.eye(n_rep, dtype=wg.dtype), wg))(w)
    bias = jnp.tile(a_bs[:, :seq_len], (1, n_rep))
    bias = jnp.broadcast_to(bias[:, :, None], (A_GROUPS, ROWS, LANES))
    wa2 = jnp.zeros((LANES, B_HEADS * B_DK), F32).at[:B_LOWRANK].set(b_wa2)
    return (
        w.astype(BF),
        bias,
        a_vnorm.reshape(1, A_WIDTH),
        a_onorm.reshape(1, A_WIDTH),
        wa2.astype(BF),
        b_ba.reshape(1, B_HEADS * B_DK),
        b_onorm.reshape(1, B_HEADS * B_DV),
        jnp.asarray(_cum_matrices(chunk), dtype=BF),
    )


def kernel(x_prompt, x_sample, state_gla, ffn1_norm, ffn1_w_in, ffn1_w_out, mix_norm, w_in,
           a_ws, a_bs, a_vnorm, a_onorm, b_wa2, b_ba, b_onorm, w_out, ffn2_norm, ffn2_w_in,
           ffn2_w_out, final_norm):
    batch, seq, _ = x_prompt.shape
    n_dec, dec_seq, _ = x_sample.shape
    assert ffn1_norm.shape[0] == 1, "single-layer kernel"
    assert seq % A_CHUNK == 0 and ROWS % dec_seq == 0 and dec_seq <= B_CHUNK

    g1 = ffn1_norm[0].reshape(1, D_MODEL)
    wg1 = ffn1_w_in[0][:, :D_FF].astype(BF)
    wu1 = ffn1_w_in[0][:, D_FF:].astype(BF)
    wo1 = ffn1_w_out[0].astype(BF)
    g2 = mix_norm[0].reshape(1, D_MODEL)
    wp = jnp.pad(w_in[0], ((0, 0), (0, P_COLS - IN_COLS))).astype(BF)
    wout = w_out[0].astype(BF)
    g3 = ffn2_norm[0].reshape(1, D_MODEL)
    wg2 = ffn2_w_in[0][:, :D_FF].astype(BF)
    wu2 = ffn2_w_in[0][:, D_FF:].astype(BF)
    wo2 = ffn2_w_out[0].astype(BF)
    gf = final_norm.reshape(1, D_MODEL)
    mix_args = (a_ws[0], a_bs[0], a_vnorm[0], a_onorm[0], b_wa2[0], b_ba[0], b_onorm[0])
    consts_p = _mix_consts(*mix_args, chunk=B_CHUNK, seq_len=A_CHUNK)
    consts_s = _mix_consts(*mix_args, chunk=dec_seq, seq_len=dec_seq)

    tm = 512
    xp = x_prompt.reshape(batch * seq, D_MODEL)
    xs = x_sample.reshape(n_dec * dec_seq, D_MODEL)

    x1p, pp = _ffn1_call(xp, g1, wg1, wu1, wo1, g2, wp, tm)
    x1s, ps = _ffn1_call(xs, g1, wg1, wu1, wo1, g2, wp, tm)

    yp, st_p = _mix_prompt_call(pp, consts_p, batch, seq, lblk=512)
    ys, va_s, st_s = _mix_sample_call(ps, state_gla[0], consts_s, n_dec, dec_seq)

    out_p = _ffn2_call(x1p, yp, wout, g3, wg2, wu2, wo2, gf, tm)
    out_s = _ffn2_call(x1s, ys, wout, g3, wg2, wu2, wo2, gf, tm)

    return (
        out_p.reshape(batch, seq, D_MODEL),
        out_s.reshape(n_dec, dec_seq, D_MODEL),
        st_p[None],
        st_s[None],
        va_s.reshape(1, n_dec, dec_seq, A_WIDTH),
    )
```

```python
import functools

import numpy as np
import jax
import jax.numpy as jnp
from jax import lax
from jax.experimental import pallas as pl
from jax.experimental.pallas import tpu as pltpu

D_MODEL = 1024
D_FF = 2816
A_WIDTH = 512
A_GROUPS = 4
A_CHUNK = 128
B_HEADS = 4
B_DK = 64
B_DV = 128
B_LOWRANK = 16
B_GATE_NORMALIZER = 16.0
B_CHUNK = 64
EPS = 1e-6

OFF_U, OFF_V, OFF_Q, OFF_K, OFF_VB, OFF_R, OFF_LR = 0, 512, 1024, 1280, 1536, 2048, 2560
IN_COLS = 2576
P_COLS = 2688

LANES = 128
MXU_DIM = 256
ROWS = 128
FF_CHUNK = MXU_DIM
N_FF_CHUNKS = D_FF // FF_CHUNK
VMEM_LIMIT = 56 * 1024 * 1024

BF = jnp.bfloat16
F32 = jnp.float32


def _dot(a, b):
    return jnp.dot(a, b, preferred_element_type=F32)


def _dot_nt(a, b):
    return lax.dot_general(a, b, (((1,), (1,)), ((), ())), preferred_element_type=F32)


def _rmsnorm(x, g):
    return (x * lax.rsqrt(jnp.mean(x * x, axis=-1, keepdims=True) + EPS)) * g


def _gelu_tanh(x):
    c = np.float32(np.sqrt(2.0 / np.pi))
    ca = np.float32(np.sqrt(2.0 / np.pi) * 0.044715)
    t = jnp.tanh(x * (c + ca * (x * x)))
    hx = 0.5 * x
    return hx + hx * t


def _silu(x):
    return x * (1.0 / (1.0 + jnp.exp(-x)))


def _log_sigmoid(x):
    return jnp.minimum(x, 0.0) - jnp.log(1.0 + jnp.exp(-jnp.abs(x)))


def _swiglu_acc(h, wg_ref, wu_ref, wo_ref, acc_ref):
    for j in range(N_FF_CHUNKS):
        cs = slice(j * FF_CHUNK, (j + 1) * FF_CHUNK)
        gate = _dot(h, wg_ref[:, cs])
        up = _dot(h, wu_ref[:, cs])
        act = (_silu(gate) * up).astype(BF)
        part = _dot(act, wo_ref[cs, :])
        if j == 0:
            acc_ref[...] = part
        else:
            acc_ref[...] += part


def _ffn1_kernel(x_ref, g1_ref, wg_ref, wu_ref, wo_ref, g2_ref, wp_ref, x1_ref, p_ref, acc_ref):
    x = x_ref[...]
    h = _rmsnorm(x, g1_ref[...]).astype(BF)
    _swiglu_acc(h, wg_ref, wu_ref, wo_ref, acc_ref)
    x1 = x + 0.5 * acc_ref[...]
    x1_ref[...] = x1
    h2 = _rmsnorm(x1, g2_ref[...]).astype(BF)
    p_ref[...] = _dot(h2, wp_ref[...])


def _resident(shape):
    nd = len(shape)
    return pl.BlockSpec(shape, lambda i, _nd=nd: (0,) * _nd, pipeline_mode=pl.Buffered(1))


def _ffn1_call(x, g1, wg, wu, wo, g2, wp, tm):
    t = x.shape[0]
    row = lambda i: (i, 0)
    return pl.pallas_call(
        _ffn1_kernel,
        grid=(t // tm,),
        in_specs=[
            pl.BlockSpec((tm, D_MODEL), row),
            _resident((1, D_MODEL)),
            _resident((D_MODEL, D_FF)),
            _resident((D_MODEL, D_FF)),
            _resident((D_FF, D_MODEL)),
            _resident((1, D_MODEL)),
            _resident((D_MODEL, P_COLS)),
        ],
        out_specs=[
            pl.BlockSpec((tm, D_MODEL), row),
            pl.BlockSpec((tm, P_COLS), row),
        ],
        out_shape=[
            jax.ShapeDtypeStruct((t, D_MODEL), F32),
            jax.ShapeDtypeStruct((t, P_COLS), F32),
        ],
        scratch_shapes=[pltpu.VMEM((tm, D_MODEL), F32)],
        compiler_params=pltpu.CompilerParams(
            dimension_semantics=("arbitrary",), vmem_limit_bytes=VMEM_LIMIT),
        name="ffn1_inproj",
    )(x, g1, wg, wu, wo, g2, wp)


def _ffn2_kernel(x1_ref, y_ref, wout_ref, g_ref, wg_ref, wu_ref, wo_ref, gf_ref, o_ref, acc_ref):
    x2 = x1_ref[...] + _dot(y_ref[...], wout_ref[...])
    h = _rmsnorm(x2, g_ref[...]).astype(BF)
    _swiglu_acc(h, wg_ref, wu_ref, wo_ref, acc_ref)
    x3 = x2 + 0.5 * acc_ref[...]
    o_ref[...] = _rmsnorm(x3, gf_ref[...])


def _ffn2_call(x1, y, wout, g, wg, wu, wo, gf, tm):
    t = x1.shape[0]
    row = lambda i: (i, 0)
    return pl.pallas_call(
        _ffn2_kernel,
        grid=(t // tm,),
        in_specs=[
            pl.BlockSpec((tm, D_MODEL), row),
            pl.BlockSpec((tm, D_MODEL), row),
            _resident((D_MODEL, D_MODEL)),
            _resident((1, D_MODEL)),
            _resident((D_MODEL, D_FF)),
            _resident((D_MODEL, D_FF)),
            _resident((D_FF, D_MODEL)),
            _resident((1, D_MODEL)),
        ],
        out_specs=pl.BlockSpec((tm, D_MODEL), row),
        out_shape=jax.ShapeDtypeStruct((t, D_MODEL), F32),
        scratch_shapes=[pltpu.VMEM((tm, D_MODEL), F32)],
        compiler_params=pltpu.CompilerParams(
            dimension_semantics=("arbitrary",), vmem_limit_bytes=VMEM_LIMIT),
        name="outproj_ffn2",
    )(x1, y, wout, g, wg, wu, wo, gf)


def _cum_matrices(chunk):
    r = np.arange(ROWS)[:, None]
    k = np.arange(ROWS)[None, :]
    same = (r // chunk) == (k // chunk)
    m_cum = same & (k <= r)
    m_ref = same & ((k % chunk) <= (chunk // 2))
    m_last = same
    m = np.concatenate([m_cum, m_ref, m_last], axis=0).astype(np.float32)
    return np.concatenate([m, m], axis=1)


def _mix_block(p_ref, r0, chunk, consts, get_state, put_state, emit):
    (wmix_ref, bias_ref, vnorm_ref, anorm_ref, wa2_ref, ba_ref, onorm_ref, cum_ref) = consts
    n_chunks = ROWS // chunk
    shift = int(np.log2(chunk))
    rows = pl.ds(r0, ROWS)

    def cols(off, width):
        return p_ref[rows, off:off + width]

    z = _dot(cols(OFF_LR, LANES).astype(BF), wa2_ref[...]) + ba_ref[...]
    la = _log_sigmoid(z) * (1.0 / B_GATE_NORMALIZER)
    la_hi = la.astype(BF)
    la_lo = (la - la_hi.astype(F32)).astype(BF)
    sums = _dot(cum_ref[...], jnp.concatenate([la_hi, la_lo], axis=0))
    b = sums[0:ROWS]
    b_ref = sums[ROWS:2 * ROWS]
    b_last = sums[2 * ROWS:3 * ROWS]

    ya = []
    va = []
    ssq = None
    for g in range(A_GROUPS):
        gs = slice(g * LANES, (g + 1) * LANES)
        u = _gelu_tanh(cols(OFF_U + g * LANES, LANES))
        v = _rmsnorm(_gelu_tanh(cols(OFF_V + g * LANES, LANES)), vnorm_ref[:, gs])
        zg = _dot(wmix_ref[g], v.astype(BF)) + bias_ref[g]
        yg = u * zg
        s = jnp.sum(yg * yg, axis=-1, keepdims=True)
        ssq = s if ssq is None else ssq + s
        ya.append(yg)
        va.append(v)
        yield
    inv = lax.rsqrt(ssq * (1.0 / A_WIDTH) + EPS)
    ya = [(ya[g] * inv) * anorm_ref[:, g * LANES:(g + 1) * LANES] for g in range(A_GROUPS)]
    gate = [_silu(cols(OFF_R + h * B_DV, B_DV)) for h in range(B_HEADS)]
    yield

    ri = lax.broadcasted_iota(jnp.int32, (ROWS, ROWS), 0)
    ci = lax.broadcasted_iota(jnp.int32, (ROWS, ROWS), 1)
    rchunk = lax.shift_right_logical(ri, shift)
    cchunk = lax.shift_right_logical(ci, shift)
    causal = (rchunk == cchunk) & (ci <= ri)
    lane = lax.broadcasted_iota(jnp.int32, (ROWS, LANES), 1)
    head0 = lane < B_DK

    yb = []
    for pair in range(B_HEADS // 2):
        ls = slice(pair * LANES, (pair + 1) * LANES)
        q = cols(OFF_Q + pair * LANES, LANES) * np.float32(B_DK ** -0.5)
        k = cols(OFF_K + pair * LANES, LANES)
        vv_bf = cols(OFF_VB + pair * 2 * B_DV, 2 * B_DV).astype(BF)
        bp, brp, blp = b[:, ls], b_ref[:, ls], b_last[:, ls]
        qs = q * jnp.exp(bp - brp)
        ks = (k * jnp.exp(brp - bp)).astype(BF)
        kl_t = (k * jnp.exp(blp - bp)).T
        bl_t = blp.T
        qb = (q * jnp.exp(bp)).astype(BF)
        yield

        o_heads = []
        for hh in range(2):
            msk = head0 if hh == 0 else jnp.logical_not(head0)
            qh = jnp.where(msk, qs, 0.0).astype(BF)
            sc = jnp.where(causal, _dot_nt(qh, ks), 0.0).astype(BF)
            o_heads.append(_dot(sc, vv_bf[:, hh * B_DV:(hh + 1) * B_DV]))
        o = jnp.concatenate(o_heads, axis=1)

        o_inter = []
        for j in range(n_chunks):
            s_prev = get_state(pair, j)
            o_inter.append(_dot(qb[j * chunk:(j + 1) * chunk], _blockdiag_bf16(*s_prev)))
            in_chunk = cchunk == j
            upd = _dot(jnp.where(in_chunk, kl_t, 0.0).astype(BF), vv_bf)
            dec = jnp.exp(bl_t[:, j * chunk:j * chunk + 1])
            put_state(pair, j, tuple(
                dec[hh * B_DK:(hh + 1) * B_DK] * s_prev[hh]
                + upd[hh * B_DK:(hh + 1) * B_DK, hh * B_DV:(hh + 1) * B_DV] for hh in range(2)))
        o = o + jnp.concatenate(o_inter, axis=0)

        for hh in range(2):
            h_idx = 2 * pair + hh
            hs = slice(h_idx * B_DV, (h_idx + 1) * B_DV)
            oh = _rmsnorm(o[:, hh * B_DV:(hh + 1) * B_DV], onorm_ref[:, hs])
            yb.append(oh * gate[h_idx])
        yield

    emit(jnp.concatenate(ya + yb, axis=1).astype(BF), jnp.concatenate(va, axis=1))


MIX_STAGGER = 6


def _run_staggered(gens):
    live = [True] * len(gens)
    t = 0
    while any(live):
        for i, gen in enumerate(gens):
            if live[i] and t >= i * MIX_STAGGER:
                try:
                    next(gen)
                except StopIteration:
                    live[i] = False
        t += 1


def _blockdiag_bf16(s0, s1):
    s0 = s0.astype(BF)
    s1 = s1.astype(BF)
    zero = jnp.zeros_like(s0)
    return jnp.concatenate(
        [jnp.concatenate([s0, zero], axis=1), jnp.concatenate([zero, s1], axis=1)], axis=0)


def _mix_prompt_kernel(p_ref, wmix_ref, bias_ref, vnorm_ref, anorm_ref, wa2_ref, ba_ref,
                       onorm_ref, cum_ref, y_ref, st_ref, s_scr, *, n_sub):
    @pl.when(pl.program_id(1) == 0)
    def _():
        s_scr[...] = jnp.zeros_like(s_scr)

    consts = (wmix_ref, bias_ref, vnorm_ref, anorm_ref, wa2_ref, ba_ref, onorm_ref, cum_ref)

    cur = {pair: (s_scr[2 * pair], s_scr[2 * pair + 1]) for pair in range(B_HEADS // 2)}

    def get_state(pair, j):
        return cur[pair]

    def put_state(pair, j, s):
        cur[pair] = s

    def emitter(i):
        def emit(y, va):
            y_ref[i * ROWS:(i + 1) * ROWS, :] = y
        return emit

    _run_staggered([
        _mix_block(p_ref, i * ROWS, B_CHUNK, consts, get_state, put_state, emitter(i))
        for i in range(n_sub)])

    for pair in range(B_HEADS // 2):
        for hh in range(2):
            s_scr[2 * pair + hh] = cur[pair][hh]
            st_ref[0, 2 * pair + hh] = cur[pair][hh]


def _mix_sample_kernel(p_ref, s0_ref, wmix_ref, bias_ref, vnorm_ref, anorm_ref, wa2_ref, ba_ref,
                       onorm_ref, cum_ref, y_ref, va_ref, st_ref, *, chunk, n_sub):
    consts = (wmix_ref, bias_ref, vnorm_ref, anorm_ref, wa2_ref, ba_ref, onorm_ref, cum_ref)
    per_blk = ROWS // chunk

    def block(i):
        def get_state(pair, j):
            n = i * per_blk + j
            return s0_ref[n, 2 * pair], s0_ref[n, 2 * pair + 1]

        def put_state(pair, j, s):
            n = i * per_blk + j
            st_ref[n, 2 * pair] = s[0]
            st_ref[n, 2 * pair + 1] = s[1]

        def emit(y, va):
            y_ref[i * ROWS:(i + 1) * ROWS, :] = y
            va_ref[i * ROWS:(i + 1) * ROWS, :] = va

        return _mix_block(p_ref, i * ROWS, chunk, consts, get_state, put_state, emit)

    _run_staggered([block(i) for i in range(n_sub)])


def _const_spec(shape, grid_rank):
    nd = len(shape)
    if grid_rank == 1:
        return pl.BlockSpec(shape, lambda i, _nd=nd: (0,) * _nd)
    return pl.BlockSpec(shape, lambda i, j, _nd=nd: (0,) * _nd)


def _mix_const_specs(grid_rank):
    return [
        _const_spec((A_GROUPS, ROWS, ROWS), grid_rank),
        _const_spec((A_GROUPS, ROWS, LANES), grid_rank),
        _const_spec((1, A_WIDTH), grid_rank),
        _const_spec((1, A_WIDTH), grid_rank),
        _const_spec((LANES, B_HEADS * B_DK), grid_rank),
        _const_spec((1, B_HEADS * B_DK), grid_rank),
        _const_spec((1, B_HEADS * B_DV), grid_rank),
        _const_spec((3 * ROWS, 2 * ROWS), grid_rank),
    ]


def _mix_prompt_call(p, consts, batch, seq, lblk):
    n_seq_blk = seq // lblk
    return pl.pallas_call(
        functools.partial(_mix_prompt_kernel, n_sub=lblk // ROWS),
        grid=(batch, n_seq_blk),
        in_specs=[pl.BlockSpec((lblk, P_COLS), lambda b, s: (b * n_seq_blk + s, 0))]
        + _mix_const_specs(2),
        out_specs=[
            pl.BlockSpec((lblk, D_MODEL), lambda b, s: (b * n_seq_blk + s, 0)),
            pl.BlockSpec((1, B_HEADS, B_DK, B_DV), lambda b, s: (b, 0, 0, 0)),
        ],
        out_shape=[
            jax.ShapeDtypeStruct((batch * seq, D_MODEL), BF),
            jax.ShapeDtypeStruct((batch, B_HEADS, B_DK, B_DV), F32),
        ],
        scratch_shapes=[pltpu.VMEM((B_HEADS, B_DK, B_DV), F32)],
        compiler_params=pltpu.CompilerParams(
            dimension_semantics=("arbitrary", "arbitrary"), vmem_limit_bytes=VMEM_LIMIT),
        name="mix_prompt",
    )(p, *consts)


def _mix_sample_call(p, s0, consts, n_seq, seq, lblk):
    seq_per_step = lblk // seq
    return pl.pallas_call(
        functools.partial(_mix_sample_kernel, chunk=seq, n_sub=lblk // ROWS),
        grid=(n_seq // seq_per_step,),
        in_specs=[
            pl.BlockSpec((lblk, P_COLS), lambda i: (i, 0)),
            pl.BlockSpec((seq_per_step, B_HEADS, B_DK, B_DV), lambda i: (i, 0, 0, 0)),
        ] + _mix_const_specs(1),
        out_specs=[
            pl.BlockSpec((lblk, D_MODEL), lambda i: (i, 0)),
            pl.BlockSpec((lblk, A_WIDTH), lambda i: (i, 0)),
            pl.BlockSpec((seq_per_step, B_HEADS, B_DK, B_DV), lambda i: (i, 0, 0, 0)),
        ],
        out_shape=[
            jax.ShapeDtypeStruct((n_seq * seq, D_MODEL), BF),
            jax.ShapeDtypeStruct((n_seq * seq, A_WIDTH), F32),
            jax.ShapeDtypeStruct((n_seq, B_HEADS, B_DK, B_DV), F32),
        ],
        compiler_params=pltpu.CompilerParams(
            dimension_semantics=("arbitrary",), vmem_limit_bytes=VMEM_LIMIT),
        name="mix_sample",
    )(p, s0, *consts)


def _mix_consts(a_ws, a_bs, a_vnorm, a_onorm, b_wa2, b_ba, b_onorm, chunk, seq_len):
    n_rep = ROWS // seq_len
    tril = jnp.tril(jnp.ones((seq_len, seq_len), dtype=bool))
    w = jnp.where(tril[None], a_ws[:, :seq_len, :seq_len], 0.0)
    if n_rep > 1:
        w = jax.vmap(lambda wg: jnp.kron(jnp.eye(n_rep, dtype=wg.dtype), wg))(w)
    bias = jnp.tile(a_bs[:, :seq_len], (1, n_rep))
    bias = jnp.broadcast_to(bias[:, :, None], (A_GROUPS, ROWS, LANES))
    wa2 = jnp.zeros((LANES, B_HEADS * B_DK), F32).at[:B_LOWRANK].set(b_wa2)
    return (
        w.astype(BF),
        bias,
        a_vnorm.reshape(1, A_WIDTH),
        a_onorm.reshape(1, A_WIDTH),
        wa2.astype(BF),
        b_ba.reshape(1, B_HEADS * B_DK),
        b_onorm.reshape(1, B_HEADS * B_DV),
        jnp.asarray(_cum_matrices(chunk), dtype=BF),
    )


def kernel(x_prompt, x_sample, state_gla, ffn1_norm, ffn1_w_in, ffn1_w_out, mix_norm, w_in,
           a_ws, a_bs, a_vnorm, a_onorm, b_wa2, b_ba, b_onorm, w_out, ffn2_norm, ffn2_w_in,
           ffn2_w_out, final_norm):
    batch, seq, _ = x_prompt.shape
    n_dec, dec_seq, _ = x_sample.shape
    assert ffn1_norm.shape[0] == 1, "single-layer kernel"
    assert seq % A_CHUNK == 0 and ROWS % dec_seq == 0 and dec_seq <= B_CHUNK

    g1 = ffn1_norm[0].reshape(1, D_MODEL)
    wg1 = ffn1_w_in[0][:, :D_FF].astype(BF)
    wu1 = ffn1_w_in[0][:, D_FF:].astype(BF)
    wo1 = ffn1_w_out[0].astype(BF)
    g2 = mix_norm[0].reshape(1, D_MODEL)
    wp = jnp.pad(w_in[0], ((0, 0), (0, P_COLS - IN_COLS))).astype(BF)
    wout = w_out[0].astype(BF)
    g3 = ffn2_norm[0].reshape(1, D_MODEL)
    wg2 = ffn2_w_in[0][:, :D_FF].astype(BF)
    wu2 = ffn2_w_in[0][:, D_FF:].astype(BF)
    wo2 = ffn2_w_out[0].astype(BF)
    gf = final_norm.reshape(1, D_MODEL)
    mix_args = (a_ws[0], a_bs[0], a_vnorm[0], a_onorm[0], b_wa2[0], b_ba[0], b_onorm[0])
    consts_p = _mix_consts(*mix_args, chunk=B_CHUNK, seq_len=A_CHUNK)
    consts_s = _mix_consts(*mix_args, chunk=dec_seq, seq_len=dec_seq)

    tm = 512
    xp = x_prompt.reshape(batch * seq, D_MODEL)
    xs = x_sample.reshape(n_dec * dec_seq, D_MODEL)

    x1p, pp = _ffn1_call(xp, g1, wg1, wu1, wo1, g2, wp, tm)
    x1s, ps = _ffn1_call(xs, g1, wg1, wu1, wo1, g2, wp, tm)

    yp, st_p = _mix_prompt_call(pp, consts_p, batch, seq, lblk=1024)
    ys, va_s, st_s = _mix_sample_call(ps, state_gla[0], consts_s, n_dec, dec_seq, lblk=512)

    out_p = _ffn2_call(x1p, yp, wout, g3, wg2, wu2, wo2, gf, tm)
    out_s = _ffn2_call(x1s, ys, wout, g3, wg2, wu2, wo2, gf, tm)

    return (
        out_p.reshape(batch, seq, D_MODEL),
        out_s.reshape(n_dec, dec_seq, D_MODEL),
        st_p[None],
        st_s[None],
        va_s.reshape(1, n_dec, dec_seq, A_WIDTH),
    )
```

```python
import functools

import numpy as np
import jax
import jax.numpy as jnp
from jax import lax
from jax.experimental import pallas as pl
from jax.experimental.pallas import tpu as pltpu

D_MODEL = 1024
D_FF = 2816
A_WIDTH = 512
A_GROUPS = 4
A_CHUNK = 128
B_HEADS = 4
B_DK = 64
B_DV = 128
B_LOWRANK = 16
B_GATE_NORMALIZER = 16.0
B_CHUNK = 64
EPS = 1e-6

OFF_U, OFF_V, OFF_Q, OFF_K, OFF_VB, OFF_R, OFF_LR = 0, 512, 1024, 1280, 1536, 2048, 2560
IN_COLS = 2576
P_COLS = 2688

LANES = 128
MXU_DIM = 256
ROWS = 128
FF_CHUNK = MXU_DIM
N_FF_CHUNKS = D_FF // FF_CHUNK
VMEM_LIMIT = 56 * 1024 * 1024

FFN_TILE = 1024
FFN_HALF = FFN_TILE // 2
MIX_STAGGER = 6

BF = jnp.bfloat16
F32 = jnp.float32


def _dot(a, b):
    return jnp.dot(a, b, preferred_element_type=F32)


def _dot_nt(a, b):
    return lax.dot_general(a, b, (((1,), (1,)), ((), ())), preferred_element_type=F32)


def _rmsnorm(x, g):
    return (x * lax.rsqrt(jnp.mean(x * x, axis=-1, keepdims=True) + EPS)) * g


def _gelu_tanh(x):
    c = np.float32(np.sqrt(2.0 / np.pi))
    ca = np.float32(np.sqrt(2.0 / np.pi) * 0.044715)
    t = jnp.tanh(x * (c + ca * (x * x)))
    hx = 0.5 * x
    return hx + hx * t


def _silu(x):
    return x * (1.0 / (1.0 + jnp.exp(-x)))


def _log_sigmoid(x):
    return jnp.minimum(x, 0.0) - jnp.log(1.0 + jnp.exp(-jnp.abs(x)))


def _staggered(gens, stagger):
    live = [True] * len(gens)
    t = 0
    while any(live):
        for i, gen in enumerate(gens):
            if live[i] and t >= i * stagger:
                try:
                    next(gen)
                except StopIteration:
                    live[i] = False
        t += 1
        yield


def _ffn_rows(x_ref, g_ref, wg_ref, wu_ref, wo_ref, acc_ref, rows, finish):
    h = _rmsnorm(x_ref[rows, :], g_ref[...]).astype(BF)
    yield
    for j in range(N_FF_CHUNKS):
        cs = slice(j * FF_CHUNK, (j + 1) * FF_CHUNK)
        gate = _dot(h, wg_ref[:, cs])
        up = _dot(h, wu_ref[:, cs])
        act = (_silu(gate) * up).astype(BF)
        part = _dot(act, wo_ref[cs, :])
        if j == 0:
            acc_ref[rows, :] = part
        else:
            acc_ref[rows, :] += part
        yield
    finish(x_ref[rows, :], acc_ref[rows, :])


def _ffn_kernel(x_ref, g_ref, wg_ref, wu_ref, wo_ref, gf_ref, o_ref, acc_ref, *, final_norm):
    def half(i):
        rows = slice(i * FFN_HALF, (i + 1) * FFN_HALF)

        def finish(x, acc):
            y = x + 0.5 * acc
            o_ref[rows, :] = _rmsnorm(y, gf_ref[...]) if final_norm else y

        return _ffn_rows(x_ref, g_ref, wg_ref, wu_ref, wo_ref, acc_ref, rows, finish)

    for _ in _staggered([half(i) for i in range(FFN_TILE // FFN_HALF)], 1):
        pass


def _resident(shape):
    nd = len(shape)
    return pl.BlockSpec(shape, lambda *_: (0,) * nd, pipeline_mode=pl.Buffered(1))


def _ffn_call(x, g, wg, wu, wo, gf, final_norm, name):
    t = x.shape[0]
    row = lambda i: (i, 0)
    return pl.pallas_call(
        functools.partial(_ffn_kernel, final_norm=final_norm),
        grid=(t // FFN_TILE,),
        in_specs=[
            pl.BlockSpec((FFN_TILE, D_MODEL), row),
            _resident((1, D_MODEL)),
            _resident((D_MODEL, D_FF)),
            _resident((D_MODEL, D_FF)),
            _resident((D_FF, D_MODEL)),
            _resident((1, D_MODEL)),
        ],
        out_specs=pl.BlockSpec((FFN_TILE, D_MODEL), row),
        out_shape=jax.ShapeDtypeStruct((t, D_MODEL), F32),
        scratch_shapes=[pltpu.VMEM((FFN_TILE, D_MODEL), F32)],
        compiler_params=pltpu.CompilerParams(
            dimension_semantics=("arbitrary",), vmem_limit_bytes=VMEM_LIMIT),
        name=name,
    )(x, g, wg, wu, wo, gf)


def _cum_matrices(chunk):
    r = np.arange(ROWS)[:, None]
    k = np.arange(ROWS)[None, :]
    same = (r // chunk) == (k // chunk)
    m_cum = same & (k <= r)
    m_ref = same & ((k % chunk) <= (chunk // 2))
    m_last = same
    m = np.concatenate([m_cum, m_ref, m_last], axis=0).astype(np.float32)
    return np.concatenate([m, m], axis=1)


def _mix_block(cols, chunk, consts, get_state, put_state, emit):
    (wmix_ref, bias_ref, vnorm_ref, anorm_ref, wa2_ref, ba_ref, onorm_ref, cum_ref) = consts
    n_chunks = ROWS // chunk
    shift = int(np.log2(chunk))

    z = _dot(cols(OFF_LR, LANES).astype(BF), wa2_ref[...]) + ba_ref[...]
    la = _log_sigmoid(z) * (1.0 / B_GATE_NORMALIZER)
    la_hi = la.astype(BF)
    la_lo = (la - la_hi.astype(F32)).astype(BF)
    sums = _dot(cum_ref[...], jnp.concatenate([la_hi, la_lo], axis=0))
    b = sums[0:ROWS]
    b_ref = sums[ROWS:2 * ROWS]
    b_last = sums[2 * ROWS:3 * ROWS]

    ya = []
    va = []
    ssq = None
    for g in range(A_GROUPS):
        gs = slice(g * LANES, (g + 1) * LANES)
        u = _gelu_tanh(cols(OFF_U + g * LANES, LANES))
        v = _rmsnorm(_gelu_tanh(cols(OFF_V + g * LANES, LANES)), vnorm_ref[:, gs])
        zg = _dot(wmix_ref[g], v.astype(BF)) + bias_ref[g]
        yg = u * zg
        s = jnp.sum(yg * yg, axis=-1, keepdims=True)
        ssq = s if ssq is None else ssq + s
        ya.append(yg)
        va.append(v)
        yield
    inv = lax.rsqrt(ssq * (1.0 / A_WIDTH) + EPS)
    ya = [(ya[g] * inv) * anorm_ref[:, g * LANES:(g + 1) * LANES] for g in range(A_GROUPS)]
    gate = [_silu(cols(OFF_R + h * B_DV, B_DV)) for h in range(B_HEADS)]
    yield

    ri = lax.broadcasted_iota(jnp.int32, (ROWS, ROWS), 0)
    ci = lax.broadcasted_iota(jnp.int32, (ROWS, ROWS), 1)
    rchunk = lax.shift_right_logical(ri, shift)
    cchunk = lax.shift_right_logical(ci, shift)
    causal = (rchunk == cchunk) & (ci <= ri)
    lane = lax.broadcasted_iota(jnp.int32, (ROWS, LANES), 1)
    head0 = lane < B_DK

    yb = []
    for pair in range(B_HEADS // 2):
        ls = slice(pair * LANES, (pair + 1) * LANES)
        q = cols(OFF_Q + pair * LANES, LANES) * np.float32(B_DK ** -0.5)
        k = cols(OFF_K + pair * LANES, LANES)
        vv_bf = cols(OFF_VB + pair * 2 * B_DV, 2 * B_DV).astype(BF)
        bp, brp, blp = b[:, ls], b_ref[:, ls], b_last[:, ls]
        qs = q * jnp.exp(bp - brp)
        ks = (k * jnp.exp(brp - bp)).astype(BF)
        kl_t = (k * jnp.exp(blp - bp)).T
        bl_t = blp.T
        qb = (q * jnp.exp(bp)).astype(BF)
        yield

        o_heads = []
        for hh in range(2):
            msk = head0 if hh == 0 else jnp.logical_not(head0)
            qh = jnp.where(msk, qs, 0.0).astype(BF)
            sc = jnp.where(causal, _dot_nt(qh, ks), 0.0).astype(BF)
            o_heads.append(_dot(sc, vv_bf[:, hh * B_DV:(hh + 1) * B_DV]))
        o = jnp.concatenate(o_heads, axis=1)

        o_inter = []
        for j in range(n_chunks):
            s_prev = get_state(pair, j)
            o_inter.append(_dot(qb[j * chunk:(j + 1) * chunk], _blockdiag_bf16(*s_prev)))
            in_chunk = cchunk == j
            upd = _dot(jnp.where(in_chunk, kl_t, 0.0).astype(BF), vv_bf)
            dec = jnp.exp(bl_t[:, j * chunk:j * chunk + 1])
            put_state(pair, j, tuple(
                dec[hh * B_DK:(hh + 1) * B_DK] * s_prev[hh]
                + upd[hh * B_DK:(hh + 1) * B_DK, hh * B_DV:(hh + 1) * B_DV] for hh in range(2)))
        o = o + jnp.concatenate(o_inter, axis=0)

        for hh in range(2):
            h_idx = 2 * pair + hh
            hs = slice(h_idx * B_DV, (h_idx + 1) * B_DV)
            oh = _rmsnorm(o[:, hh * B_DV:(hh + 1) * B_DV], onorm_ref[:, hs])
            yb.append(oh * gate[h_idx])
        yield

    emit(jnp.concatenate(ya + yb, axis=1).astype(BF), jnp.concatenate(va, axis=1))


def _blockdiag_bf16(s0, s1):
    s0 = s0.astype(BF)
    s1 = s1.astype(BF)
    zero = jnp.zeros_like(s0)
    return jnp.concatenate(
        [jnp.concatenate([s0, zero], axis=1), jnp.concatenate([zero, s1], axis=1)], axis=0)


def _in_proj(x1_ref, g2_ref, wp_ref, p_scr, g, rows):
    h = _rmsnorm(x1_ref[rows, :], g2_ref[...]).astype(BF)
    for c0 in range(0, P_COLS, MXU_DIM):
        cs = slice(c0, min(c0 + MXU_DIM, P_COLS))
        p_scr[g, :, cs] = _dot(h, wp_ref[:, cs])


def _out_proj(x1_ref, y_scr, wout_ref, x2_ref, g, rows):
    y = y_scr[g]
    for c0 in range(0, D_MODEL, MXU_DIM):
        cs = slice(c0, c0 + MXU_DIM)
        x2_ref[rows, cs] = x1_ref[rows, cs] + _dot(y, wout_ref[:, cs])


def _mixproj_groups(x1_ref, g2_ref, wp_ref, wout_ref, x2_ref, p_scr, y_scr, group, n_groups,
                    mix_group):
    rows = [slice(g * group, (g + 1) * group) for g in range(n_groups)]
    for g in range(n_groups):
        _in_proj(x1_ref, g2_ref, wp_ref, p_scr, g, rows[g])
    for g in range(n_groups):
        mix_group(g)
    for g in range(n_groups):
        _out_proj(x1_ref, y_scr, wout_ref, x2_ref, g, rows[g])


def _block_cols(p_scr, g, i):
    return lambda off, width: p_scr[g, i * ROWS:(i + 1) * ROWS, off:off + width]


def _run_blocks(blocks):
    for _ in _staggered(blocks, MIX_STAGGER):
        pass


def _mixproj_prompt_kernel(x1_ref, g2_ref, wp_ref, wout_ref, wmix_ref, bias_ref, vnorm_ref,
                           anorm_ref, wa2_ref, ba_ref, onorm_ref, cum_ref, x2_ref, st_ref,
                           p_scr, y_scr, s_scr, *, group, n_groups):
    @pl.when(pl.program_id(1) == 0)
    def _():
        s_scr[...] = jnp.zeros_like(s_scr)

    consts = (wmix_ref, bias_ref, vnorm_ref, anorm_ref, wa2_ref, ba_ref, onorm_ref, cum_ref)
    n_blk = group // ROWS
    cur = {pair: (s_scr[2 * pair], s_scr[2 * pair + 1]) for pair in range(B_HEADS // 2)}

    def get_state(pair, j):
        return cur[pair]

    def put_state(pair, j, s):
        cur[pair] = s

    def mix_group(g):
        def block(i):
            def emit(y, va):
                y_scr[g, i * ROWS:(i + 1) * ROWS, :] = y
            return _mix_block(_block_cols(p_scr, g, i), B_CHUNK, consts, get_state, put_state, emit)

        _run_blocks([block(i) for i in range(n_blk)])

    _mixproj_groups(x1_ref, g2_ref, wp_ref, wout_ref, x2_ref, p_scr, y_scr, group, n_groups,
                    mix_group)

    for pair in range(B_HEADS // 2):
        for hh in range(2):
            s_scr[2 * pair + hh] = cur[pair][hh]
            st_ref[0, 2 * pair + hh] = cur[pair][hh]


def _mixproj_sample_kernel(x1_ref, s0_ref, g2_ref, wp_ref, wout_ref, wmix_ref, bias_ref, vnorm_ref,
                           anorm_ref, wa2_ref, ba_ref, onorm_ref, cum_ref, x2_ref, va_ref, st_ref,
                           p_scr, y_scr, *, chunk, group, n_groups):
    consts = (wmix_ref, bias_ref, vnorm_ref, anorm_ref, wa2_ref, ba_ref, onorm_ref, cum_ref)
    n_blk = group // ROWS
    per_blk = ROWS // chunk

    def mix_group(g):
        def block(i):
            blk = g * n_blk + i

            def get_state(pair, j):
                n = blk * per_blk + j
                return s0_ref[n, 2 * pair], s0_ref[n, 2 * pair + 1]

            def put_state(pair, j, s):
                n = blk * per_blk + j
                st_ref[n, 2 * pair] = s[0]
                st_ref[n, 2 * pair + 1] = s[1]

            def emit(y, va):
                y_scr[g, i * ROWS:(i + 1) * ROWS, :] = y
                va_ref[blk * ROWS:(blk + 1) * ROWS, :] = va

            return _mix_block(_block_cols(p_scr, g, i), chunk, consts, get_state, put_state, emit)

        _run_blocks([block(i) for i in range(n_blk)])

    _mixproj_groups(x1_ref, g2_ref, wp_ref, wout_ref, x2_ref, p_scr, y_scr, group, n_groups,
                    mix_group)


def _mix_const_specs():
    return [
        _resident((A_GROUPS, ROWS, ROWS)),
        _resident((A_GROUPS, ROWS, LANES)),
        _resident((1, A_WIDTH)),
        _resident((1, A_WIDTH)),
        _resident((LANES, B_HEADS * B_DK)),
        _resident((1, B_HEADS * B_DK)),
        _resident((1, B_HEADS * B_DV)),
        _resident((3 * ROWS, 2 * ROWS)),
    ]


def _proj_specs():
    return [
        _resident((1, D_MODEL)),
        _resident((D_MODEL, P_COLS)),
        _resident((D_MODEL, D_MODEL)),
    ]


def _mixproj_scratch(group, n_groups):
    return [pltpu.VMEM((n_groups, group, P_COLS), F32),
            pltpu.VMEM((n_groups, group, D_MODEL), BF)]


def _mixproj_prompt_call(x1, g2, wp, wout, consts, batch, seq, tile, group):
    n_tiles = seq // tile
    n_groups = tile // group
    row = lambda b, s: (b * n_tiles + s, 0)
    return pl.pallas_call(
        functools.partial(_mixproj_prompt_kernel, group=group, n_groups=n_groups),
        grid=(batch, n_tiles),
        in_specs=[pl.BlockSpec((tile, D_MODEL), row)] + _proj_specs() + _mix_const_specs(),
        out_specs=[
            pl.BlockSpec((tile, D_MODEL), row),
            pl.BlockSpec((1, B_HEADS, B_DK, B_DV), lambda b, s: (b, 0, 0, 0)),
        ],
        out_shape=[
            jax.ShapeDtypeStruct((batch * seq, D_MODEL), F32),
            jax.ShapeDtypeStruct((batch, B_HEADS, B_DK, B_DV), F32),
        ],
        scratch_shapes=_mixproj_scratch(group, n_groups)
        + [pltpu.VMEM((B_HEADS, B_DK, B_DV), F32)],
        compiler_params=pltpu.CompilerParams(
            dimension_semantics=("arbitrary", "arbitrary"), vmem_limit_bytes=VMEM_LIMIT),
        name="mixproj_prompt",
    )(x1, g2, wp, wout, *consts)


def _mixproj_sample_call(x1, s0, g2, wp, wout, consts, n_seq, seq, tile, group):
    seq_per_step = tile // seq
    n_groups = tile // group
    row = lambda i: (i, 0)
    state = lambda i: (i, 0, 0, 0)
    return pl.pallas_call(
        functools.partial(_mixproj_sample_kernel, chunk=seq, group=group, n_groups=n_groups),
        grid=(n_seq // seq_per_step,),
        in_specs=[
            pl.BlockSpec((tile, D_MODEL), row),
            pl.BlockSpec((seq_per_step, B_HEADS, B_DK, B_DV), state),
        ] + _proj_specs() + _mix_const_specs(),
        out_specs=[
            pl.BlockSpec((tile, D_MODEL), row),
            pl.BlockSpec((tile, A_WIDTH), row),
            pl.BlockSpec((seq_per_step, B_HEADS, B_DK, B_DV), state),
        ],
        out_shape=[
            jax.ShapeDtypeStruct((n_seq * seq, D_MODEL), F32),
            jax.ShapeDtypeStruct((n_seq * seq, A_WIDTH), F32),
            jax.ShapeDtypeStruct((n_seq, B_HEADS, B_DK, B_DV), F32),
        ],
        scratch_shapes=_mixproj_scratch(group, n_groups),
        compiler_params=pltpu.CompilerParams(
            dimension_semantics=("arbitrary",), vmem_limit_bytes=VMEM_LIMIT),
        name="mixproj_sample",
    )(x1, s0, g2, wp, wout, *consts)


def _mix_consts(a_ws, a_bs, a_vnorm, a_onorm, b_wa2, b_ba, b_onorm, chunk, seq_len):
    n_rep = ROWS // seq_len
    tril = jnp.tril(jnp.ones((seq_len, seq_len), dtype=bool))
    w = jnp.where(tril[None], a_ws[:, :seq_len, :seq_len], 0.0)
    if n_rep > 1:
        w = jax.vmap(lambda wg: jnp.kron(jnp.eye(n_rep, dtype=wg.dtype), wg))(w)
    bias = jnp.tile(a_bs[:, :seq_len], (1, n_rep))
    bias = jnp.broadcast_to(bias[:, :, None], (A_GROUPS, ROWS, LANES))
    wa2 = jnp.zeros((LANES, B_HEADS * B_DK), F32).at[:B_LOWRANK].set(b_wa2)
    return (
        w.astype(BF),
        bias,
        a_vnorm.reshape(1, A_WIDTH),
        a_onorm.reshape(1, A_WIDTH),
        wa2.astype(BF),
        b_ba.reshape(1, B_HEADS * B_DK),
        b_onorm.reshape(1, B_HEADS * B_DV),
        jnp.asarray(_cum_matrices(chunk), dtype=BF),
    )


def kernel(x_prompt, x_sample, state_gla, ffn1_norm, ffn1_w_in, ffn1_w_out, mix_norm, w_in,
           a_ws, a_bs, a_vnorm, a_onorm, b_wa2, b_ba, b_onorm, w_out, ffn2_norm, ffn2_w_in,
           ffn2_w_out, final_norm):
    batch, seq, _ = x_prompt.shape
    n_dec, dec_seq, _ = x_sample.shape
    assert ffn1_norm.shape[0] == 1, "single-layer kernel"
    assert seq % A_CHUNK == 0 and ROWS % dec_seq == 0 and dec_seq <= B_CHUNK
    assert (batch * seq) % FFN_TILE == 0 and (n_dec * dec_seq) % FFN_TILE == 0

    g1 = ffn1_norm[0].reshape(1, D_MODEL)
    wg1 = ffn1_w_in[0][:, :D_FF].astype(BF)
    wu1 = ffn1_w_in[0][:, D_FF:].astype(BF)
    wo1 = ffn1_w_out[0].astype(BF)
    g2 = mix_norm[0].reshape(1, D_MODEL)
    wp = jnp.pad(w_in[0], ((0, 0), (0, P_COLS - IN_COLS))).astype(BF)
    wout = w_out[0].astype(BF)
    g3 = ffn2_norm[0].reshape(1, D_MODEL)
    wg2 = ffn2_w_in[0][:, :D_FF].astype(BF)
    wu2 = ffn2_w_in[0][:, D_FF:].astype(BF)
    wo2 = ffn2_w_out[0].astype(BF)
    gf = final_norm.reshape(1, D_MODEL)
    mix_args = (a_ws[0], a_bs[0], a_vnorm[0], a_onorm[0], b_wa2[0], b_ba[0], b_onorm[0])
    consts_p = _mix_consts(*mix_args, chunk=B_CHUNK, seq_len=A_CHUNK)
    consts_s = _mix_consts(*mix_args, chunk=dec_seq, seq_len=dec_seq)

    xp = x_prompt.reshape(batch * seq, D_MODEL)
    xs = x_sample.reshape(n_dec * dec_seq, D_MODEL)

    x1p = _ffn_call(xp, g1, wg1, wu1, wo1, gf, False, "ffn1")
    x1s = _ffn_call(xs, g1, wg1, wu1, wo1, gf, False, "ffn1")

    x2p, st_p = _mixproj_prompt_call(x1p, g2, wp, wout, consts_p, batch, seq, tile=1024, group=512)
    x2s, va_s, st_s = _mixproj_sample_call(x1s, state_gla[0], g2, wp, wout, consts_s, n_dec,
                                           dec_seq, tile=256, group=128)

    out_p = _ffn_call(x2p, g3, wg2, wu2, wo2, gf, True, "ffn2")
    out_s = _ffn_call(x2s, g3, wg2, wu2, wo2, gf, True, "ffn2")

    return (
        out_p.reshape(batch, seq, D_MODEL),
        out_s.reshape(n_dec, dec_seq, D_MODEL),
        st_p[None],
        st_s[None],
        va_s.reshape(1, n_dec, dec_seq, A_WIDTH),
    )
```

```python
import functools

import numpy as np
import jax
import jax.numpy as jnp
from jax import lax
from jax.experimental import pallas as pl
from jax.experimental.pallas import tpu as pltpu

D_MODEL = 1024
D_FF = 2816
A_WIDTH = 512
A_GROUPS = 4
A_CHUNK = 128
B_HEADS = 4
B_DK = 64
B_DV = 128
B_LOWRANK = 16
B_GATE_NORMALIZER = 16.0
B_CHUNK = 64
EPS = 1e-6

OFF_U, OFF_V, OFF_Q, OFF_K, OFF_VB, OFF_R, OFF_LR = 0, 512, 1024, 1280, 1536, 2048, 2560
IN_COLS = 2576
P_COLS = 2688

LANES = 128
MXU_DIM = 256
ROWS = 128
FF_CHUNK = MXU_DIM
N_FF_CHUNKS = D_FF // FF_CHUNK
VMEM_LIMIT = 56 * 1024 * 1024

FFN_TILE = 1024
FFN_HALF = FFN_TILE // 2
MIX_STAGGER = 6

BF = jnp.bfloat16
F32 = jnp.float32


def _dot(a, b):
    return jnp.dot(a, b, preferred_element_type=F32)


def _dot_nt(a, b):
    return lax.dot_general(a, b, (((1,), (1,)), ((), ())), preferred_element_type=F32)


def _rmsnorm(x, g):
    return (x * lax.rsqrt(jnp.mean(x * x, axis=-1, keepdims=True) + EPS)) * g


def _gelu_tanh(x):
    c = np.float32(np.sqrt(2.0 / np.pi))
    ca = np.float32(np.sqrt(2.0 / np.pi) * 0.044715)
    t = jnp.tanh(x * (c + ca * (x * x)))
    hx = 0.5 * x
    return hx + hx * t


def _silu(x):
    return x * (1.0 / (1.0 + jnp.exp(-x)))


def _log_sigmoid(x):
    return jnp.minimum(x, 0.0) - jnp.log(1.0 + jnp.exp(-jnp.abs(x)))


def _staggered(gens, stagger):
    live = [True] * len(gens)
    t = 0
    while any(live):
        for i, gen in enumerate(gens):
            if live[i] and t >= i * stagger:
                try:
                    next(gen)
                except StopIteration:
                    live[i] = False
        t += 1
        yield


def _ffn_rows(x_ref, g_ref, wi_ref, wo_ref, acc_ref, rows, finish):
    h = _rmsnorm(x_ref[rows, :], g_ref[...]).astype(BF)
    yield
    for j in range(N_FF_CHUNKS):
        cs = slice(j * FF_CHUNK, (j + 1) * FF_CHUNK)
        gate = _dot(h, wi_ref[:, cs])
        up = _dot(h, wi_ref[:, D_FF + j * FF_CHUNK:D_FF + (j + 1) * FF_CHUNK])
        act = (_silu(gate) * up).astype(BF)
        part = _dot(act, wo_ref[cs, :])
        if j == 0:
            acc_ref[rows, :] = part
        else:
            acc_ref[rows, :] += part
        yield
    finish(x_ref[rows, :], acc_ref[rows, :])


def _ffn_kernel(x_ref, g_ref, wi_ref, wo_ref, gf_ref, o_ref, acc_ref, *, final_norm):
    def half(i):
        rows = slice(i * FFN_HALF, (i + 1) * FFN_HALF)

        def finish(x, acc):
            y = x + 0.5 * acc
            o_ref[rows, :] = _rmsnorm(y, gf_ref[...]) if final_norm else y

        return _ffn_rows(x_ref, g_ref, wi_ref, wo_ref, acc_ref, rows, finish)

    for _ in _staggered([half(i) for i in range(FFN_TILE // FFN_HALF)], 1):
        pass


def _resident(shape):
    nd = len(shape)
    return pl.BlockSpec(shape, lambda *_: (0,) * nd, pipeline_mode=pl.Buffered(1))


def _ffn_call(x, g, wi, wo, gf, final_norm, name):
    t = x.shape[0]
    row = lambda i: (i, 0)
    return pl.pallas_call(
        functools.partial(_ffn_kernel, final_norm=final_norm),
        grid=(t // FFN_TILE,),
        in_specs=[
            pl.BlockSpec((FFN_TILE, D_MODEL), row),
            _resident((1, D_MODEL)),
            _resident((D_MODEL, 2 * D_FF)),
            _resident((D_FF, D_MODEL)),
            _resident((1, D_MODEL)),
        ],
        out_specs=pl.BlockSpec((FFN_TILE, D_MODEL), row),
        out_shape=jax.ShapeDtypeStruct((t, D_MODEL), F32),
        scratch_shapes=[pltpu.VMEM((FFN_TILE, D_MODEL), F32)],
        compiler_params=pltpu.CompilerParams(
            dimension_semantics=("arbitrary",), vmem_limit_bytes=VMEM_LIMIT),
        name=name,
    )(x, g, wi, wo, gf)


def _cum_matrices(chunk):
    r = np.arange(ROWS)[:, None]
    k = np.arange(ROWS)[None, :]
    same = (r // chunk) == (k // chunk)
    m_cum = same & (k <= r)
    m_ref = same & ((k % chunk) <= (chunk // 2))
    m_last = same
    m = np.concatenate([m_cum, m_ref, m_last], axis=0).astype(np.float32)
    return np.concatenate([m, m], axis=1)


def _mix_block(cols, chunk, consts, get_state, put_state, emit):
    (wmix_ref, bias_ref, vnorm_ref, anorm_ref, wa2_ref, ba_ref, onorm_ref, cum_ref) = consts
    n_chunks = ROWS // chunk
    shift = int(np.log2(chunk))

    z = _dot(cols(OFF_LR, LANES).astype(BF), wa2_ref[...]) + ba_ref[...]
    la = _log_sigmoid(z) * (1.0 / B_GATE_NORMALIZER)
    la_hi = la.astype(BF)
    la_lo = (la - la_hi.astype(F32)).astype(BF)
    sums = _dot(cum_ref[...], jnp.concatenate([la_hi, la_lo], axis=0))
    b = sums[0:ROWS]
    b_ref = sums[ROWS:2 * ROWS]
    b_last = sums[2 * ROWS:3 * ROWS]

    ya = []
    va = []
    ssq = None
    for g in range(A_GROUPS):
        gs = slice(g * LANES, (g + 1) * LANES)
        u = _gelu_tanh(cols(OFF_U + g * LANES, LANES))
        v = _rmsnorm(_gelu_tanh(cols(OFF_V + g * LANES, LANES)), vnorm_ref[:, gs])
        zg = _dot(wmix_ref[g], v.astype(BF)) + bias_ref[g]
        yg = u * zg
        s = jnp.sum(yg * yg, axis=-1, keepdims=True)
        ssq = s if ssq is None else ssq + s
        ya.append(yg)
        va.append(v)
        yield
    inv = lax.rsqrt(ssq * (1.0 / A_WIDTH) + EPS)
    ya = [(ya[g] * inv) * anorm_ref[:, g * LANES:(g + 1) * LANES] for g in range(A_GROUPS)]
    gate = [_silu(cols(OFF_R + h * B_DV, B_DV)) for h in range(B_HEADS)]
    yield

    ri = lax.broadcasted_iota(jnp.int32, (ROWS, ROWS), 0)
    ci = lax.broadcasted_iota(jnp.int32, (ROWS, ROWS), 1)
    rchunk = lax.shift_right_logical(ri, shift)
    cchunk = lax.shift_right_logical(ci, shift)
    causal = (rchunk == cchunk) & (ci <= ri)
    lane = lax.broadcasted_iota(jnp.int32, (ROWS, LANES), 1)
    head0 = lane < B_DK

    yb = []
    for pair in range(B_HEADS // 2):
        ls = slice(pair * LANES, (pair + 1) * LANES)
        q = cols(OFF_Q + pair * LANES, LANES) * np.float32(B_DK ** -0.5)
        k = cols(OFF_K + pair * LANES, LANES)
        vv_bf = cols(OFF_VB + pair * 2 * B_DV, 2 * B_DV).astype(BF)
        bp, brp, blp = b[:, ls], b_ref[:, ls], b_last[:, ls]
        qs = q * jnp.exp(bp - brp)
        ks = (k * jnp.exp(brp - bp)).astype(BF)
        kl_t = (k * jnp.exp(blp - bp)).T
        bl_t = blp.T
        qb = (q * jnp.exp(bp)).astype(BF)
        yield

        o_heads = []
        for hh in range(2):
            msk = head0 if hh == 0 else jnp.logical_not(head0)
            qh = jnp.where(msk, qs, 0.0).astype(BF)
            sc = jnp.where(causal, _dot_nt(qh, ks), 0.0).astype(BF)
            o_heads.append(_dot(sc, vv_bf[:, hh * B_DV:(hh + 1) * B_DV]))
        o = jnp.concatenate(o_heads, axis=1)

        o_inter = []
        for j in range(n_chunks):
            s_prev = get_state(pair, j)
            o_inter.append(_dot(qb[j * chunk:(j + 1) * chunk], _blockdiag_bf16(*s_prev)))
            in_chunk = cchunk == j
            upd = _dot(jnp.where(in_chunk, kl_t, 0.0).astype(BF), vv_bf)
            dec = jnp.exp(bl_t[:, j * chunk:j * chunk + 1])
            put_state(pair, j, tuple(
                dec[hh * B_DK:(hh + 1) * B_DK] * s_prev[hh]
                + upd[hh * B_DK:(hh + 1) * B_DK, hh * B_DV:(hh + 1) * B_DV] for hh in range(2)))
        o = o + jnp.concatenate(o_inter, axis=0)

        for hh in range(2):
            h_idx = 2 * pair + hh
            hs = slice(h_idx * B_DV, (h_idx + 1) * B_DV)
            oh = _rmsnorm(o[:, hh * B_DV:(hh + 1) * B_DV], onorm_ref[:, hs])
            yb.append(oh * gate[h_idx])
        yield

    emit(jnp.concatenate(ya + yb, axis=1).astype(BF), jnp.concatenate(va, axis=1))


def _blockdiag_bf16(s0, s1):
    s0 = s0.astype(BF)
    s1 = s1.astype(BF)
    zero = jnp.zeros_like(s0)
    return jnp.concatenate(
        [jnp.concatenate([s0, zero], axis=1), jnp.concatenate([zero, s1], axis=1)], axis=0)


def _in_proj(x1_ref, g2_ref, wp_ref, p_scr, g, rows):
    h = _rmsnorm(x1_ref[rows, :], g2_ref[...]).astype(BF)
    for c0 in range(0, P_COLS, MXU_DIM):
        cs = slice(c0, min(c0 + MXU_DIM, P_COLS))
        p_scr[g, :, cs] = _dot(h, wp_ref[:, cs])


def _out_proj(x1_ref, y_scr, wout_ref, x2_ref, g, rows):
    y = y_scr[g]
    for c0 in range(0, D_MODEL, MXU_DIM):
        cs = slice(c0, c0 + MXU_DIM)
        x2_ref[rows, cs] = x1_ref[rows, cs] + _dot(y, wout_ref[:, cs])


def _mixproj_tile(x1_ref, g2_ref, wp_ref, wout_ref, x2_ref, p_scr, y_scr, group, n_groups,
                  make_block):
    rows = [slice(g * group, (g + 1) * group) for g in range(n_groups)]
    for g in range(n_groups):
        _in_proj(x1_ref, g2_ref, wp_ref, p_scr, g, rows[g])
    blocks = [make_block(g, i) for g in range(n_groups) for i in range(group // ROWS)]
    for _ in _staggered(blocks, MIX_STAGGER):
        pass
    for g in range(n_groups):
        _out_proj(x1_ref, y_scr, wout_ref, x2_ref, g, rows[g])


def _block_cols(p_scr, g, i):
    return lambda off, width: p_scr[g, i * ROWS:(i + 1) * ROWS, off:off + width]


def _mixproj_prompt_kernel(x1_ref, g2_ref, wp_ref, wout_ref, wmix_ref, bias_ref, vnorm_ref,
                           anorm_ref, wa2_ref, ba_ref, onorm_ref, cum_ref, x2_ref, st_ref,
                           p_scr, y_scr, s_scr, *, group, n_groups):
    @pl.when(pl.program_id(1) == 0)
    def _():
        s_scr[...] = jnp.zeros_like(s_scr)

    consts = (wmix_ref, bias_ref, vnorm_ref, anorm_ref, wa2_ref, ba_ref, onorm_ref, cum_ref)
    cur = {pair: (s_scr[2 * pair], s_scr[2 * pair + 1]) for pair in range(B_HEADS // 2)}

    def get_state(pair, j):
        return cur[pair]

    def put_state(pair, j, s):
        cur[pair] = s

    def make_block(g, i):
        def emit(y, va):
            y_scr[g, i * ROWS:(i + 1) * ROWS, :] = y
        return _mix_block(_block_cols(p_scr, g, i), B_CHUNK, consts, get_state, put_state, emit)

    _mixproj_tile(x1_ref, g2_ref, wp_ref, wout_ref, x2_ref, p_scr, y_scr, group, n_groups,
                  make_block)

    for pair in range(B_HEADS // 2):
        for hh in range(2):
            s_scr[2 * pair + hh] = cur[pair][hh]
            st_ref[0, 2 * pair + hh] = cur[pair][hh]


def _mixproj_sample_kernel(x1_ref, s0_ref, g2_ref, wp_ref, wout_ref, wmix_ref, bias_ref, vnorm_ref,
                           anorm_ref, wa2_ref, ba_ref, onorm_ref, cum_ref, x2_ref, va_ref, st_ref,
                           p_scr, y_scr, *, chunk, group, n_groups):
    consts = (wmix_ref, bias_ref, vnorm_ref, anorm_ref, wa2_ref, ba_ref, onorm_ref, cum_ref)
    n_blk = group // ROWS
    per_blk = ROWS // chunk

    def make_block(g, i):
        blk = g * n_blk + i

        def get_state(pair, j):
            n = blk * per_blk + j
            return s0_ref[n, 2 * pair], s0_ref[n, 2 * pair + 1]

        def put_state(pair, j, s):
            n = blk * per_blk + j
            st_ref[n, 2 * pair] = s[0]
            st_ref[n, 2 * pair + 1] = s[1]

        def emit(y, va):
            y_scr[g, i * ROWS:(i + 1) * ROWS, :] = y
            va_ref[blk * ROWS:(blk + 1) * ROWS, :] = va

        return _mix_block(_block_cols(p_scr, g, i), chunk, consts, get_state, put_state, emit)

    _mixproj_tile(x1_ref, g2_ref, wp_ref, wout_ref, x2_ref, p_scr, y_scr, group, n_groups,
                  make_block)


def _mix_const_specs():
    return [
        _resident((A_GROUPS, ROWS, ROWS)),
        _resident((A_GROUPS, ROWS, LANES)),
        _resident((1, A_WIDTH)),
        _resident((1, A_WIDTH)),
        _resident((LANES, B_HEADS * B_DK)),
        _resident((1, B_HEADS * B_DK)),
        _resident((1, B_HEADS * B_DV)),
        _resident((3 * ROWS, 2 * ROWS)),
    ]


def _proj_specs():
    return [
        _resident((1, D_MODEL)),
        _resident((D_MODEL, P_COLS)),
        _resident((D_MODEL, D_MODEL)),
    ]


def _mixproj_scratch(group, n_groups):
    return [pltpu.VMEM((n_groups, group, P_COLS), F32),
            pltpu.VMEM((n_groups, group, D_MODEL), BF)]


def _mixproj_prompt_call(x1, g2, wp, wout, consts, batch, seq, tile, group):
    n_tiles = seq // tile
    n_groups = tile // group
    row = lambda b, s: (b * n_tiles + s, 0)
    return pl.pallas_call(
        functools.partial(_mixproj_prompt_kernel, group=group, n_groups=n_groups),
        grid=(batch, n_tiles),
        in_specs=[pl.BlockSpec((tile, D_MODEL), row)] + _proj_specs() + _mix_const_specs(),
        out_specs=[
            pl.BlockSpec((tile, D_MODEL), row),
            pl.BlockSpec((1, B_HEADS, B_DK, B_DV), lambda b, s: (b, 0, 0, 0)),
        ],
        out_shape=[
            jax.ShapeDtypeStruct((batch * seq, D_MODEL), F32),
            jax.ShapeDtypeStruct((batch, B_HEADS, B_DK, B_DV), F32),
        ],
        scratch_shapes=_mixproj_scratch(group, n_groups)
        + [pltpu.VMEM((B_HEADS, B_DK, B_DV), F32)],
        compiler_params=pltpu.CompilerParams(
            dimension_semantics=("arbitrary", "arbitrary"), vmem_limit_bytes=VMEM_LIMIT),
        name="mixproj_prompt",
    )(x1, g2, wp, wout, *consts)


def _mixproj_sample_call(x1, s0, g2, wp, wout, consts, n_seq, seq, tile, group):
    seq_per_step = tile // seq
    n_groups = tile // group
    row = lambda i: (i, 0)
    state = lambda i: (i, 0, 0, 0)
    return pl.pallas_call(
        functools.partial(_mixproj_sample_kernel, chunk=seq, group=group, n_groups=n_groups),
        grid=(n_seq // seq_per_step,),
        in_specs=[
            pl.BlockSpec((tile, D_MODEL), row),
            pl.BlockSpec((seq_per_step, B_HEADS, B_DK, B_DV), state),
        ] + _proj_specs() + _mix_const_specs(),
        out_specs=[
            pl.BlockSpec((tile, D_MODEL), row),
            pl.BlockSpec((tile, A_WIDTH), row),
            pl.BlockSpec((seq_per_step, B_HEADS, B_DK, B_DV), state),
        ],
        out_shape=[
            jax.ShapeDtypeStruct((n_seq * seq, D_MODEL), F32),
            jax.ShapeDtypeStruct((n_seq * seq, A_WIDTH), F32),
            jax.ShapeDtypeStruct((n_seq, B_HEADS, B_DK, B_DV), F32),
        ],
        scratch_shapes=_mixproj_scratch(group, n_groups),
        compiler_params=pltpu.CompilerParams(
            dimension_semantics=("arbitrary",), vmem_limit_bytes=VMEM_LIMIT),
        name="mixproj_sample",
    )(x1, s0, g2, wp, wout, *consts)


def _mix_consts(a_ws, a_bs, a_vnorm, a_onorm, b_wa2, b_ba, b_onorm, chunk, seq_len):
    n_rep = ROWS // seq_len
    tril = jnp.tril(jnp.ones((seq_len, seq_len), dtype=bool))
    w = jnp.where(tril[None], a_ws[:, :seq_len, :seq_len], 0.0)
    if n_rep > 1:
        w = jax.vmap(lambda wg: jnp.kron(jnp.eye(n_rep, dtype=wg.dtype), wg))(w)
    bias = jnp.tile(a_bs[:, :seq_len], (1, n_rep))
    bias = jnp.broadcast_to(bias[:, :, None], (A_GROUPS, ROWS, LANES))
    wa2 = jnp.zeros((LANES, B_HEADS * B_DK), F32).at[:B_LOWRANK].set(b_wa2)
    return (
        w.astype(BF),
        bias,
        a_vnorm.reshape(1, A_WIDTH),
        a_onorm.reshape(1, A_WIDTH),
        wa2.astype(BF),
        b_ba.reshape(1, B_HEADS * B_DK),
        b_onorm.reshape(1, B_HEADS * B_DV),
        jnp.asarray(_cum_matrices(chunk), dtype=BF),
    )


def kernel(x_prompt, x_sample, state_gla, ffn1_norm, ffn1_w_in, ffn1_w_out, mix_norm, w_in,
           a_ws, a_bs, a_vnorm, a_onorm, b_wa2, b_ba, b_onorm, w_out, ffn2_norm, ffn2_w_in,
           ffn2_w_out, final_norm):
    batch, seq, _ = x_prompt.shape
    n_dec, dec_seq, _ = x_sample.shape
    assert ffn1_norm.shape[0] == 1, "single-layer kernel"
    assert seq % A_CHUNK == 0 and ROWS % dec_seq == 0 and dec_seq <= B_CHUNK
    assert (batch * seq) % FFN_TILE == 0 and (n_dec * dec_seq) % FFN_TILE == 0

    g1 = ffn1_norm[0].reshape(1, D_MODEL)
    wi1 = ffn1_w_in[0].astype(BF)
    wo1 = ffn1_w_out[0].astype(BF)
    g2 = mix_norm[0].reshape(1, D_MODEL)
    wp = jnp.pad(w_in[0], ((0, 0), (0, P_COLS - IN_COLS))).astype(BF)
    wout = w_out[0].astype(BF)
    g3 = ffn2_norm[0].reshape(1, D_MODEL)
    wi2 = ffn2_w_in[0].astype(BF)
    wo2 = ffn2_w_out[0].astype(BF)
    gf = final_norm.reshape(1, D_MODEL)
    mix_args = (a_ws[0], a_bs[0], a_vnorm[0], a_onorm[0], b_wa2[0], b_ba[0], b_onorm[0])
    consts_p = _mix_consts(*mix_args, chunk=B_CHUNK, seq_len=A_CHUNK)
    consts_s = _mix_consts(*mix_args, chunk=dec_seq, seq_len=dec_seq)

    xp = x_prompt.reshape(batch * seq, D_MODEL)
    xs = x_sample.reshape(n_dec * dec_seq, D_MODEL)

    x1p = _ffn_call(xp, g1, wi1, wo1, gf, False, "ffn1")
    x1s = _ffn_call(xs, g1, wi1, wo1, gf, False, "ffn1")

    x2p, st_p = _mixproj_prompt_call(x1p, g2, wp, wout, consts_p, batch, seq, tile=1024, group=512)
    x2s, va_s, st_s = _mixproj_sample_call(x1s, state_gla[0], g2, wp, wout, consts_s, n_dec,
                                           dec_seq, tile=256, group=128)

    out_p = _ffn_call(x2p, g3, wi2, wo2, gf, True, "ffn2")
    out_s = _ffn_call(x2s, g3, wi2, wo2, gf, True, "ffn2")

    return (
        out_p.reshape(batch, seq, D_MODEL),
        out_s.reshape(n_dec, dec_seq, D_MODEL),
        st_p[None],
        st_s[None],
        va_s.reshape(1, n_dec, dec_seq, A_WIDTH),
    )
```

```python
import functools

import numpy as np
import jax
import jax.numpy as jnp
from jax import lax
from jax.experimental import pallas as pl
from jax.experimental.pallas import tpu as pltpu

D_MODEL = 1024
D_FF = 2816
A_WIDTH = 512
A_GROUPS = 4
A_CHUNK = 128
B_HEADS = 4
B_DK = 64
B_DV = 128
B_LOWRANK = 16
B_GATE_NORMALIZER = 16.0
B_CHUNK = 64
EPS = 1e-6

OFF_U, OFF_V, OFF_Q, OFF_K, OFF_VB, OFF_R, OFF_LR = 0, 512, 1024, 1280, 1536, 2048, 2560
IN_COLS = 2576
P_COLS = 2688

LANES = 128
MXU_DIM = 256
ROWS = 128
FF_CHUNK = MXU_DIM
N_FF_CHUNKS = D_FF // FF_CHUNK
VMEM_LIMIT = 56 * 1024 * 1024

FFN_TILE = 1024
FFN_HALF = FFN_TILE // 2
WEIGHT_SLOT_ROWS = 256
MIX_STAGGER = 6

BF = jnp.bfloat16
F32 = jnp.float32


def _dot(a, b):
    return jnp.dot(a, b, preferred_element_type=F32)


def _dot_nt(a, b):
    return lax.dot_general(a, b, (((1,), (1,)), ((), ())), preferred_element_type=F32)


def _rmsnorm(x, g):
    return (x * lax.rsqrt(jnp.mean(x * x, axis=-1, keepdims=True) + EPS)) * g


def _gelu_tanh(x):
    c = np.float32(np.sqrt(2.0 / np.pi))
    ca = np.float32(np.sqrt(2.0 / np.pi) * 0.044715)
    t = jnp.tanh(x * (c + ca * (x * x)))
    hx = 0.5 * x
    return hx + hx * t


def _silu(x):
    return x * (1.0 / (1.0 + jnp.exp(-x)))


def _log_sigmoid(x):
    return jnp.minimum(x, 0.0) - jnp.log(1.0 + jnp.exp(-jnp.abs(x)))


def _staggered(gens, stagger):
    live = [True] * len(gens)
    t = 0
    while any(live):
        for i, gen in enumerate(gens):
            if live[i] and t >= i * stagger:
                try:
                    next(gen)
                except StopIteration:
                    live[i] = False
        t += 1
        yield


def _weight_chunks(wi_hbm, wo_hbm, wi_ref, wo_ref):
    chunks = []
    for r0 in range(0, D_MODEL, WEIGHT_SLOT_ROWS):
        for c0 in range(0, 2 * D_FF, D_MODEL):
            w = min(D_MODEL, 2 * D_FF - c0)
            view = (pl.ds(r0, WEIGHT_SLOT_ROWS), pl.ds(c0, w))
            chunks.append((wi_hbm.at[view], wi_ref.at[view], w))
    for r0 in range(0, D_FF, WEIGHT_SLOT_ROWS):
        view = (pl.ds(r0, WEIGHT_SLOT_ROWS), pl.ds(0, D_MODEL))
        chunks.append((wo_hbm.at[view], wo_ref.at[view], D_MODEL))
    return chunks


def _load_weights_bf16(wi_hbm, wo_hbm, wi_ref, wo_ref, stage_ref, sem):
    chunks = _weight_chunks(wi_hbm, wo_hbm, wi_ref, wo_ref)
    n_slots = stage_ref.shape[0] // WEIGHT_SLOT_ROWS

    def slot(k):
        return stage_ref.at[pl.ds((k % n_slots) * WEIGHT_SLOT_ROWS, WEIGHT_SLOT_ROWS),
                            pl.ds(0, chunks[k][2])]

    def copy(k):
        return pltpu.make_async_copy(chunks[k][0], slot(k), sem.at[k % n_slots])

    for k in range(n_slots):
        copy(k).start()
    for k in range(len(chunks)):
        copy(k).wait()
        chunks[k][1][...] = slot(k)[...].astype(BF)
        if k + n_slots < len(chunks):
            copy(k + n_slots).start()


def _ffn_rows(load_x, g_ref, wi_ref, wo_ref, acc_ref, rows, finish):
    h = _rmsnorm(load_x(rows), g_ref[...]).astype(BF)
    yield
    for j in range(N_FF_CHUNKS):
        cs = slice(j * FF_CHUNK, (j + 1) * FF_CHUNK)
        gate = _dot(h, wi_ref[:, cs])
        up = _dot(h, wi_ref[:, D_FF + j * FF_CHUNK:D_FF + (j + 1) * FF_CHUNK])
        act = (_silu(gate) * up).astype(BF)
        part = _dot(act, wo_ref[cs, :])
        if j == 0:
            acc_ref[rows, :] = part
        else:
            acc_ref[rows, :] += part
        yield
    finish(load_x(rows), acc_ref[rows, :])


def _ffn_kernel(xa_ref, xb_ref, g_ref, wi_hbm, wo_hbm, gf_ref, oa_ref, ob_ref,
                wi_ref, wo_ref, acc_ref, sem, *, final_norm):
    first = pl.program_id(0) == 0

    @pl.when(first)
    def _():
        _load_weights_bf16(wi_hbm, wo_hbm, wi_ref, wo_ref, acc_ref, sem)

    def load_x(rows):
        return jnp.where(first, xb_ref[rows, :], xa_ref[rows, :])

    def half(i):
        rows = slice(i * FFN_HALF, (i + 1) * FFN_HALF)

        def finish(x, acc):
            y = x + 0.5 * acc
            oa_ref[rows, :] = _rmsnorm(y, gf_ref[...]) if final_norm else y

        return _ffn_rows(load_x, g_ref, wi_ref, wo_ref, acc_ref, rows, finish)

    for _ in _staggered([half(i) for i in range(FFN_TILE // FFN_HALF)], 1):
        pass

    @pl.when(first)
    def _():
        ob_ref[...] = oa_ref[...]


def _resident(shape):
    nd = len(shape)
    return pl.BlockSpec(shape, lambda *_: (0,) * nd, pipeline_mode=pl.Buffered(1))


def _ffn_call(xa, xb, g, wi, wo, gf, final_norm, name):
    ta = xa.shape[0]
    assert ta % FFN_TILE == 0 and xb.shape[0] == FFN_TILE
    tile_a = lambda i: (jnp.maximum(i - 1, 0), 0)
    hbm = pl.BlockSpec(memory_space=pl.ANY)
    return pl.pallas_call(
        functools.partial(_ffn_kernel, final_norm=final_norm),
        grid=(1 + ta // FFN_TILE,),
        in_specs=[
            pl.BlockSpec((FFN_TILE, D_MODEL), tile_a),
            _resident((FFN_TILE, D_MODEL)),
            _resident((1, D_MODEL)),
            hbm,
            hbm,
            _resident((1, D_MODEL)),
        ],
        out_specs=[
            pl.BlockSpec((FFN_TILE, D_MODEL), tile_a),
            pl.BlockSpec((FFN_TILE, D_MODEL), lambda i: (0, 0)),
        ],
        out_shape=[
            jax.ShapeDtypeStruct((ta, D_MODEL), F32),
            jax.ShapeDtypeStruct((FFN_TILE, D_MODEL), F32),
        ],
        scratch_shapes=[
            pltpu.VMEM((D_MODEL, 2 * D_FF), BF),
            pltpu.VMEM((D_FF, D_MODEL), BF),
            pltpu.VMEM((FFN_TILE, D_MODEL), F32),
            pltpu.SemaphoreType.DMA((FFN_TILE // WEIGHT_SLOT_ROWS,)),
        ],
        compiler_params=pltpu.CompilerParams(
            dimension_semantics=("arbitrary",), vmem_limit_bytes=VMEM_LIMIT),
        name=name,
    )(xa, xb, g, wi, wo, gf)


def _cum_matrices(chunk):
    r = np.arange(ROWS)[:, None]
    k = np.arange(ROWS)[None, :]
    same = (r // chunk) == (k // chunk)
    m_cum = same & (k <= r)
    m_ref = same & ((k % chunk) <= (chunk // 2))
    m_last = same
    m = np.concatenate([m_cum, m_ref, m_last], axis=0).astype(np.float32)
    return np.concatenate([m, m], axis=1)


def _mix_block(cols, chunk, consts, get_state, put_state, emit):
    (wmix_ref, bias_ref, vnorm_ref, anorm_ref, wa2_ref, ba_ref, onorm_ref, cum_ref) = consts
    n_chunks = ROWS // chunk
    shift = int(np.log2(chunk))

    z = _dot(cols(OFF_LR, LANES).astype(BF), wa2_ref[...]) + ba_ref[...]
    la = _log_sigmoid(z) * (1.0 / B_GATE_NORMALIZER)
    la_hi = la.astype(BF)
    la_lo = (la - la_hi.astype(F32)).astype(BF)
    sums = _dot(cum_ref[...], jnp.concatenate([la_hi, la_lo], axis=0))
    b = sums[0:ROWS]
    b_ref = sums[ROWS:2 * ROWS]
    b_last = sums[2 * ROWS:3 * ROWS]

    ya = []
    va = []
    ssq = None
    for g in range(A_GROUPS):
        gs = slice(g * LANES, (g + 1) * LANES)
        u = _gelu_tanh(cols(OFF_U + g * LANES, LANES))
        v = _rmsnorm(_gelu_tanh(cols(OFF_V + g * LANES, LANES)), vnorm_ref[:, gs])
        zg = _dot(wmix_ref[g], v.astype(BF)) + bias_ref[g]
        yg = u * zg
        s = jnp.sum(yg * yg, axis=-1, keepdims=True)
        ssq = s if ssq is None else ssq + s
        ya.append(yg)
        va.append(v)
        yield
    inv = lax.rsqrt(ssq * (1.0 / A_WIDTH) + EPS)
    ya = [(ya[g] * inv) * anorm_ref[:, g * LANES:(g + 1) * LANES] for g in range(A_GROUPS)]
    gate = [_silu(cols(OFF_R + h * B_DV, B_DV)) for h in range(B_HEADS)]
    yield

    ri = lax.broadcasted_iota(jnp.int32, (ROWS, ROWS), 0)
    ci = lax.broadcasted_iota(jnp.int32, (ROWS, ROWS), 1)
    rchunk = lax.shift_right_logical(ri, shift)
    cchunk = lax.shift_right_logical(ci, shift)
    causal = (rchunk == cchunk) & (ci <= ri)
    lane = lax.broadcasted_iota(jnp.int32, (ROWS, LANES), 1)
    head0 = lane < B_DK

    yb = []
    for pair in range(B_HEADS // 2):
        ls = slice(pair * LANES, (pair + 1) * LANES)
        q = cols(OFF_Q + pair * LANES, LANES) * np.float32(B_DK ** -0.5)
        k = cols(OFF_K + pair * LANES, LANES)
        vv_bf = cols(OFF_VB + pair * 2 * B_DV, 2 * B_DV).astype(BF)
        bp, brp, blp = b[:, ls], b_ref[:, ls], b_last[:, ls]
        qs = q * jnp.exp(bp - brp)
        ks = (k * jnp.exp(brp - bp)).astype(BF)
        kl_t = (k * jnp.exp(blp - bp)).T
        bl_t = blp.T
        qb = (q * jnp.exp(bp)).astype(BF)
        yield

        o_heads = []
        for hh in range(2):
            msk = head0 if hh == 0 else jnp.logical_not(head0)
            qh = jnp.where(msk, qs, 0.0).astype(BF)
            sc = jnp.where(causal, _dot_nt(qh, ks), 0.0).astype(BF)
            o_heads.append(_dot(sc, vv_bf[:, hh * B_DV:(hh + 1) * B_DV]))
        o = jnp.concatenate(o_heads, axis=1)

        o_inter = []
        for j in range(n_chunks):
            s_prev = get_state(pair, j)
            o_inter.append(_dot(qb[j * chunk:(j + 1) * chunk], _blockdiag_bf16(*s_prev)))
            in_chunk = cchunk == j
            upd = _dot(jnp.where(in_chunk, kl_t, 0.0).astype(BF), vv_bf)
            dec = jnp.exp(bl_t[:, j * chunk:j * chunk + 1])
            put_state(pair, j, tuple(
                dec[hh * B_DK:(hh + 1) * B_DK] * s_prev[hh]
                + upd[hh * B_DK:(hh + 1) * B_DK, hh * B_DV:(hh + 1) * B_DV] for hh in range(2)))
        o = o + jnp.concatenate(o_inter, axis=0)

        for hh in range(2):
            h_idx = 2 * pair + hh
            hs = slice(h_idx * B_DV, (h_idx + 1) * B_DV)
            oh = _rmsnorm(o[:, hh * B_DV:(hh + 1) * B_DV], onorm_ref[:, hs])
            yb.append(oh * gate[h_idx])
        yield

    emit(jnp.concatenate(ya + yb, axis=1).astype(BF), jnp.concatenate(va, axis=1))


def _blockdiag_bf16(s0, s1):
    s0 = s0.astype(BF)
    s1 = s1.astype(BF)
    zero = jnp.zeros_like(s0)
    return jnp.concatenate(
        [jnp.concatenate([s0, zero], axis=1), jnp.concatenate([zero, s1], axis=1)], axis=0)


def _in_proj(x1_ref, g2_ref, wp_ref, p_scr, g, rows):
    h = _rmsnorm(x1_ref[rows, :], g2_ref[...]).astype(BF)
    for c0 in range(0, P_COLS, MXU_DIM):
        cs = slice(c0, min(c0 + MXU_DIM, P_COLS))
        p_scr[g, :, cs] = _dot(h, wp_ref[:, cs])


def _out_proj(x1_ref, y_scr, wout_ref, x2_ref, g, rows):
    y = y_scr[g]
    for c0 in range(0, D_MODEL, MXU_DIM):
        cs = slice(c0, c0 + MXU_DIM)
        x2_ref[rows, cs] = x1_ref[rows, cs] + _dot(y, wout_ref[:, cs])


def _mixproj_tile(x1_ref, g2_ref, wp_ref, wout_ref, x2_ref, p_scr, y_scr, group, n_groups,
                  make_block):
    rows = [slice(g * group, (g + 1) * group) for g in range(n_groups)]
    for g in range(n_groups):
        _in_proj(x1_ref, g2_ref, wp_ref, p_scr, g, rows[g])
    blocks = [make_block(g, i) for g in range(n_groups) for i in range(group // ROWS)]
    for _ in _staggered(blocks, MIX_STAGGER):
        pass
    for g in range(n_groups):
        _out_proj(x1_ref, y_scr, wout_ref, x2_ref, g, rows[g])


def _block_cols(p_scr, g, i):
    return lambda off, width: p_scr[g, i * ROWS:(i + 1) * ROWS, off:off + width]


def _mixproj_prompt_kernel(x1_ref, g2_ref, wp_ref, wout_ref, wmix_ref, bias_ref, vnorm_ref,
                           anorm_ref, wa2_ref, ba_ref, onorm_ref, cum_ref, x2_ref, st_ref,
                           p_scr, y_scr, s_scr, *, group, n_groups):
    @pl.when(pl.program_id(1) == 0)
    def _():
        s_scr[...] = jnp.zeros_like(s_scr)

    consts = (wmix_ref, bias_ref, vnorm_ref, anorm_ref, wa2_ref, ba_ref, onorm_ref, cum_ref)
    cur = {pair: (s_scr[2 * pair], s_scr[2 * pair + 1]) for pair in range(B_HEADS // 2)}

    def get_state(pair, j):
        return cur[pair]

    def put_state(pair, j, s):
        cur[pair] = s

    def make_block(g, i):
        def emit(y, va):
            y_scr[g, i * ROWS:(i + 1) * ROWS, :] = y
        return _mix_block(_block_cols(p_scr, g, i), B_CHUNK, consts, get_state, put_state, emit)

    _mixproj_tile(x1_ref, g2_ref, wp_ref, wout_ref, x2_ref, p_scr, y_scr, group, n_groups,
                  make_block)

    for pair in range(B_HEADS // 2):
        for hh in range(2):
            s_scr[2 * pair + hh] = cur[pair][hh]
            st_ref[0, 2 * pair + hh] = cur[pair][hh]


def _mixproj_sample_kernel(x1_ref, s0_ref, g2_ref, wp_ref, wout_ref, wmix_ref, bias_ref, vnorm_ref,
                           anorm_ref, wa2_ref, ba_ref, onorm_ref, cum_ref, x2_ref, va_ref, st_ref,
                           p_scr, y_scr, *, chunk, group, n_groups):
    consts = (wmix_ref, bias_ref, vnorm_ref, anorm_ref, wa2_ref, ba_ref, onorm_ref, cum_ref)
    n_blk = group // ROWS
    per_blk = ROWS // chunk

    def make_block(g, i):
        blk = g * n_blk + i

        def get_state(pair, j):
            n = blk * per_blk + j
            return s0_ref[n, 2 * pair], s0_ref[n, 2 * pair + 1]

        def put_state(pair, j, s):
            n = blk * per_blk + j
            st_ref[n, 2 * pair] = s[0]
            st_ref[n, 2 * pair + 1] = s[1]

        def emit(y, va):
            y_scr[g, i * ROWS:(i + 1) * ROWS, :] = y
            va_ref[blk * ROWS:(blk + 1) * ROWS, :] = va

        return _mix_block(_block_cols(p_scr, g, i), chunk, consts, get_state, put_state, emit)

    _mixproj_tile(x1_ref, g2_ref, wp_ref, wout_ref, x2_ref, p_scr, y_scr, group, n_groups,
                  make_block)


def _mix_const_specs():
    return [
        _resident((A_GROUPS, ROWS, ROWS)),
        _resident((A_GROUPS, ROWS, LANES)),
        _resident((1, A_WIDTH)),
        _resident((1, A_WIDTH)),
        _resident((LANES, B_HEADS * B_DK)),
        _resident((1, B_HEADS * B_DK)),
        _resident((1, B_HEADS * B_DV)),
        _resident((3 * ROWS, 2 * ROWS)),
    ]


def _proj_specs():
    return [
        _resident((1, D_MODEL)),
        _resident((D_MODEL, P_COLS)),
        _resident((D_MODEL, D_MODEL)),
    ]


def _mixproj_scratch(group, n_groups):
    return [pltpu.VMEM((n_groups, group, P_COLS), F32),
            pltpu.VMEM((n_groups, group, D_MODEL), BF)]


def _mixproj_prompt_call(x1, g2, wp, wout, consts, batch, seq, tile, group):
    n_tiles = seq // tile
    n_groups = tile // group
    row = lambda b, s: (b * n_tiles + s, 0)
    return pl.pallas_call(
        functools.partial(_mixproj_prompt_kernel, group=group, n_groups=n_groups),
        grid=(batch, n_tiles),
        in_specs=[pl.BlockSpec((tile, D_MODEL), row)] + _proj_specs() + _mix_const_specs(),
        out_specs=[
            pl.BlockSpec((tile, D_MODEL), row),
            pl.BlockSpec((1, B_HEADS, B_DK, B_DV), lambda b, s: (b, 0, 0, 0)),
        ],
        out_shape=[
            jax.ShapeDtypeStruct((batch * seq, D_MODEL), F32),
            jax.ShapeDtypeStruct((batch, B_HEADS, B_DK, B_DV), F32),
        ],
        scratch_shapes=_mixproj_scratch(group, n_groups)
        + [pltpu.VMEM((B_HEADS, B_DK, B_DV), F32)],
        compiler_params=pltpu.CompilerParams(
            dimension_semantics=("arbitrary", "arbitrary"), vmem_limit_bytes=VMEM_LIMIT),
        name="mixproj_prompt",
    )(x1, g2, wp, wout, *consts)


def _mixproj_sample_call(x1, s0, g2, wp, wout, consts, n_seq, seq, tile, group):
    seq_per_step = tile // seq
    n_groups = tile // group
    row = lambda i: (i, 0)
    state = lambda i: (i, 0, 0, 0)
    return pl.pallas_call(
        functools.partial(_mixproj_sample_kernel, chunk=seq, group=group, n_groups=n_groups),
        grid=(n_seq // seq_per_step,),
        in_specs=[
            pl.BlockSpec((tile, D_MODEL), row),
            pl.BlockSpec((seq_per_step, B_HEADS, B_DK, B_DV), state),
        ] + _proj_specs() + _mix_const_specs(),
        out_specs=[
            pl.BlockSpec((tile, D_MODEL), row),
            pl.BlockSpec((tile, A_WIDTH), row),
            pl.BlockSpec((seq_per_step, B_HEADS, B_DK, B_DV), state),
        ],
        out_shape=[
            jax.ShapeDtypeStruct((n_seq * seq, D_MODEL), F32),
            jax.ShapeDtypeStruct((n_seq * seq, A_WIDTH), F32),
            jax.ShapeDtypeStruct((n_seq, B_HEADS, B_DK, B_DV), F32),
        ],
        scratch_shapes=_mixproj_scratch(group, n_groups),
        compiler_params=pltpu.CompilerParams(
            dimension_semantics=("arbitrary",), vmem_limit_bytes=VMEM_LIMIT),
        name="mixproj_sample",
    )(x1, s0, g2, wp, wout, *consts)


def _mix_consts(a_ws, a_bs, a_vnorm, a_onorm, b_wa2, b_ba, b_onorm, chunk, seq_len):
    n_rep = ROWS // seq_len
    tril = jnp.tril(jnp.ones((seq_len, seq_len), dtype=bool))
    w = jnp.where(tril[None], a_ws[:, :seq_len, :seq_len], 0.0)
    if n_rep > 1:
        w = jax.vmap(lambda wg: jnp.kron(jnp.eye(n_rep, dtype=wg.dtype), wg))(w)
    bias = jnp.tile(a_bs[:, :seq_len], (1, n_rep))
    bias = jnp.broadcast_to(bias[:, :, None], (A_GROUPS, ROWS, LANES))
    wa2 = jnp.zeros((LANES, B_HEADS * B_DK), F32).at[:B_LOWRANK].set(b_wa2)
    return (
        w.astype(BF),
        bias,
        a_vnorm.reshape(1, A_WIDTH),
        a_onorm.reshape(1, A_WIDTH),
        wa2.astype(BF),
        b_ba.reshape(1, B_HEADS * B_DK),
        b_onorm.reshape(1, B_HEADS * B_DV),
        jnp.asarray(_cum_matrices(chunk), dtype=BF),
    )


def kernel(x_prompt, x_sample, state_gla, ffn1_norm, ffn1_w_in, ffn1_w_out, mix_norm, w_in,
           a_ws, a_bs, a_vnorm, a_onorm, b_wa2, b_ba, b_onorm, w_out, ffn2_norm, ffn2_w_in,
           ffn2_w_out, final_norm):
    batch, seq, _ = x_prompt.shape
    n_dec, dec_seq, _ = x_sample.shape
    assert ffn1_norm.shape[0] == 1, "single-layer kernel"
    assert seq % A_CHUNK == 0 and ROWS % dec_seq == 0 and dec_seq <= B_CHUNK

    g1 = ffn1_norm[0].reshape(1, D_MODEL)
    wi1 = ffn1_w_in.reshape(D_MODEL, 2 * D_FF)
    wo1 = ffn1_w_out.reshape(D_FF, D_MODEL)
    g2 = mix_norm[0].reshape(1, D_MODEL)
    wp = jnp.pad(w_in[0], ((0, 0), (0, P_COLS - IN_COLS))).astype(BF)
    wout = w_out[0].astype(BF)
    g3 = ffn2_norm[0].reshape(1, D_MODEL)
    wi2 = ffn2_w_in.reshape(D_MODEL, 2 * D_FF)
    wo2 = ffn2_w_out.reshape(D_FF, D_MODEL)
    gf = final_norm.reshape(1, D_MODEL)
    mix_args = (a_ws[0], a_bs[0], a_vnorm[0], a_onorm[0], b_wa2[0], b_ba[0], b_onorm[0])
    consts_p = _mix_consts(*mix_args, chunk=B_CHUNK, seq_len=A_CHUNK)
    consts_s = _mix_consts(*mix_args, chunk=dec_seq, seq_len=dec_seq)

    xp = x_prompt.reshape(batch * seq, D_MODEL)
    xs = x_sample.reshape(n_dec * dec_seq, D_MODEL)

    x1p, x1s = _ffn_call(xp, xs, g1, wi1, wo1, gf, False, "ffn1")

    x2p, st_p = _mixproj_prompt_call(x1p, g2, wp, wout, consts_p, batch, seq, tile=1024, group=512)
    s0 = state_gla.reshape(state_gla.shape[1:])
    x2s, va_s, st_s = _mixproj_sample_call(x1s, s0, g2, wp, wout, consts_s, n_dec, dec_seq,
                                           tile=256, group=128)

    out_p, out_s = _ffn_call(x2p, x2s, g3, wi2, wo2, gf, True, "ffn2")

    return (
        out_p.reshape(batch, seq, D_MODEL),
        out_s.reshape(n_dec, dec_seq, D_MODEL),
        st_p[None],
        st_s[None],
        va_s.reshape(1, n_dec, dec_seq, A_WIDTH),
    )
```

```python
import functools

import numpy as np
import jax
import jax.numpy as jnp
from jax import lax
from jax.experimental import pallas as pl
from jax.experimental.pallas import tpu as pltpu

D_MODEL = 1024
D_FF = 2816
A_WIDTH = 512
A_GROUPS = 4
A_CHUNK = 128
B_HEADS = 4
B_DK = 64
B_DV = 128
B_LOWRANK = 16
B_GATE_NORMALIZER = 16.0
B_CHUNK = 64
EPS = 1e-6

OFF_U, OFF_V, OFF_Q, OFF_K, OFF_VB, OFF_R, OFF_LR = 0, 512, 1024, 1280, 1536, 2048, 2560
IN_COLS = 2576
P_COLS = 2688

LANES = 128
MXU_DIM = 256
ROWS = 128
FF_CHUNK = MXU_DIM
N_FF_CHUNKS = D_FF // FF_CHUNK
VMEM_LIMIT = 56 * 1024 * 1024

FFN_TILE = 1024
FFN_HALF = FFN_TILE // 2
WEIGHT_SLOT_ROWS = 256
P_MAIN = (IN_COLS // LANES) * LANES
CAST_BLOCKS = 16
MIX_STAGGER = 6

BF = jnp.bfloat16
F32 = jnp.float32


def _dot(a, b):
    return jnp.dot(a, b, preferred_element_type=F32)


def _dot_nt(a, b):
    return lax.dot_general(a, b, (((1,), (1,)), ((), ())), preferred_element_type=F32)


def _rmsnorm(x, g):
    return (x * lax.rsqrt(jnp.mean(x * x, axis=-1, keepdims=True) + EPS)) * g


def _gelu_tanh(x):
    c = np.float32(np.sqrt(2.0 / np.pi))
    ca = np.float32(np.sqrt(2.0 / np.pi) * 0.044715)
    t = jnp.tanh(x * (c + ca * (x * x)))
    hx = 0.5 * x
    return hx + hx * t


def _silu(x):
    return x * (1.0 / (1.0 + jnp.exp(-x)))


def _log_sigmoid(x):
    return jnp.minimum(x, 0.0) - jnp.log(1.0 + jnp.exp(-jnp.abs(x)))


def _staggered(gens, stagger):
    live = [True] * len(gens)
    t = 0
    while any(live):
        for i, gen in enumerate(gens):
            if live[i] and t >= i * stagger:
                try:
                    next(gen)
                except StopIteration:
                    live[i] = False
        t += 1
        yield


def _cast_job(w, col_block=0, cols=None):
    rows, width = w.shape
    cols = width if cols is None else cols
    return dict(w=w, block=(rows // CAST_BLOCKS, cols), col_block=col_block,
                valid=min(cols, width - col_block * cols), out=jax.ShapeDtypeStruct((rows, cols), BF))


def _cast_specs(jobs, row_block):
    in_specs = [pl.BlockSpec(j["block"], lambda *ids, j=j: (row_block(*ids), j["col_block"]))
                for j in jobs]
    out_specs = [pl.BlockSpec(j["block"], lambda *ids: (row_block(*ids), 0)) for j in jobs]
    return in_specs, out_specs, [j["out"] for j in jobs]


def _cast_blocks(src_refs, dst_refs, valid):
    for src, dst, v in zip(src_refs, dst_refs, valid):
        x = src[...]
        if v < x.shape[1]:
            lane = lax.broadcasted_iota(jnp.int32, x.shape, 1)
            x = jnp.where(lane < v, x, 0.0)
        dst[...] = x.astype(BF)


def _weight_chunks(wi_hbm, wo_hbm, wi_ref, wo_ref):
    chunks = []
    for r0 in range(0, D_MODEL, WEIGHT_SLOT_ROWS):
        for c0 in range(0, 2 * D_FF, D_MODEL):
            w = min(D_MODEL, 2 * D_FF - c0)
            view = (pl.ds(r0, WEIGHT_SLOT_ROWS), pl.ds(c0, w))
            chunks.append((wi_hbm.at[view], wi_ref.at[view], w))
    for r0 in range(0, D_FF, WEIGHT_SLOT_ROWS):
        view = (pl.ds(r0, WEIGHT_SLOT_ROWS), pl.ds(0, D_MODEL))
        chunks.append((wo_hbm.at[view], wo_ref.at[view], D_MODEL))
    return chunks


def _load_weights_bf16(wi_hbm, wo_hbm, wi_ref, wo_ref, stage_ref, sem):
    chunks = _weight_chunks(wi_hbm, wo_hbm, wi_ref, wo_ref)
    n_slots = stage_ref.shape[0] // WEIGHT_SLOT_ROWS

    def slot(k):
        return stage_ref.at[pl.ds((k % n_slots) * WEIGHT_SLOT_ROWS, WEIGHT_SLOT_ROWS),
                            pl.ds(0, chunks[k][2])]

    def copy(k):
        return pltpu.make_async_copy(chunks[k][0], slot(k), sem.at[k % n_slots])

    for k in range(n_slots):
        copy(k).start()
    for k in range(len(chunks)):
        copy(k).wait()
        chunks[k][1][...] = slot(k)[...].astype(BF)
        if k + n_slots < len(chunks):
            copy(k + n_slots).start()


def _ffn_rows(load_x, g_ref, wi_ref, wo_ref, acc_ref, rows, finish):
    h = _rmsnorm(load_x(rows), g_ref[...]).astype(BF)
    yield
    for j in range(N_FF_CHUNKS):
        cs = slice(j * FF_CHUNK, (j + 1) * FF_CHUNK)
        gate = _dot(h, wi_ref[:, cs])
        up = _dot(h, wi_ref[:, D_FF + j * FF_CHUNK:D_FF + (j + 1) * FF_CHUNK])
        act = (_silu(gate) * up).astype(BF)
        part = _dot(act, wo_ref[cs, :])
        if j == 0:
            acc_ref[rows, :] = part
        else:
            acc_ref[rows, :] += part
        yield
    finish(load_x(rows), acc_ref[rows, :])


def _ffn_kernel(*refs, final_norm, stream_weights, cast_valid):
    n_cast = len(cast_valid)
    xa_ref, xb_ref, g_ref, wi_in, wo_in, gf_ref = refs[:6]
    cast_src = refs[6:6 + n_cast]
    oa_ref, ob_ref = refs[6 + n_cast:8 + n_cast]
    cast_dst = refs[8 + n_cast:8 + 2 * n_cast]
    scratch = refs[8 + 2 * n_cast:]
    first = pl.program_id(0) == 0

    if stream_weights:
        wi_ref, wo_ref, acc_ref, sem = scratch

        @pl.when(first)
        def _():
            _load_weights_bf16(wi_in, wo_in, wi_ref, wo_ref, acc_ref, sem)
    else:
        (acc_ref,) = scratch
        wi_ref, wo_ref = wi_in, wo_in

    def load_x(rows):
        return jnp.where(first, xb_ref[rows, :], xa_ref[rows, :])

    def half(i):
        rows = slice(i * FFN_HALF, (i + 1) * FFN_HALF)

        def finish(x, acc):
            y = x + 0.5 * acc
            oa_ref[rows, :] = _rmsnorm(y, gf_ref[...]) if final_norm else y

        return _ffn_rows(load_x, g_ref, wi_ref, wo_ref, acc_ref, rows, finish)

    for _ in _staggered([half(i) for i in range(FFN_TILE // FFN_HALF)], 1):
        pass
    _cast_blocks(cast_src, cast_dst, cast_valid)

    @pl.when(first)
    def _():
        ob_ref[...] = oa_ref[...]


def _resident(shape):
    nd = len(shape)
    return pl.BlockSpec(shape, lambda *_: (0,) * nd, pipeline_mode=pl.Buffered(1))


def _ffn_call(xa, xb, g, wi, wo, gf, final_norm, name, cast_jobs=()):
    ta = xa.shape[0]
    n_a = ta // FFN_TILE
    assert ta % FFN_TILE == 0 and xb.shape[0] == FFN_TILE
    assert not cast_jobs or n_a == CAST_BLOCKS
    stream_weights = wi.dtype == F32
    tile_a = lambda i: (jnp.maximum(i - 1, 0), 0)
    cast_in, cast_out, cast_shapes = _cast_specs(cast_jobs, lambda i: jnp.maximum(i - 1, 0))
    if stream_weights:
        w_specs = [pl.BlockSpec(memory_space=pl.ANY)] * 2
        w_scratch = [pltpu.VMEM(wi.shape, BF), pltpu.VMEM(wo.shape, BF)]
        sems = [pltpu.SemaphoreType.DMA((FFN_TILE // WEIGHT_SLOT_ROWS,))]
    else:
        w_specs = [_resident(wi.shape), _resident(wo.shape)]
        w_scratch, sems = [], []
    return pl.pallas_call(
        functools.partial(_ffn_kernel, final_norm=final_norm, stream_weights=stream_weights,
                          cast_valid=tuple(j["valid"] for j in cast_jobs)),
        grid=(1 + n_a,),
        in_specs=[
            pl.BlockSpec((FFN_TILE, D_MODEL), tile_a),
            _resident((FFN_TILE, D_MODEL)),
            _resident((1, D_MODEL)),
            *w_specs,
            _resident((1, D_MODEL)),
            *cast_in,
        ],
        out_specs=[
            pl.BlockSpec((FFN_TILE, D_MODEL), tile_a),
            pl.BlockSpec((FFN_TILE, D_MODEL), lambda i: (0, 0)),
            *cast_out,
        ],
        out_shape=[
            jax.ShapeDtypeStruct((ta, D_MODEL), F32),
            jax.ShapeDtypeStruct((FFN_TILE, D_MODEL), F32),
            *cast_shapes,
        ],
        scratch_shapes=w_scratch + [pltpu.VMEM((FFN_TILE, D_MODEL), F32)] + sems,
        compiler_params=pltpu.CompilerParams(
            dimension_semantics=("arbitrary",), vmem_limit_bytes=VMEM_LIMIT),
        name=name,
    )(xa, xb, g, wi, wo, gf, *[j["w"] for j in cast_jobs])


def _cum_matrices(chunk):
    r = np.arange(ROWS)[:, None]
    k = np.arange(ROWS)[None, :]
    same = (r // chunk) == (k // chunk)
    m_cum = same & (k <= r)
    m_ref = same & ((k % chunk) <= (chunk // 2))
    m_last = same
    m = np.concatenate([m_cum, m_ref, m_last], axis=0).astype(np.float32)
    return np.concatenate([m, m], axis=1)


def _mix_block(cols, chunk, consts, get_state, put_state, emit):
    (wmix_ref, bias_ref, vnorm_ref, anorm_ref, wa2_ref, ba_ref, onorm_ref, cum_ref) = consts
    n_chunks = ROWS // chunk
    shift = int(np.log2(chunk))

    z = _dot(cols(OFF_LR, LANES).astype(BF), wa2_ref[...]) + ba_ref[...]
    la = _log_sigmoid(z) * (1.0 / B_GATE_NORMALIZER)
    la_hi = la.astype(BF)
    la_lo = (la - la_hi.astype(F32)).astype(BF)
    sums = _dot(cum_ref[...], jnp.concatenate([la_hi, la_lo], axis=0))
    b = sums[0:ROWS]
    b_ref = sums[ROWS:2 * ROWS]
    b_last = sums[2 * ROWS:3 * ROWS]

    ya = []
    va = []
    ssq = None
    for g in range(A_GROUPS):
        gs = slice(g * LANES, (g + 1) * LANES)
        u = _gelu_tanh(cols(OFF_U + g * LANES, LANES))
        v = _rmsnorm(_gelu_tanh(cols(OFF_V + g * LANES, LANES)), vnorm_ref[:, gs])
        zg = _dot(wmix_ref[g], v.astype(BF)) + bias_ref[g]
        yg = u * zg
        s = jnp.sum(yg * yg, axis=-1, keepdims=True)
        ssq = s if ssq is None else ssq + s
        ya.append(yg)
        va.append(v)
        yield
    inv = lax.rsqrt(ssq * (1.0 / A_WIDTH) + EPS)
    ya = [(ya[g] * inv) * anorm_ref[:, g * LANES:(g + 1) * LANES] for g in range(A_GROUPS)]
    gate = [_silu(cols(OFF_R + h * B_DV, B_DV)) for h in range(B_HEADS)]
    yield

    ri = lax.broadcasted_iota(jnp.int32, (ROWS, ROWS), 0)
    ci = lax.broadcasted_iota(jnp.int32, (ROWS, ROWS), 1)
    rchunk = lax.shift_right_logical(ri, shift)
    cchunk = lax.shift_right_logical(ci, shift)
    causal = (rchunk == cchunk) & (ci <= ri)
    lane = lax.broadcasted_iota(jnp.int32, (ROWS, LANES), 1)
    head0 = lane < B_DK

    yb = []
    for pair in range(B_HEADS // 2):
        ls = slice(pair * LANES, (pair + 1) * LANES)
        q = cols(OFF_Q + pair * LANES, LANES) * np.float32(B_DK ** -0.5)
        k = cols(OFF_K + pair * LANES, LANES)
        vv_bf = cols(OFF_VB + pair * 2 * B_DV, 2 * B_DV).astype(BF)
        bp, brp, blp = b[:, ls], b_ref[:, ls], b_last[:, ls]
        qs = q * jnp.exp(bp - brp)
        ks = (k * jnp.exp(brp - bp)).astype(BF)
        kl_t = (k * jnp.exp(blp - bp)).T
        bl_t = blp.T
        qb = (q * jnp.exp(bp)).astype(BF)
        yield

        o_heads = []
        for hh in range(2):
            msk = head0 if hh == 0 else jnp.logical_not(head0)
            qh = jnp.where(msk, qs, 0.0).astype(BF)
            sc = jnp.where(causal, _dot_nt(qh, ks), 0.0).astype(BF)
            o_heads.append(_dot(sc, vv_bf[:, hh * B_DV:(hh + 1) * B_DV]))
        o = jnp.concatenate(o_heads, axis=1)

        o_inter = []
        for j in range(n_chunks):
            s_prev = get_state(pair, j)
            o_inter.append(_dot(qb[j * chunk:(j + 1) * chunk], _blockdiag_bf16(*s_prev)))
            in_chunk = cchunk == j
            upd = _dot(jnp.where(in_chunk, kl_t, 0.0).astype(BF), vv_bf)
            dec = jnp.exp(bl_t[:, j * chunk:j * chunk + 1])
            put_state(pair, j, tuple(
                dec[hh * B_DK:(hh + 1) * B_DK] * s_prev[hh]
                + upd[hh * B_DK:(hh + 1) * B_DK, hh * B_DV:(hh + 1) * B_DV] for hh in range(2)))
        o = o + jnp.concatenate(o_inter, axis=0)

        for hh in range(2):
            h_idx = 2 * pair + hh
            hs = slice(h_idx * B_DV, (h_idx + 1) * B_DV)
            oh = _rmsnorm(o[:, hh * B_DV:(hh + 1) * B_DV], onorm_ref[:, hs])
            yb.append(oh * gate[h_idx])
        yield

    emit(jnp.concatenate(ya + yb, axis=1).astype(BF), jnp.concatenate(va, axis=1))


def _blockdiag_bf16(s0, s1):
    s0 = s0.astype(BF)
    s1 = s1.astype(BF)
    zero = jnp.zeros_like(s0)
    return jnp.concatenate(
        [jnp.concatenate([s0, zero], axis=1), jnp.concatenate([zero, s1], axis=1)], axis=0)


def _in_proj(x1_ref, g2_ref, wp_refs, p_scr, g, rows):
    wp_ref, wpt_ref = wp_refs
    h = _rmsnorm(x1_ref[rows, :], g2_ref[...]).astype(BF)
    for c0 in range(0, P_MAIN, MXU_DIM):
        cs = slice(c0, c0 + MXU_DIM)
        p_scr[g, :, cs] = _dot(h, wp_ref[:, cs])
    p_scr[g, :, P_MAIN:P_COLS] = _dot(h, wpt_ref[...])


def _out_proj(x1_ref, y_scr, wout_ref, x2_ref, g, rows):
    y = y_scr[g]
    for c0 in range(0, D_MODEL, MXU_DIM):
        cs = slice(c0, c0 + MXU_DIM)
        x2_ref[rows, cs] = x1_ref[rows, cs] + _dot(y, wout_ref[:, cs])


def _mixproj_tile(x1_ref, g2_ref, wp_refs, wout_ref, x2_ref, p_scr, y_scr, group, n_groups,
                  make_block):
    rows = [slice(g * group, (g + 1) * group) for g in range(n_groups)]
    for g in range(n_groups):
        _in_proj(x1_ref, g2_ref, wp_refs, p_scr, g, rows[g])
    blocks = [make_block(g, i) for g in range(n_groups) for i in range(group // ROWS)]
    for _ in _staggered(blocks, MIX_STAGGER):
        pass
    for g in range(n_groups):
        _out_proj(x1_ref, y_scr, wout_ref, x2_ref, g, rows[g])


def _block_cols(p_scr, g, i):
    return lambda off, width: p_scr[g, i * ROWS:(i + 1) * ROWS, off:off + width]


def _mixproj_prompt_kernel(*refs, group, n_groups, cast_valid):
    n_cast = len(cast_valid)
    (x1_ref, g2_ref, wp_ref, wpt_ref, wout_ref, wmix_ref, bias_ref, vnorm_ref, anorm_ref, wa2_ref,
     ba_ref, onorm_ref, cum_ref) = refs[:13]
    cast_src = refs[13:13 + n_cast]
    x2_ref, st_ref = refs[13 + n_cast:15 + n_cast]
    cast_dst = refs[15 + n_cast:15 + 2 * n_cast]
    p_scr, y_scr, s_scr = refs[15 + 2 * n_cast:]

    @pl.when(pl.program_id(1) == 0)
    def _():
        s_scr[...] = jnp.zeros_like(s_scr)

    consts = (wmix_ref, bias_ref, vnorm_ref, anorm_ref, wa2_ref, ba_ref, onorm_ref, cum_ref)
    cur = {pair: (s_scr[2 * pair], s_scr[2 * pair + 1]) for pair in range(B_HEADS // 2)}

    def get_state(pair, j):
        return cur[pair]

    def put_state(pair, j, s):
        cur[pair] = s

    def make_block(g, i):
        def emit(y, va):
            y_scr[g, i * ROWS:(i + 1) * ROWS, :] = y
        return _mix_block(_block_cols(p_scr, g, i), B_CHUNK, consts, get_state, put_state, emit)

    _mixproj_tile(x1_ref, g2_ref, (wp_ref, wpt_ref), wout_ref, x2_ref, p_scr, y_scr, group,
                  n_groups, make_block)
    _cast_blocks(cast_src, cast_dst, cast_valid)

    for pair in range(B_HEADS // 2):
        for hh in range(2):
            s_scr[2 * pair + hh] = cur[pair][hh]
            st_ref[0, 2 * pair + hh] = cur[pair][hh]


def _mixproj_sample_kernel(x1_ref, s0_ref, g2_ref, wp_ref, wpt_ref, wout_ref, wmix_ref, bias_ref,
                           vnorm_ref, anorm_ref, wa2_ref, ba_ref, onorm_ref, cum_ref, x2_ref,
                           va_ref, st_ref, p_scr, y_scr, *, chunk, group, n_groups):
    consts = (wmix_ref, bias_ref, vnorm_ref, anorm_ref, wa2_ref, ba_ref, onorm_ref, cum_ref)
    n_blk = group // ROWS
    per_blk = ROWS // chunk

    def make_block(g, i):
        blk = g * n_blk + i

        def get_state(pair, j):
            n = blk * per_blk + j
            return s0_ref[n, 2 * pair], s0_ref[n, 2 * pair + 1]

        def put_state(pair, j, s):
            n = blk * per_blk + j
            st_ref[n, 2 * pair] = s[0]
            st_ref[n, 2 * pair + 1] = s[1]

        def emit(y, va):
            y_scr[g, i * ROWS:(i + 1) * ROWS, :] = y
            va_ref[blk * ROWS:(blk + 1) * ROWS, :] = va

        return _mix_block(_block_cols(p_scr, g, i), chunk, consts, get_state, put_state, emit)

    _mixproj_tile(x1_ref, g2_ref, (wp_ref, wpt_ref), wout_ref, x2_ref, p_scr, y_scr, group,
                  n_groups, make_block)


def _mix_const_specs():
    return [
        _resident((A_GROUPS, ROWS, ROWS)),
        _resident((A_GROUPS, ROWS, LANES)),
        _resident((1, A_WIDTH)),
        _resident((1, A_WIDTH)),
        _resident((LANES, B_HEADS * B_DK)),
        _resident((1, B_HEADS * B_DK)),
        _resident((1, B_HEADS * B_DV)),
        _resident((3 * ROWS, 2 * ROWS)),
    ]


def _proj_specs():
    return [
        _resident((1, D_MODEL)),
        _resident((D_MODEL, P_MAIN)),
        _resident((D_MODEL, P_COLS - P_MAIN)),
        _resident((D_MODEL, D_MODEL)),
    ]


def _mixproj_scratch(group, n_groups):
    return [pltpu.VMEM((n_groups, group, P_COLS), F32),
            pltpu.VMEM((n_groups, group, D_MODEL), BF)]


def _mixproj_prompt_call(x1, g2, wp, wpt, wout, consts, batch, seq, tile, group, cast_jobs=()):
    n_tiles = seq // tile
    n_groups = tile // group
    assert not cast_jobs or batch * n_tiles == CAST_BLOCKS
    row = lambda b, s: (b * n_tiles + s, 0)
    cast_in, cast_out, cast_shapes = _cast_specs(cast_jobs, lambda b, s: b * n_tiles + s)
    return pl.pallas_call(
        functools.partial(_mixproj_prompt_kernel, group=group, n_groups=n_groups,
                          cast_valid=tuple(j["valid"] for j in cast_jobs)),
        grid=(batch, n_tiles),
        in_specs=[pl.BlockSpec((tile, D_MODEL), row)] + _proj_specs() + _mix_const_specs()
        + cast_in,
        out_specs=[
            pl.BlockSpec((tile, D_MODEL), row),
            pl.BlockSpec((1, B_HEADS, B_DK, B_DV), lambda b, s: (b, 0, 0, 0)),
            *cast_out,
        ],
        out_shape=[
            jax.ShapeDtypeStruct((batch * seq, D_MODEL), F32),
            jax.ShapeDtypeStruct((batch, B_HEADS, B_DK, B_DV), F32),
            *cast_shapes,
        ],
        scratch_shapes=_mixproj_scratch(group, n_groups)
        + [pltpu.VMEM((B_HEADS, B_DK, B_DV), F32)],
        compiler_params=pltpu.CompilerParams(
            dimension_semantics=("arbitrary", "arbitrary"), vmem_limit_bytes=VMEM_LIMIT),
        name="mixproj_prompt",
    )(x1, g2, wp, wpt, wout, *consts, *[j["w"] for j in cast_jobs])


def _mixproj_sample_call(x1, s0, g2, wp, wpt, wout, consts, n_seq, seq, tile, group):
    seq_per_step = tile // seq
    n_groups = tile // group
    row = lambda i: (i, 0)
    state = lambda i: (i, 0, 0, 0)
    return pl.pallas_call(
        functools.partial(_mixproj_sample_kernel, chunk=seq, group=group, n_groups=n_groups),
        grid=(n_seq // seq_per_step,),
        in_specs=[
            pl.BlockSpec((tile, D_MODEL), row),
            pl.BlockSpec((seq_per_step, B_HEADS, B_DK, B_DV), state),
        ] + _proj_specs() + _mix_const_specs(),
        out_specs=[
            pl.BlockSpec((tile, D_MODEL), row),
            pl.BlockSpec((tile, A_WIDTH), row),
            pl.BlockSpec((seq_per_step, B_HEADS, B_DK, B_DV), state),
        ],
        out_shape=[
            jax.ShapeDtypeStruct((n_seq * seq, D_MODEL), F32),
            jax.ShapeDtypeStruct((n_seq * seq, A_WIDTH), F32),
            jax.ShapeDtypeStruct((n_seq, B_HEADS, B_DK, B_DV), F32),
        ],
        scratch_shapes=_mixproj_scratch(group, n_groups),
        compiler_params=pltpu.CompilerParams(
            dimension_semantics=("arbitrary",), vmem_limit_bytes=VMEM_LIMIT),
        name="mixproj_sample",
    )(x1, s0, g2, wp, wpt, wout, *consts)


def _mix_consts(a_ws, a_bs, a_vnorm, a_onorm, b_wa2, b_ba, b_onorm, chunk, seq_len):
    n_rep = ROWS // seq_len
    tril = jnp.tril(jnp.ones((seq_len, seq_len), dtype=bool))
    w = jnp.where(tril[None], a_ws[:, :seq_len, :seq_len], 0.0)
    if n_rep > 1:
        w = jax.vmap(lambda wg: jnp.kron(jnp.eye(n_rep, dtype=wg.dtype), wg))(w)
    bias = jnp.tile(a_bs[:, :seq_len], (1, n_rep))
    bias = jnp.broadcast_to(bias[:, :, None], (A_GROUPS, ROWS, LANES))
    wa2 = jnp.zeros((LANES, B_HEADS * B_DK), F32).at[:B_LOWRANK].set(b_wa2)
    return (
        w.astype(BF),
        bias,
        a_vnorm.reshape(1, A_WIDTH),
        a_onorm.reshape(1, A_WIDTH),
        wa2.astype(BF),
        b_ba.reshape(1, B_HEADS * B_DK),
        b_onorm.reshape(1, B_HEADS * B_DV),
        jnp.asarray(_cum_matrices(chunk), dtype=BF),
    )


def kernel(x_prompt, x_sample, state_gla, ffn1_norm, ffn1_w_in, ffn1_w_out, mix_norm, w_in,
           a_ws, a_bs, a_vnorm, a_onorm, b_wa2, b_ba, b_onorm, w_out, ffn2_norm, ffn2_w_in,
           ffn2_w_out, final_norm):
    batch, seq, _ = x_prompt.shape
    n_dec, dec_seq, _ = x_sample.shape
    assert ffn1_norm.shape[0] == 1, "single-layer kernel"
    assert seq % A_CHUNK == 0 and ROWS % dec_seq == 0 and dec_seq <= B_CHUNK

    g1 = ffn1_norm.reshape(1, D_MODEL)
    wi1 = ffn1_w_in.reshape(D_MODEL, 2 * D_FF)
    wo1 = ffn1_w_out.reshape(D_FF, D_MODEL)
    g2 = mix_norm.reshape(1, D_MODEL)
    w_in2d = w_in.reshape(D_MODEL, IN_COLS)
    w_out2d = w_out.reshape(D_MODEL, D_MODEL)
    g3 = ffn2_norm.reshape(1, D_MODEL)
    wi2 = ffn2_w_in.reshape(D_MODEL, 2 * D_FF)
    wo2 = ffn2_w_out.reshape(D_FF, D_MODEL)
    gf = final_norm.reshape(1, D_MODEL)
    mix_args = (a_ws[0], a_bs[0], a_vnorm[0], a_onorm[0], b_wa2[0], b_ba[0], b_onorm[0])
    consts_p = _mix_consts(*mix_args, chunk=B_CHUNK, seq_len=A_CHUNK)
    consts_s = _mix_consts(*mix_args, chunk=dec_seq, seq_len=dec_seq)

    xp = x_prompt.reshape(batch * seq, D_MODEL)
    xs = x_sample.reshape(n_dec * dec_seq, D_MODEL)
    s0 = state_gla.reshape(state_gla.shape[1:])

    x1p, x1s, wp, wpt, wout = _ffn_call(
        xp, xs, g1, wi1, wo1, gf, False, "ffn1",
        cast_jobs=(_cast_job(w_in2d, 0, P_MAIN),
                   _cast_job(w_in2d, P_MAIN // (P_COLS - P_MAIN), P_COLS - P_MAIN),
                   _cast_job(w_out2d)))
    x2p, st_p, wi2_bf, wo2_bf = _mixproj_prompt_call(
        x1p, g2, wp, wpt, wout, consts_p, batch, seq, tile=1024, group=512,
        cast_jobs=(_cast_job(wi2), _cast_job(wo2)))
    x2s, va_s, st_s = _mixproj_sample_call(x1s, s0, g2, wp, wpt, wout, consts_s, n_dec, dec_seq,
                                           tile=256, group=128)
    out_p, out_s = _ffn_call(x2p, x2s, g3, wi2_bf, wo2_bf, gf, True, "ffn2")

    return (
        out_p.reshape(batch, seq, D_MODEL),
        out_s.reshape(n_dec, dec_seq, D_MODEL),
        st_p[None],
        st_s[None],
        va_s.reshape(1, n_dec, dec_seq, A_WIDTH),
    )
```

```python
import functools

import numpy as np
import jax
import jax.numpy as jnp
from jax import lax
from jax.experimental import pallas as pl
from jax.experimental.pallas import tpu as pltpu

D_MODEL = 1024
D_FF = 2816
A_WIDTH = 512
A_GROUPS = 4
A_CHUNK = 128
B_HEADS = 4
B_DK = 64
B_DV = 128
B_LOWRANK = 16
B_GATE_NORMALIZER = 16.0
B_CHUNK = 64
EPS = 1e-6

OFF_U, OFF_V, OFF_Q, OFF_K, OFF_VB, OFF_R, OFF_LR = 0, 512, 1024, 1280, 1536, 2048, 2560
IN_COLS = 2576
P_COLS = 2688

LANES = 128
MXU_DIM = 256
ROWS = 128
FF_CHUNK = MXU_DIM
N_FF_CHUNKS = D_FF // FF_CHUNK
VMEM_LIMIT = 56 * 1024 * 1024

FFN_TILE = 1024
FFN_HALF = FFN_TILE // 2
WEIGHT_SLOT_ROWS = 256
P_MAIN = (IN_COLS // LANES) * LANES
CAST_BLOCKS = 16
MIX_STAGGER = 6

BF = jnp.bfloat16
F32 = jnp.float32


def _dot(a, b):
    return jnp.dot(a, b, preferred_element_type=F32)


def _dot_nt(a, b):
    return lax.dot_general(a, b, (((1,), (1,)), ((), ())), preferred_element_type=F32)


def _rmsnorm(x, g):
    return (x * lax.rsqrt(jnp.mean(x * x, axis=-1, keepdims=True) + EPS)) * g


def _gelu_tanh(x):
    c = np.float32(np.sqrt(2.0 / np.pi))
    ca = np.float32(np.sqrt(2.0 / np.pi) * 0.044715)
    t = jnp.tanh(x * (c + ca * (x * x)))
    hx = 0.5 * x
    return hx + hx * t


def _silu(x):
    return x * (1.0 / (1.0 + jnp.exp(-x)))


def _log_sigmoid(x):
    return jnp.minimum(x, 0.0) - jnp.log(1.0 + jnp.exp(-jnp.abs(x)))


def _staggered(gens, stagger):
    live = [True] * len(gens)
    t = 0
    while any(live):
        for i, gen in enumerate(gens):
            if live[i] and t >= i * stagger:
                try:
                    next(gen)
                except StopIteration:
                    live[i] = False
        t += 1
        yield


def _cast_rows(w):
    rows, cols = w.shape
    blk = (rows // CAST_BLOCKS, cols)
    return dict(w=w, src_block=blk, src_index=lambda r: (r, 0), dst_block=blk,
                dst_index=lambda r: (r, 0), out=jax.ShapeDtypeStruct(w.shape, BF), transposed=None)


def _cast_transposed(w_t, row0, n_rows, out_cols, valid_cols=None):
    cols = w_t.shape[1]
    first, last = row0 // n_rows, (row0 + out_cols) // n_rows - 1
    return dict(w=w_t, src_block=(n_rows, cols), dst_block=(cols, n_rows),
                src_index=lambda r: (jnp.minimum(first + r, last), 0),
                dst_index=lambda r: (0, jnp.minimum(r, last - first)),
                out=jax.ShapeDtypeStruct((cols, out_cols), BF),
                transposed=n_rows if valid_cols is None else valid_cols)


def _cast_specs(jobs, row_block):
    in_specs = [pl.BlockSpec(j["src_block"], lambda *ids, j=j: j["src_index"](row_block(*ids)))
                for j in jobs]
    out_specs = [pl.BlockSpec(j["dst_block"], lambda *ids, j=j: j["dst_index"](row_block(*ids)))
                 for j in jobs]
    return in_specs, out_specs, [j["out"] for j in jobs]


def _cast_blocks(src_refs, dst_refs, transposed):
    for src, dst, valid in zip(src_refs, dst_refs, transposed):
        x = src[...]
        if valid is not None:
            x = x.T
            if valid < x.shape[1]:
                lane = lax.broadcasted_iota(jnp.int32, x.shape, 1)
                x = jnp.where(lane < valid, x, 0.0)
        dst[...] = x.astype(BF)


def _weight_chunks(wi_hbm, wo_hbm, wi_ref, wo_ref):
    chunks = []
    for r0 in range(0, D_MODEL, WEIGHT_SLOT_ROWS):
        for c0 in range(0, 2 * D_FF, D_MODEL):
            w = min(D_MODEL, 2 * D_FF - c0)
            view = (pl.ds(r0, WEIGHT_SLOT_ROWS), pl.ds(c0, w))
            chunks.append((wi_hbm.at[view], wi_ref.at[view], w))
    for r0 in range(0, D_FF, WEIGHT_SLOT_ROWS):
        view = (pl.ds(r0, WEIGHT_SLOT_ROWS), pl.ds(0, D_MODEL))
        chunks.append((wo_hbm.at[view], wo_ref.at[view], D_MODEL))
    return chunks


def _load_weights_bf16(wi_hbm, wo_hbm, wi_ref, wo_ref, stage_ref, sem):
    chunks = _weight_chunks(wi_hbm, wo_hbm, wi_ref, wo_ref)
    n_slots = stage_ref.shape[0] // WEIGHT_SLOT_ROWS

    def slot(k):
        return stage_ref.at[pl.ds((k % n_slots) * WEIGHT_SLOT_ROWS, WEIGHT_SLOT_ROWS),
                            pl.ds(0, chunks[k][2])]

    def copy(k):
        return pltpu.make_async_copy(chunks[k][0], slot(k), sem.at[k % n_slots])

    for k in range(n_slots):
        copy(k).start()
    for k in range(len(chunks)):
        copy(k).wait()
        chunks[k][1][...] = slot(k)[...].astype(BF)
        if k + n_slots < len(chunks):
            copy(k + n_slots).start()


def _ffn_rows(load_x, g_ref, wi_ref, wo_ref, acc_ref, rows, finish):
    h = _rmsnorm(load_x(rows), g_ref[...]).astype(BF)
    yield
    for j in range(N_FF_CHUNKS):
        cs = slice(j * FF_CHUNK, (j + 1) * FF_CHUNK)
        gate = _dot(h, wi_ref[:, cs])
        up = _dot(h, wi_ref[:, D_FF + j * FF_CHUNK:D_FF + (j + 1) * FF_CHUNK])
        act = (_silu(gate) * up).astype(BF)
        part = _dot(act, wo_ref[cs, :])
        if j == 0:
            acc_ref[rows, :] = part
        else:
            acc_ref[rows, :] += part
        yield
    finish(load_x(rows), acc_ref[rows, :])


def _ffn_kernel(*refs, final_norm, stream_weights, cast_modes):
    n_cast = len(cast_modes)
    xa_ref, xb_ref, g_ref, wi_in, wo_in, gf_ref = refs[:6]
    cast_src = refs[6:6 + n_cast]
    oa_ref, ob_ref = refs[6 + n_cast:8 + n_cast]
    cast_dst = refs[8 + n_cast:8 + 2 * n_cast]
    scratch = refs[8 + 2 * n_cast:]
    first = pl.program_id(0) == 0

    if stream_weights:
        wi_ref, wo_ref, acc_ref, sem = scratch

        @pl.when(first)
        def _():
            _load_weights_bf16(wi_in, wo_in, wi_ref, wo_ref, acc_ref, sem)
    else:
        (acc_ref,) = scratch
        wi_ref, wo_ref = wi_in, wo_in

    def load_x(rows):
        return jnp.where(first, xb_ref[rows, :], xa_ref[rows, :])

    def half(i):
        rows = slice(i * FFN_HALF, (i + 1) * FFN_HALF)

        def finish(x, acc):
            y = x + 0.5 * acc
            oa_ref[rows, :] = _rmsnorm(y, gf_ref[...]) if final_norm else y

        return _ffn_rows(load_x, g_ref, wi_ref, wo_ref, acc_ref, rows, finish)

    for _ in _staggered([half(i) for i in range(FFN_TILE // FFN_HALF)], 1):
        pass
    _cast_blocks(cast_src, cast_dst, cast_modes)

    @pl.when(first)
    def _():
        ob_ref[...] = oa_ref[...]


def _resident(shape):
    nd = len(shape)
    return pl.BlockSpec(shape, lambda *_: (0,) * nd, pipeline_mode=pl.Buffered(1))


def _ffn_call(xa, xb, g, wi, wo, gf, final_norm, name, cast_jobs=()):
    ta = xa.shape[0]
    n_a = ta // FFN_TILE
    assert ta % FFN_TILE == 0 and xb.shape[0] == FFN_TILE
    assert not cast_jobs or n_a == CAST_BLOCKS
    stream_weights = wi.dtype == F32
    tile_a = lambda i: (jnp.maximum(i - 1, 0), 0)
    cast_in, cast_out, cast_shapes = _cast_specs(cast_jobs, lambda i: jnp.maximum(i - 1, 0))
    if stream_weights:
        w_specs = [pl.BlockSpec(memory_space=pl.ANY)] * 2
        w_scratch = [pltpu.VMEM(wi.shape, BF), pltpu.VMEM(wo.shape, BF)]
        sems = [pltpu.SemaphoreType.DMA((FFN_TILE // WEIGHT_SLOT_ROWS,))]
    else:
        w_specs = [_resident(wi.shape), _resident(wo.shape)]
        w_scratch, sems = [], []
    return pl.pallas_call(
        functools.partial(_ffn_kernel, final_norm=final_norm, stream_weights=stream_weights,
                          cast_modes=tuple(j["transposed"] for j in cast_jobs)),
        grid=(1 + n_a,),
        in_specs=[
            pl.BlockSpec((FFN_TILE, D_MODEL), tile_a),
            _resident((FFN_TILE, D_MODEL)),
            _resident((1, D_MODEL)),
            *w_specs,
            _resident((1, D_MODEL)),
            *cast_in,
        ],
        out_specs=[
            pl.BlockSpec((FFN_TILE, D_MODEL), tile_a),
            pl.BlockSpec((FFN_TILE, D_MODEL), lambda i: (0, 0)),
            *cast_out,
        ],
        out_shape=[
            jax.ShapeDtypeStruct((ta, D_MODEL), F32),
            jax.ShapeDtypeStruct((FFN_TILE, D_MODEL), F32),
            *cast_shapes,
        ],
        scratch_shapes=w_scratch + [pltpu.VMEM((FFN_TILE, D_MODEL), F32)] + sems,
        compiler_params=pltpu.CompilerParams(
            dimension_semantics=("arbitrary",), vmem_limit_bytes=VMEM_LIMIT),
        name=name,
    )(xa, xb, g, wi, wo, gf, *[j["w"] for j in cast_jobs])


def _cum_matrices(chunk):
    r = np.arange(ROWS)[:, None]
    k = np.arange(ROWS)[None, :]
    same = (r // chunk) == (k // chunk)
    m_cum = same & (k <= r)
    m_ref = same & ((k % chunk) <= (chunk // 2))
    m_last = same
    m = np.concatenate([m_cum, m_ref, m_last], axis=0).astype(np.float32)
    return np.concatenate([m, m], axis=1)


def _mix_block(cols, chunk, consts, get_state, put_state, emit):
    (wmix_ref, bias_ref, vnorm_ref, anorm_ref, wa2_ref, ba_ref, onorm_ref, cum_ref) = consts
    n_chunks = ROWS // chunk
    shift = int(np.log2(chunk))

    z = _dot(cols(OFF_LR, LANES).astype(BF), wa2_ref[...]) + ba_ref[...]
    la = _log_sigmoid(z) * (1.0 / B_GATE_NORMALIZER)
    la_hi = la.astype(BF)
    la_lo = (la - la_hi.astype(F32)).astype(BF)
    sums = _dot(cum_ref[...], jnp.concatenate([la_hi, la_lo], axis=0))
    b = sums[0:ROWS]
    b_ref = sums[ROWS:2 * ROWS]
    b_last = sums[2 * ROWS:3 * ROWS]

    ya = []
    va = []
    ssq = None
    for g in range(A_GROUPS):
        gs = slice(g * LANES, (g + 1) * LANES)
        u = _gelu_tanh(cols(OFF_U + g * LANES, LANES))
        v = _rmsnorm(_gelu_tanh(cols(OFF_V + g * LANES, LANES)), vnorm_ref[:, gs])
        zg = _dot(wmix_ref[g], v.astype(BF)) + bias_ref[g]
        yg = u * zg
        s = jnp.sum(yg * yg, axis=-1, keepdims=True)
        ssq = s if ssq is None else ssq + s
        ya.append(yg)
        va.append(v)
        yield
    inv = lax.rsqrt(ssq * (1.0 / A_WIDTH) + EPS)
    ya = [(ya[g] * inv) * anorm_ref[:, g * LANES:(g + 1) * LANES] for g in range(A_GROUPS)]
    gate = [_silu(cols(OFF_R + h * B_DV, B_DV)) for h in range(B_HEADS)]
    yield

    ri = lax.broadcasted_iota(jnp.int32, (ROWS, ROWS), 0)
    ci = lax.broadcasted_iota(jnp.int32, (ROWS, ROWS), 1)
    rchunk = lax.shift_right_logical(ri, shift)
    cchunk = lax.shift_right_logical(ci, shift)
    causal = (rchunk == cchunk) & (ci <= ri)
    lane = lax.broadcasted_iota(jnp.int32, (ROWS, LANES), 1)
    head0 = lane < B_DK

    yb = []
    for pair in range(B_HEADS // 2):
        ls = slice(pair * LANES, (pair + 1) * LANES)
        q = cols(OFF_Q + pair * LANES, LANES) * np.float32(B_DK ** -0.5)
        k = cols(OFF_K + pair * LANES, LANES)
        vv_bf = cols(OFF_VB + pair * 2 * B_DV, 2 * B_DV).astype(BF)
        bp, brp, blp = b[:, ls], b_ref[:, ls], b_last[:, ls]
        qs = q * jnp.exp(bp - brp)
        ks = (k * jnp.exp(brp - bp)).astype(BF)
        kl_t = (k * jnp.exp(blp - bp)).T
        bl_t = blp.T
        qb = (q * jnp.exp(bp)).astype(BF)
        yield

        o_heads = []
        for hh in range(2):
            msk = head0 if hh == 0 else jnp.logical_not(head0)
            qh = jnp.where(msk, qs, 0.0).astype(BF)
            sc = jnp.where(causal, _dot_nt(qh, ks), 0.0).astype(BF)
            o_heads.append(_dot(sc, vv_bf[:, hh * B_DV:(hh + 1) * B_DV]))
        o = jnp.concatenate(o_heads, axis=1)

        o_inter = []
        for j in range(n_chunks):
            s_prev = get_state(pair, j)
            o_inter.append(_dot(qb[j * chunk:(j + 1) * chunk], _blockdiag_bf16(*s_prev)))
            in_chunk = cchunk == j
            upd = _dot(jnp.where(in_chunk, kl_t, 0.0).astype(BF), vv_bf)
            dec = jnp.exp(bl_t[:, j * chunk:j * chunk + 1])
            put_state(pair, j, tuple(
                dec[hh * B_DK:(hh + 1) * B_DK] * s_prev[hh]
                + upd[hh * B_DK:(hh + 1) * B_DK, hh * B_DV:(hh + 1) * B_DV] for hh in range(2)))
        o = o + jnp.concatenate(o_inter, axis=0)

        for hh in range(2):
            h_idx = 2 * pair + hh
            hs = slice(h_idx * B_DV, (h_idx + 1) * B_DV)
            oh = _rmsnorm(o[:, hh * B_DV:(hh + 1) * B_DV], onorm_ref[:, hs])
            yb.append(oh * gate[h_idx])
        yield

    emit(jnp.concatenate(ya + yb, axis=1).astype(BF), jnp.concatenate(va, axis=1))


def _blockdiag_bf16(s0, s1):
    s0 = s0.astype(BF)
    s1 = s1.astype(BF)
    zero = jnp.zeros_like(s0)
    return jnp.concatenate(
        [jnp.concatenate([s0, zero], axis=1), jnp.concatenate([zero, s1], axis=1)], axis=0)


def _in_proj(x1_ref, g2_ref, wp_refs, p_scr, g, rows):
    wp_ref, wpt_ref = wp_refs
    h = _rmsnorm(x1_ref[rows, :], g2_ref[...]).astype(BF)
    for c0 in range(0, P_MAIN, MXU_DIM):
        cs = slice(c0, c0 + MXU_DIM)
        p_scr[g, :, cs] = _dot(h, wp_ref[:, cs])
    p_scr[g, :, P_MAIN:P_COLS] = _dot(h, wpt_ref[...])


def _out_proj(x1_ref, y_scr, wout_ref, x2_ref, g, rows):
    y = y_scr[g]
    for c0 in range(0, D_MODEL, MXU_DIM):
        cs = slice(c0, c0 + MXU_DIM)
        x2_ref[rows, cs] = x1_ref[rows, cs] + _dot(y, wout_ref[:, cs])


def _mixproj_tile(x1_ref, g2_ref, wp_refs, wout_ref, x2_ref, p_scr, y_scr, group, n_groups,
                  make_block):
    rows = [slice(g * group, (g + 1) * group) for g in range(n_groups)]
    for g in range(n_groups):
        _in_proj(x1_ref, g2_ref, wp_refs, p_scr, g, rows[g])
    blocks = [make_block(g, i) for g in range(n_groups) for i in range(group // ROWS)]
    for _ in _staggered(blocks, MIX_STAGGER):
        pass
    for g in range(n_groups):
        _out_proj(x1_ref, y_scr, wout_ref, x2_ref, g, rows[g])


def _block_cols(p_scr, g, i):
    return lambda off, width: p_scr[g, i * ROWS:(i + 1) * ROWS, off:off + width]


def _mixproj_prompt_kernel(*refs, group, n_groups, cast_modes):
    n_cast = len(cast_modes)
    (x1_ref, g2_ref, wp_ref, wpt_ref, wout_ref, wmix_ref, bias_ref, vnorm_ref, anorm_ref, wa2_ref,
     ba_ref, onorm_ref, cum_ref) = refs[:13]
    cast_src = refs[13:13 + n_cast]
    x2_ref, st_ref = refs[13 + n_cast:15 + n_cast]
    cast_dst = refs[15 + n_cast:15 + 2 * n_cast]
    p_scr, y_scr, s_scr = refs[15 + 2 * n_cast:]

    @pl.when(pl.program_id(1) == 0)
    def _():
        s_scr[...] = jnp.zeros_like(s_scr)

    consts = (wmix_ref, bias_ref, vnorm_ref, anorm_ref, wa2_ref, ba_ref, onorm_ref, cum_ref)
    cur = {pair: (s_scr[2 * pair], s_scr[2 * pair + 1]) for pair in range(B_HEADS // 2)}

    def get_state(pair, j):
        return cur[pair]

    def put_state(pair, j, s):
        cur[pair] = s

    def make_block(g, i):
        def emit(y, va):
            y_scr[g, i * ROWS:(i + 1) * ROWS, :] = y
        return _mix_block(_block_cols(p_scr, g, i), B_CHUNK, consts, get_state, put_state, emit)

    _mixproj_tile(x1_ref, g2_ref, (wp_ref, wpt_ref), wout_ref, x2_ref, p_scr, y_scr, group,
                  n_groups, make_block)
    _cast_blocks(cast_src, cast_dst, cast_modes)

    for pair in range(B_HEADS // 2):
        for hh in range(2):
            s_scr[2 * pair + hh] = cur[pair][hh]
            st_ref[0, 2 * pair + hh] = cur[pair][hh]


def _mixproj_sample_kernel(x1_ref, s0_ref, g2_ref, wp_ref, wpt_ref, wout_ref, wmix_ref, bias_ref,
                           vnorm_ref, anorm_ref, wa2_ref, ba_ref, onorm_ref, cum_ref, x2_ref,
                           va_ref, st_ref, p_scr, y_scr, *, chunk, group, n_groups):
    consts = (wmix_ref, bias_ref, vnorm_ref, anorm_ref, wa2_ref, ba_ref, onorm_ref, cum_ref)
    n_blk = group // ROWS
    per_blk = ROWS // chunk

    def make_block(g, i):
        blk = g * n_blk + i

        def get_state(pair, j):
            n = blk * per_blk + j
            return s0_ref[n, 2 * pair], s0_ref[n, 2 * pair + 1]

        def put_state(pair, j, s):
            n = blk * per_blk + j
            st_ref[n, 2 * pair] = s[0]
            st_ref[n, 2 * pair + 1] = s[1]

        def emit(y, va):
            y_scr[g, i * ROWS:(i + 1) * ROWS, :] = y
            va_ref[blk * ROWS:(blk + 1) * ROWS, :] = va

        return _mix_block(_block_cols(p_scr, g, i), chunk, consts, get_state, put_state, emit)

    _mixproj_tile(x1_ref, g2_ref, (wp_ref, wpt_ref), wout_ref, x2_ref, p_scr, y_scr, group,
                  n_groups, make_block)


def _mix_const_specs():
    return [
        _resident((A_GROUPS, ROWS, ROWS)),
        _resident((A_GROUPS, ROWS, LANES)),
        _resident((1, A_WIDTH)),
        _resident((1, A_WIDTH)),
        _resident((LANES, B_HEADS * B_DK)),
        _resident((1, B_HEADS * B_DK)),
        _resident((1, B_HEADS * B_DV)),
        _resident((3 * ROWS, 2 * ROWS)),
    ]


def _proj_specs():
    return [
        _resident((1, D_MODEL)),
        _resident((D_MODEL, P_MAIN)),
        _resident((D_MODEL, P_COLS - P_MAIN)),
        _resident((D_MODEL, D_MODEL)),
    ]


def _mixproj_scratch(group, n_groups):
    return [pltpu.VMEM((n_groups, group, P_COLS), F32),
            pltpu.VMEM((n_groups, group, D_MODEL), BF)]


def _mixproj_prompt_call(x1, g2, wp, wpt, wout, consts, batch, seq, tile, group, cast_jobs=()):
    n_tiles = seq // tile
    n_groups = tile // group
    assert not cast_jobs or batch * n_tiles == CAST_BLOCKS
    row = lambda b, s: (b * n_tiles + s, 0)
    cast_in, cast_out, cast_shapes = _cast_specs(cast_jobs, lambda b, s: b * n_tiles + s)
    return pl.pallas_call(
        functools.partial(_mixproj_prompt_kernel, group=group, n_groups=n_groups,
                          cast_modes=tuple(j["transposed"] for j in cast_jobs)),
        grid=(batch, n_tiles),
        in_specs=[pl.BlockSpec((tile, D_MODEL), row)] + _proj_specs() + _mix_const_specs()
        + cast_in,
        out_specs=[
            pl.BlockSpec((tile, D_MODEL), row),
            pl.BlockSpec((1, B_HEADS, B_DK, B_DV), lambda b, s: (b, 0, 0, 0)),
            *cast_out,
        ],
        out_shape=[
            jax.ShapeDtypeStruct((batch * seq, D_MODEL), F32),
            jax.ShapeDtypeStruct((batch, B_HEADS, B_DK, B_DV), F32),
            *cast_shapes,
        ],
        scratch_shapes=_mixproj_scratch(group, n_groups)
        + [pltpu.VMEM((B_HEADS, B_DK, B_DV), F32)],
        compiler_params=pltpu.CompilerParams(
            dimension_semantics=("arbitrary", "arbitrary"), vmem_limit_bytes=VMEM_LIMIT),
        name="mixproj_prompt",
    )(x1, g2, wp, wpt, wout, *consts, *[j["w"] for j in cast_jobs])


def _mixproj_sample_call(x1, s0, g2, wp, wpt, wout, consts, n_seq, seq, tile, group):
    seq_per_step = tile // seq
    n_groups = tile // group
    row = lambda i: (i, 0)
    state = lambda i: (i, 0, 0, 0)
    return pl.pallas_call(
        functools.partial(_mixproj_sample_kernel, chunk=seq, group=group, n_groups=n_groups),
        grid=(n_seq // seq_per_step,),
        in_specs=[
            pl.BlockSpec((tile, D_MODEL), row),
            pl.BlockSpec((seq_per_step, B_HEADS, B_DK, B_DV), state),
        ] + _proj_specs() + _mix_const_specs(),
        out_specs=[
            pl.BlockSpec((tile, D_MODEL), row),
            pl.BlockSpec((tile, A_WIDTH), row),
            pl.BlockSpec((seq_per_step, B_HEADS, B_DK, B_DV), state),
        ],
        out_shape=[
            jax.ShapeDtypeStruct((n_seq * seq, D_MODEL), F32),
            jax.ShapeDtypeStruct((n_seq * seq, A_WIDTH), F32),
            jax.ShapeDtypeStruct((n_seq, B_HEADS, B_DK, B_DV), F32),
        ],
        scratch_shapes=_mixproj_scratch(group, n_groups),
        compiler_params=pltpu.CompilerParams(
            dimension_semantics=("arbitrary",), vmem_limit_bytes=VMEM_LIMIT),
        name="mixproj_sample",
    )(x1, s0, g2, wp, wpt, wout, *consts)


def _mix_consts(a_ws, a_bs, a_vnorm, a_onorm, b_wa2, b_ba, b_onorm, chunk, seq_len):
    n_rep = ROWS // seq_len
    tril = jnp.tril(jnp.ones((seq_len, seq_len), dtype=bool))
    w = jnp.where(tril[None], a_ws[:, :seq_len, :seq_len], 0.0)
    if n_rep > 1:
        w = jax.vmap(lambda wg: jnp.kron(jnp.eye(n_rep, dtype=wg.dtype), wg))(w)
    bias = jnp.tile(a_bs[:, :seq_len], (1, n_rep))
    bias = jnp.broadcast_to(bias[:, :, None], (A_GROUPS, ROWS, LANES))
    wa2 = jnp.zeros((LANES, B_HEADS * B_DK), F32).at[:B_LOWRANK].set(b_wa2)
    return (
        w.astype(BF),
        bias,
        a_vnorm.reshape(1, A_WIDTH),
        a_onorm.reshape(1, A_WIDTH),
        wa2.astype(BF),
        b_ba.reshape(1, B_HEADS * B_DK),
        b_onorm.reshape(1, B_HEADS * B_DV),
        jnp.asarray(_cum_matrices(chunk), dtype=BF),
    )


def kernel(x_prompt, x_sample, state_gla, ffn1_norm, ffn1_w_in, ffn1_w_out, mix_norm, w_in,
           a_ws, a_bs, a_vnorm, a_onorm, b_wa2, b_ba, b_onorm, w_out, ffn2_norm, ffn2_w_in,
           ffn2_w_out, final_norm):
    batch, seq, _ = x_prompt.shape
    n_dec, dec_seq, _ = x_sample.shape
    assert ffn1_norm.shape[0] == 1, "single-layer kernel"
    assert seq % A_CHUNK == 0 and ROWS % dec_seq == 0 and dec_seq <= B_CHUNK

    g1 = ffn1_norm.reshape(1, D_MODEL)
    wi1 = ffn1_w_in.reshape(D_MODEL, 2 * D_FF)
    wo1 = ffn1_w_out.reshape(D_FF, D_MODEL)
    g2 = mix_norm.reshape(1, D_MODEL)
    w_in_t = jnp.swapaxes(w_in, 1, 2).reshape(IN_COLS, D_MODEL)
    w_out2d = w_out.reshape(D_MODEL, D_MODEL)
    g3 = ffn2_norm.reshape(1, D_MODEL)
    wi2 = ffn2_w_in.reshape(D_MODEL, 2 * D_FF)
    wo2 = ffn2_w_out.reshape(D_FF, D_MODEL)
    gf = final_norm.reshape(1, D_MODEL)
    mix_args = (a_ws[0], a_bs[0], a_vnorm[0], a_onorm[0], b_wa2[0], b_ba[0], b_onorm[0])
    consts_p = _mix_consts(*mix_args, chunk=B_CHUNK, seq_len=A_CHUNK)
    consts_s = _mix_consts(*mix_args, chunk=dec_seq, seq_len=dec_seq)

    xp = x_prompt.reshape(batch * seq, D_MODEL)
    xs = x_sample.reshape(n_dec * dec_seq, D_MODEL)
    s0 = state_gla.reshape(state_gla.shape[1:])

    x1p, x1s, wp, wpt, wout = _ffn_call(
        xp, xs, g1, wi1, wo1, gf, False, "ffn1",
        cast_jobs=(_cast_transposed(w_in_t, 0, MXU_DIM, P_MAIN),
                   _cast_transposed(w_in_t, P_MAIN, P_COLS - P_MAIN, P_COLS - P_MAIN,
                                    valid_cols=IN_COLS - P_MAIN),
                   _cast_rows(w_out2d)))
    x2p, st_p, wi2_bf, wo2_bf = _mixproj_prompt_call(
        x1p, g2, wp, wpt, wout, consts_p, batch, seq, tile=1024, group=512,
        cast_jobs=(_cast_rows(wi2), _cast_rows(wo2)))
    x2s, va_s, st_s = _mixproj_sample_call(x1s, s0, g2, wp, wpt, wout, consts_s, n_dec, dec_seq,
                                           tile=256, group=128)
    out_p, out_s = _ffn_call(x2p, x2s, g3, wi2_bf, wo2_bf, gf, True, "ffn2")

    return (
        out_p.reshape(batch, seq, D_MODEL),
        out_s.reshape(n_dec, dec_seq, D_MODEL),
        st_p[None],
        st_s[None],
        va_s.reshape(1, n_dec, dec_seq, A_WIDTH),
    )
```

```python
import functools

import numpy as np
import jax
import jax.numpy as jnp
from jax import lax
from jax.experimental import pallas as pl
from jax.experimental.pallas import tpu as pltpu

D_MODEL = 1024
D_FF = 2816
A_WIDTH = 512
A_GROUPS = 4
A_CHUNK = 128
B_HEADS = 4
B_DK = 64
B_DV = 128
B_LOWRANK = 16
B_GATE_NORMALIZER = 16.0
B_CHUNK = 64
EPS = 1e-6

OFF_U, OFF_V, OFF_Q, OFF_K, OFF_VB, OFF_R, OFF_LR = 0, 512, 1024, 1280, 1536, 2048, 2560
IN_COLS = 2576
P_COLS = 2688

LANES = 128
MXU_DIM = 256
ROWS = 128
FF_CHUNK = MXU_DIM
N_FF_CHUNKS = D_FF // FF_CHUNK
VMEM_LIMIT = 56 * 1024 * 1024

FFN_TILE = 1024
FFN_HALF = FFN_TILE // 2
WEIGHT_SLOT_ROWS = 256
P_MAIN = (IN_COLS // LANES) * LANES
CAST_BLOCKS = 16
MIX_STAGGER = 10

BF = jnp.bfloat16
F32 = jnp.float32


def _dot(a, b):
    return jnp.dot(a, b, preferred_element_type=F32)


def _dot_nt(a, b):
    return lax.dot_general(a, b, (((1,), (1,)), ((), ())), preferred_element_type=F32)


def _rmsnorm(x, g):
    return (x * lax.rsqrt(jnp.mean(x * x, axis=-1, keepdims=True) + EPS)) * g


def _gelu_tanh(x):
    c = np.float32(np.sqrt(2.0 / np.pi))
    ca = np.float32(np.sqrt(2.0 / np.pi) * 0.044715)
    t = jnp.tanh(x * (c + ca * (x * x)))
    hx = 0.5 * x
    return hx + hx * t


def _silu(x):
    return x * (1.0 / (1.0 + jnp.exp(-x)))


def _log_sigmoid(x):
    return jnp.minimum(x, 0.0) - jnp.log(1.0 + jnp.exp(-jnp.abs(x)))


def _staggered(gens, stagger):
    live = [True] * len(gens)
    t = 0
    while any(live):
        for i, gen in enumerate(gens):
            if live[i] and t >= i * stagger:
                try:
                    next(gen)
                except StopIteration:
                    live[i] = False
        t += 1
        yield


def _cast_rows(w):
    rows, cols = w.shape
    blk = (rows // CAST_BLOCKS, cols)
    return dict(w=w, src_block=blk, src_index=lambda r: (r, 0), dst_block=blk,
                dst_index=lambda r: (r, 0), out=jax.ShapeDtypeStruct(w.shape, BF), transposed=None)


def _cast_transposed(w_t, row0, n_rows, out_cols, valid_cols=None):
    cols = w_t.shape[1]
    first, last = row0 // n_rows, (row0 + out_cols) // n_rows - 1
    return dict(w=w_t, src_block=(n_rows, cols), dst_block=(cols, n_rows),
                src_index=lambda r: (jnp.minimum(first + r, last), 0),
                dst_index=lambda r: (0, jnp.minimum(r, last - first)),
                out=jax.ShapeDtypeStruct((cols, out_cols), BF),
                transposed=n_rows if valid_cols is None else valid_cols)


def _cast_specs(jobs, row_block):
    in_specs = [pl.BlockSpec(j["src_block"], lambda *ids, j=j: j["src_index"](row_block(*ids)))
                for j in jobs]
    out_specs = [pl.BlockSpec(j["dst_block"], lambda *ids, j=j: j["dst_index"](row_block(*ids)))
                 for j in jobs]
    return in_specs, out_specs, [j["out"] for j in jobs]


def _cast_blocks(src_refs, dst_refs, transposed):
    for src, dst, valid in zip(src_refs, dst_refs, transposed):
        x = src[...]
        if valid is not None:
            x = x.T
            if valid < x.shape[1]:
                lane = lax.broadcasted_iota(jnp.int32, x.shape, 1)
                x = jnp.where(lane < valid, x, 0.0)
        dst[...] = x.astype(BF)


def _weight_chunks(wi_hbm, wo_hbm, wi_ref, wo_ref):
    chunks = []
    for r0 in range(0, D_MODEL, WEIGHT_SLOT_ROWS):
        for c0 in range(0, 2 * D_FF, D_MODEL):
            w = min(D_MODEL, 2 * D_FF - c0)
            view = (pl.ds(r0, WEIGHT_SLOT_ROWS), pl.ds(c0, w))
            chunks.append((wi_hbm.at[view], wi_ref.at[view], w))
    for r0 in range(0, D_FF, WEIGHT_SLOT_ROWS):
        view = (pl.ds(r0, WEIGHT_SLOT_ROWS), pl.ds(0, D_MODEL))
        chunks.append((wo_hbm.at[view], wo_ref.at[view], D_MODEL))
    return chunks


def _load_weights_bf16(wi_hbm, wo_hbm, wi_ref, wo_ref, stage_ref, sem):
    chunks = _weight_chunks(wi_hbm, wo_hbm, wi_ref, wo_ref)
    n_slots = stage_ref.shape[0] // WEIGHT_SLOT_ROWS

    def slot(k):
        return stage_ref.at[pl.ds((k % n_slots) * WEIGHT_SLOT_ROWS, WEIGHT_SLOT_ROWS),
                            pl.ds(0, chunks[k][2])]

    def copy(k):
        return pltpu.make_async_copy(chunks[k][0], slot(k), sem.at[k % n_slots])

    for k in range(n_slots):
        copy(k).start()
    for k in range(len(chunks)):
        copy(k).wait()
        chunks[k][1][...] = slot(k)[...].astype(BF)
        if k + n_slots < len(chunks):
            copy(k + n_slots).start()


def _ffn_rows(load_x, g_ref, wi_ref, wo_ref, acc_ref, rows, finish):
    h = _rmsnorm(load_x(rows), g_ref[...]).astype(BF)
    yield
    for j in range(N_FF_CHUNKS):
        cs = slice(j * FF_CHUNK, (j + 1) * FF_CHUNK)
        gate = _dot(h, wi_ref[:, cs])
        up = _dot(h, wi_ref[:, D_FF + j * FF_CHUNK:D_FF + (j + 1) * FF_CHUNK])
        act = (_silu(gate) * up).astype(BF)
        part = _dot(act, wo_ref[cs, :])
        if j == 0:
            acc_ref[rows, :] = part
        else:
            acc_ref[rows, :] += part
        yield
    finish(load_x(rows), acc_ref[rows, :])


def _ffn_kernel(*refs, final_norm, stream_weights, cast_modes):
    n_cast = len(cast_modes)
    xa_ref, xb_ref, g_ref, wi_in, wo_in, gf_ref = refs[:6]
    cast_src = refs[6:6 + n_cast]
    oa_ref, ob_ref = refs[6 + n_cast:8 + n_cast]
    cast_dst = refs[8 + n_cast:8 + 2 * n_cast]
    scratch = refs[8 + 2 * n_cast:]
    first = pl.program_id(0) == 0

    if stream_weights:
        wi_ref, wo_ref, acc_ref, sem = scratch

        @pl.when(first)
        def _():
            _load_weights_bf16(wi_in, wo_in, wi_ref, wo_ref, acc_ref, sem)
    else:
        (acc_ref,) = scratch
        wi_ref, wo_ref = wi_in, wo_in

    def load_x(rows):
        return jnp.where(first, xb_ref[rows, :], xa_ref[rows, :])

    def half(i):
        rows = slice(i * FFN_HALF, (i + 1) * FFN_HALF)

        def finish(x, acc):
            y = x + 0.5 * acc
            oa_ref[rows, :] = _rmsnorm(y, gf_ref[...]) if final_norm else y

        return _ffn_rows(load_x, g_ref, wi_ref, wo_ref, acc_ref, rows, finish)

    for _ in _staggered([half(i) for i in range(FFN_TILE // FFN_HALF)], 1):
        pass
    _cast_blocks(cast_src, cast_dst, cast_modes)

    @pl.when(first)
    def _():
        ob_ref[...] = oa_ref[...]


def _resident(shape):
    nd = len(shape)
    return pl.BlockSpec(shape, lambda *_: (0,) * nd, pipeline_mode=pl.Buffered(1))


def _ffn_call(xa, xb, g, wi, wo, gf, final_norm, name, cast_jobs=()):
    ta = xa.shape[0]
    n_a = ta // FFN_TILE
    assert ta % FFN_TILE == 0 and xb.shape[0] == FFN_TILE
    assert not cast_jobs or n_a == CAST_BLOCKS
    stream_weights = wi.dtype == F32
    tile_a = lambda i: (jnp.maximum(i - 1, 0), 0)
    cast_in, cast_out, cast_shapes = _cast_specs(cast_jobs, lambda i: jnp.maximum(i - 1, 0))
    if stream_weights:
        w_specs = [pl.BlockSpec(memory_space=pl.ANY)] * 2
        w_scratch = [pltpu.VMEM(wi.shape, BF), pltpu.VMEM(wo.shape, BF)]
        sems = [pltpu.SemaphoreType.DMA((FFN_TILE // WEIGHT_SLOT_ROWS,))]
    else:
        w_specs = [_resident(wi.shape), _resident(wo.shape)]
        w_scratch, sems = [], []
    return pl.pallas_call(
        functools.partial(_ffn_kernel, final_norm=final_norm, stream_weights=stream_weights,
                          cast_modes=tuple(j["transposed"] for j in cast_jobs)),
        grid=(1 + n_a,),
        in_specs=[
            pl.BlockSpec((FFN_TILE, D_MODEL), tile_a),
            _resident((FFN_TILE, D_MODEL)),
            _resident((1, D_MODEL)),
            *w_specs,
            _resident((1, D_MODEL)),
            *cast_in,
        ],
        out_specs=[
            pl.BlockSpec((FFN_TILE, D_MODEL), tile_a),
            pl.BlockSpec((FFN_TILE, D_MODEL), lambda i: (0, 0)),
            *cast_out,
        ],
        out_shape=[
            jax.ShapeDtypeStruct((ta, D_MODEL), F32),
            jax.ShapeDtypeStruct((FFN_TILE, D_MODEL), F32),
            *cast_shapes,
        ],
        scratch_shapes=w_scratch + [pltpu.VMEM((FFN_TILE, D_MODEL), F32)] + sems,
        compiler_params=pltpu.CompilerParams(
            dimension_semantics=("arbitrary",), vmem_limit_bytes=VMEM_LIMIT),
        name=name,
    )(xa, xb, g, wi, wo, gf, *[j["w"] for j in cast_jobs])


def _cum_matrices(chunk):
    r = np.arange(ROWS)[:, None]
    k = np.arange(ROWS)[None, :]
    same = (r // chunk) == (k // chunk)
    m_cum = same & (k <= r)
    m_ref = same & ((k % chunk) <= (chunk // 2))
    m_last = same
    m = np.concatenate([m_cum, m_ref, m_last], axis=0).astype(np.float32)
    return np.concatenate([m, m], axis=1)


def _mix_block(cols, chunk, consts, get_state, put_state, emit):
    (wmix_ref, bias_ref, vnorm_ref, anorm_ref, wa2_ref, ba_ref, onorm_ref, cum_ref) = consts
    n_chunks = ROWS // chunk
    shift = int(np.log2(chunk))

    z = _dot(cols(OFF_LR, LANES).astype(BF), wa2_ref[...]) + ba_ref[...]
    la = _log_sigmoid(z) * (1.0 / B_GATE_NORMALIZER)
    la_hi = la.astype(BF)
    la_lo = (la - la_hi.astype(F32)).astype(BF)
    sums = _dot(cum_ref[...], jnp.concatenate([la_hi, la_lo], axis=0))
    b = sums[0:ROWS]
    b_ref = sums[ROWS:2 * ROWS]
    b_last = sums[2 * ROWS:3 * ROWS]
    yield

    ya = []
    va = []
    ssq = None
    for g in range(A_GROUPS):
        gs = slice(g * LANES, (g + 1) * LANES)
        u = _gelu_tanh(cols(OFF_U + g * LANES, LANES))
        v = _rmsnorm(_gelu_tanh(cols(OFF_V + g * LANES, LANES)), vnorm_ref[:, gs])
        yield
        zg = _dot(wmix_ref[g], v.astype(BF)) + bias_ref[g]
        yg = u * zg
        s = jnp.sum(yg * yg, axis=-1, keepdims=True)
        ssq = s if ssq is None else ssq + s
        ya.append(yg)
        va.append(v)
        yield
    inv = lax.rsqrt(ssq * (1.0 / A_WIDTH) + EPS)
    ya = [(ya[g] * inv) * anorm_ref[:, g * LANES:(g + 1) * LANES] for g in range(A_GROUPS)]
    yield
    gate = []
    for h in range(B_HEADS):
        gate.append(_silu(cols(OFF_R + h * B_DV, B_DV)))
        if h % 2 == 1:
            yield

    ri = lax.broadcasted_iota(jnp.int32, (ROWS, ROWS), 0)
    ci = lax.broadcasted_iota(jnp.int32, (ROWS, ROWS), 1)
    rchunk = lax.shift_right_logical(ri, shift)
    cchunk = lax.shift_right_logical(ci, shift)
    causal = (rchunk == cchunk) & (ci <= ri)
    lane = lax.broadcasted_iota(jnp.int32, (ROWS, LANES), 1)
    head0 = lane < B_DK

    yb = []
    for pair in range(B_HEADS // 2):
        ls = slice(pair * LANES, (pair + 1) * LANES)
        q = cols(OFF_Q + pair * LANES, LANES) * np.float32(B_DK ** -0.5)
        k = cols(OFF_K + pair * LANES, LANES)
        vv_bf = cols(OFF_VB + pair * 2 * B_DV, 2 * B_DV).astype(BF)
        bp, brp, blp = b[:, ls], b_ref[:, ls], b_last[:, ls]
        qs = q * jnp.exp(bp - brp)
        ks = (k * jnp.exp(brp - bp)).astype(BF)
        kl_t = (k * jnp.exp(blp - bp)).T
        bl_t = blp.T
        qb = (q * jnp.exp(bp)).astype(BF)
        yield

        o_heads = []
        for hh in range(2):
            msk = head0 if hh == 0 else jnp.logical_not(head0)
            qh = jnp.where(msk, qs, 0.0).astype(BF)
            sc = jnp.where(causal, _dot_nt(qh, ks), 0.0).astype(BF)
            o_heads.append(_dot(sc, vv_bf[:, hh * B_DV:(hh + 1) * B_DV]))
            yield
        o = jnp.concatenate(o_heads, axis=1)

        o_inter = []
        for j in range(n_chunks):
            s_prev = get_state(pair, j)
            o_inter.append(_dot(qb[j * chunk:(j + 1) * chunk], _blockdiag_bf16(*s_prev)))
            in_chunk = cchunk == j
            upd = _dot(jnp.where(in_chunk, kl_t, 0.0).astype(BF), vv_bf)
            dec = jnp.exp(bl_t[:, j * chunk:j * chunk + 1])
            put_state(pair, j, tuple(
                dec[hh * B_DK:(hh + 1) * B_DK] * s_prev[hh]
                + upd[hh * B_DK:(hh + 1) * B_DK, hh * B_DV:(hh + 1) * B_DV] for hh in range(2)))
        o = o + jnp.concatenate(o_inter, axis=0)
        yield

        for hh in range(2):
            h_idx = 2 * pair + hh
            hs = slice(h_idx * B_DV, (h_idx + 1) * B_DV)
            oh = _rmsnorm(o[:, hh * B_DV:(hh + 1) * B_DV], onorm_ref[:, hs])
            yb.append(oh * gate[h_idx])
        yield

    emit(jnp.concatenate(ya + yb, axis=1).astype(BF), jnp.concatenate(va, axis=1))


def _blockdiag_bf16(s0, s1):
    s0 = s0.astype(BF)
    s1 = s1.astype(BF)
    zero = jnp.zeros_like(s0)
    return jnp.concatenate(
        [jnp.concatenate([s0, zero], axis=1), jnp.concatenate([zero, s1], axis=1)], axis=0)


def _in_proj(x1_ref, g2_ref, wp_refs, p_scr, g, rows):
    wp_ref, wpt_ref = wp_refs
    h = _rmsnorm(x1_ref[rows, :], g2_ref[...]).astype(BF)
    for c0 in range(0, P_MAIN, MXU_DIM):
        cs = slice(c0, c0 + MXU_DIM)
        p_scr[g, :, cs] = _dot(h, wp_ref[:, cs])
    p_scr[g, :, P_MAIN:P_COLS] = _dot(h, wpt_ref[...])


def _out_proj(x1_ref, y_scr, wout_ref, x2_ref, g, rows):
    y = y_scr[g]
    for c0 in range(0, D_MODEL, MXU_DIM):
        cs = slice(c0, c0 + MXU_DIM)
        x2_ref[rows, cs] = x1_ref[rows, cs] + _dot(y, wout_ref[:, cs])


def _mixproj_tile(x1_ref, g2_ref, wp_refs, wout_ref, x2_ref, p_scr, y_scr, group, n_groups,
                  make_block):
    rows = [slice(g * group, (g + 1) * group) for g in range(n_groups)]
    for g in range(n_groups):
        _in_proj(x1_ref, g2_ref, wp_refs, p_scr, g, rows[g])
    blocks = [make_block(g, i) for g in range(n_groups) for i in range(group // ROWS)]
    for _ in _staggered(blocks, MIX_STAGGER):
        pass
    for g in range(n_groups):
        _out_proj(x1_ref, y_scr, wout_ref, x2_ref, g, rows[g])


def _block_cols(p_scr, g, i):
    return lambda off, width: p_scr[g, i * ROWS:(i + 1) * ROWS, off:off + width]


def _mixproj_prompt_kernel(*refs, group, n_groups, cast_modes):
    n_cast = len(cast_modes)
    (x1_ref, g2_ref, wp_ref, wpt_ref, wout_ref, wmix_ref, bias_ref, vnorm_ref, anorm_ref, wa2_ref,
     ba_ref, onorm_ref, cum_ref) = refs[:13]
    cast_src = refs[13:13 + n_cast]
    x2_ref, st_ref = refs[13 + n_cast:15 + n_cast]
    cast_dst = refs[15 + n_cast:15 + 2 * n_cast]
    p_scr, y_scr, s_scr = refs[15 + 2 * n_cast:]

    @pl.when(pl.program_id(1) == 0)
    def _():
        s_scr[...] = jnp.zeros_like(s_scr)

    consts = (wmix_ref, bias_ref, vnorm_ref, anorm_ref, wa2_ref, ba_ref, onorm_ref, cum_ref)
    cur = {pair: (s_scr[2 * pair], s_scr[2 * pair + 1]) for pair in range(B_HEADS // 2)}

    def get_state(pair, j):
        return cur[pair]

    def put_state(pair, j, s):
        cur[pair] = s

    def make_block(g, i):
        def emit(y, va):
            y_scr[g, i * ROWS:(i + 1) * ROWS, :] = y
        return _mix_block(_block_cols(p_scr, g, i), B_CHUNK, consts, get_state, put_state, emit)

    _mixproj_tile(x1_ref, g2_ref, (wp_ref, wpt_ref), wout_ref, x2_ref, p_scr, y_scr, group,
                  n_groups, make_block)
    _cast_blocks(cast_src, cast_dst, cast_modes)

    for pair in range(B_HEADS // 2):
        for hh in range(2):
            s_scr[2 * pair + hh] = cur[pair][hh]
            st_ref[0, 2 * pair + hh] = cur[pair][hh]


def _mixproj_sample_kernel(x1_ref, s0_ref, g2_ref, wp_ref, wpt_ref, wout_ref, wmix_ref, bias_ref,
                           vnorm_ref, anorm_ref, wa2_ref, ba_ref, onorm_ref, cum_ref, x2_ref,
                           va_ref, st_ref, p_scr, y_scr, *, chunk, group, n_groups):
    consts = (wmix_ref, bias_ref, vnorm_ref, anorm_ref, wa2_ref, ba_ref, onorm_ref, cum_ref)
    n_blk = group // ROWS
    per_blk = ROWS // chunk

    def make_block(g, i):
        blk = g * n_blk + i

        def get_state(pair, j):
            n = blk * per_blk + j
            return s0_ref[n, 2 * pair], s0_ref[n, 2 * pair + 1]

        def put_state(pair, j, s):
            n = blk * per_blk + j
            st_ref[n, 2 * pair] = s[0]
            st_ref[n, 2 * pair + 1] = s[1]

        def emit(y, va):
            y_scr[g, i * ROWS:(i + 1) * ROWS, :] = y
            va_ref[blk * ROWS:(blk + 1) * ROWS, :] = va

        return _mix_block(_block_cols(p_scr, g, i), chunk, consts, get_state, put_state, emit)

    _mixproj_tile(x1_ref, g2_ref, (wp_ref, wpt_ref), wout_ref, x2_ref, p_scr, y_scr, group,
                  n_groups, make_block)


def _mix_const_specs():
    return [
        _resident((A_GROUPS, ROWS, ROWS)),
        _resident((A_GROUPS, ROWS, LANES)),
        _resident((1, A_WIDTH)),
        _resident((1, A_WIDTH)),
        _resident((LANES, B_HEADS * B_DK)),
        _resident((1, B_HEADS * B_DK)),
        _resident((1, B_HEADS * B_DV)),
        _resident((3 * ROWS, 2 * ROWS)),
    ]


def _proj_specs():
    return [
        _resident((1, D_MODEL)),
        _resident((D_MODEL, P_MAIN)),
        _resident((D_MODEL, P_COLS - P_MAIN)),
        _resident((D_MODEL, D_MODEL)),
    ]


def _mixproj_scratch(group, n_groups):
    return [pltpu.VMEM((n_groups, group, P_COLS), F32),
            pltpu.VMEM((n_groups, group, D_MODEL), BF)]


def _mixproj_prompt_call(x1, g2, wp, wpt, wout, consts, batch, seq, tile, group, cast_jobs=()):
    n_tiles = seq // tile
    n_groups = tile // group
    assert not cast_jobs or batch * n_tiles == CAST_BLOCKS
    row = lambda b, s: (b * n_tiles + s, 0)
    cast_in, cast_out, cast_shapes = _cast_specs(cast_jobs, lambda b, s: b * n_tiles + s)
    return pl.pallas_call(
        functools.partial(_mixproj_prompt_kernel, group=group, n_groups=n_groups,
                          cast_modes=tuple(j["transposed"] for j in cast_jobs)),
        grid=(batch, n_tiles),
        in_specs=[pl.BlockSpec((tile, D_MODEL), row)] + _proj_specs() + _mix_const_specs()
        + cast_in,
        out_specs=[
            pl.BlockSpec((tile, D_MODEL), row),
            pl.BlockSpec((1, B_HEADS, B_DK, B_DV), lambda b, s: (b, 0, 0, 0)),
            *cast_out,
        ],
        out_shape=[
            jax.ShapeDtypeStruct((batch * seq, D_MODEL), F32),
            jax.ShapeDtypeStruct((batch, B_HEADS, B_DK, B_DV), F32),
            *cast_shapes,
        ],
        scratch_shapes=_mixproj_scratch(group, n_groups)
        + [pltpu.VMEM((B_HEADS, B_DK, B_DV), F32)],
        compiler_params=pltpu.CompilerParams(
            dimension_semantics=("arbitrary", "arbitrary"), vmem_limit_bytes=VMEM_LIMIT),
        name="mixproj_prompt",
    )(x1, g2, wp, wpt, wout, *consts, *[j["w"] for j in cast_jobs])


def _mixproj_sample_call(x1, s0, g2, wp, wpt, wout, consts, n_seq, seq, tile, group):
    seq_per_step = tile // seq
    n_groups = tile // group
    row = lambda i: (i, 0)
    state = lambda i: (i, 0, 0, 0)
    return pl.pallas_call(
        functools.partial(_mixproj_sample_kernel, chunk=seq, group=group, n_groups=n_groups),
        grid=(n_seq // seq_per_step,),
        in_specs=[
            pl.BlockSpec((tile, D_MODEL), row),
            pl.BlockSpec((seq_per_step, B_HEADS, B_DK, B_DV), state),
        ] + _proj_specs() + _mix_const_specs(),
        out_specs=[
            pl.BlockSpec((tile, D_MODEL), row),
            pl.BlockSpec((tile, A_WIDTH), row),
            pl.BlockSpec((seq_per_step, B_HEADS, B_DK, B_DV), state),
        ],
        out_shape=[
            jax.ShapeDtypeStruct((n_seq * seq, D_MODEL), F32),
            jax.ShapeDtypeStruct((n_seq * seq, A_WIDTH), F32),
            jax.ShapeDtypeStruct((n_seq, B_HEADS, B_DK, B_DV), F32),
        ],
        scratch_shapes=_mixproj_scratch(group, n_groups),
        compiler_params=pltpu.CompilerParams(
            dimension_semantics=("arbitrary",), vmem_limit_bytes=VMEM_LIMIT),
        name="mixproj_sample",
    )(x1, s0, g2, wp, wpt, wout, *consts)


def _mix_consts(a_ws, a_bs, a_vnorm, a_onorm, b_wa2, b_ba, b_onorm, chunk, seq_len):
    n_rep = ROWS // seq_len
    tril = jnp.tril(jnp.ones((seq_len, seq_len), dtype=bool))
    w = jnp.where(tril[None], a_ws[:, :seq_len, :seq_len], 0.0)
    if n_rep > 1:
        w = jax.vmap(lambda wg: jnp.kron(jnp.eye(n_rep, dtype=wg.dtype), wg))(w)
    bias = jnp.tile(a_bs[:, :seq_len], (1, n_rep))
    bias = jnp.broadcast_to(bias[:, :, None], (A_GROUPS, ROWS, LANES))
    wa2 = jnp.zeros((LANES, B_HEADS * B_DK), F32).at[:B_LOWRANK].set(b_wa2)
    return (
        w.astype(BF),
        bias,
        a_vnorm.reshape(1, A_WIDTH),
        a_onorm.reshape(1, A_WIDTH),
        wa2.astype(BF),
        b_ba.reshape(1, B_HEADS * B_DK),
        b_onorm.reshape(1, B_HEADS * B_DV),
        jnp.asarray(_cum_matrices(chunk), dtype=BF),
    )


def kernel(x_prompt, x_sample, state_gla, ffn1_norm, ffn1_w_in, ffn1_w_out, mix_norm, w_in,
           a_ws, a_bs, a_vnorm, a_onorm, b_wa2, b_ba, b_onorm, w_out, ffn2_norm, ffn2_w_in,
           ffn2_w_out, final_norm):
    batch, seq, _ = x_prompt.shape
    n_dec, dec_seq, _ = x_sample.shape
    assert ffn1_norm.shape[0] == 1, "single-layer kernel"
    assert seq % A_CHUNK == 0 and ROWS % dec_seq == 0 and dec_seq <= B_CHUNK

    g1 = ffn1_norm.reshape(1, D_MODEL)
    wi1 = ffn1_w_in.reshape(D_MODEL, 2 * D_FF)
    wo1 = ffn1_w_out.reshape(D_FF, D_MODEL)
    g2 = mix_norm.reshape(1, D_MODEL)
    w_in_t = jnp.swapaxes(w_in, 1, 2).reshape(IN_COLS, D_MODEL)
    w_out2d = w_out.reshape(D_MODEL, D_MODEL)
    g3 = ffn2_norm.reshape(1, D_MODEL)
    wi2 = ffn2_w_in.reshape(D_MODEL, 2 * D_FF)
    wo2 = ffn2_w_out.reshape(D_FF, D_MODEL)
    gf = final_norm.reshape(1, D_MODEL)
    mix_args = (a_ws[0], a_bs[0], a_vnorm[0], a_onorm[0], b_wa2[0], b_ba[0], b_onorm[0])
    consts_p = _mix_consts(*mix_args, chunk=B_CHUNK, seq_len=A_CHUNK)
    consts_s = _mix_consts(*mix_args, chunk=dec_seq, seq_len=dec_seq)

    xp = x_prompt.reshape(batch * seq, D_MODEL)
    xs = x_sample.reshape(n_dec * dec_seq, D_MODEL)
    s0 = state_gla.reshape(state_gla.shape[1:])

    x1p, x1s, wp, wpt, wout = _ffn_call(
        xp, xs, g1, wi1, wo1, gf, False, "ffn1",
        cast_jobs=(_cast_transposed(w_in_t, 0, MXU_DIM, P_MAIN),
                   _cast_transposed(w_in_t, P_MAIN, P_COLS - P_MAIN, P_COLS - P_MAIN,
                                    valid_cols=IN_COLS - P_MAIN),
                   _cast_rows(w_out2d)))
    x2p, st_p, wi2_bf, wo2_bf = _mixproj_prompt_call(
        x1p, g2, wp, wpt, wout, consts_p, batch, seq, tile=1024, group=512,
        cast_jobs=(_cast_rows(wi2), _cast_rows(wo2)))
    x2s, va_s, st_s = _mixproj_sample_call(x1s, s0, g2, wp, wpt, wout, consts_s, n_dec, dec_seq,
                                           tile=256, group=128)
    out_p, out_s = _ffn_call(x2p, x2s, g3, wi2_bf, wo2_bf, gf, True, "ffn2")

    return (
        out_p.reshape(batch, seq, D_MODEL),
        out_s.reshape(n_dec, dec_seq, D_MODEL),
        st_p[None],
        st_s[None],
        va_s.reshape(1, n_dec, dec_seq, A_WIDTH),
    )
```

```python
import functools

import numpy as np
import jax
import jax.numpy as jnp
from jax import lax
from jax.experimental import pallas as pl
from jax.experimental.pallas import tpu as pltpu

D_MODEL = 1024
D_FF = 2816
A_WIDTH = 512
A_GROUPS = 4
A_CHUNK = 128
B_HEADS = 4
B_DK = 64
B_DV = 128
B_LOWRANK = 16
B_GATE_NORMALIZER = 16.0
B_CHUNK = 64
EPS = 1e-6

OFF_U, OFF_V, OFF_Q, OFF_K, OFF_VB, OFF_R, OFF_LR = 0, 512, 1024, 1280, 1536, 2048, 2560
IN_COLS = 2576
P_COLS = 2688

LANES = 128
MXU_DIM = 256
ROWS = 128
FF_CHUNK = MXU_DIM
N_FF_CHUNKS = D_FF // FF_CHUNK
VMEM_LIMIT = 56 * 1024 * 1024

FFN_TILE = 1024
FFN_HALF = FFN_TILE // 2
WEIGHT_SLOT_ROWS = 256
P_MAIN = (IN_COLS // LANES) * LANES
CAST_BLOCKS = 16
MIX_STAGGER = 10

BF = jnp.bfloat16
F32 = jnp.float32


def _dot(a, b):
    return jnp.dot(a, b, preferred_element_type=F32)


def _dot_nt(a, b):
    return lax.dot_general(a, b, (((1,), (1,)), ((), ())), preferred_element_type=F32)


def _rmsnorm(x, g):
    return (x * lax.rsqrt(jnp.mean(x * x, axis=-1, keepdims=True) + EPS)) * g


def _gelu_tanh(x):
    c = np.float32(np.sqrt(2.0 / np.pi))
    ca = np.float32(np.sqrt(2.0 / np.pi) * 0.044715)
    t = jnp.tanh(x * (c + ca * (x * x)))
    hx = 0.5 * x
    return hx + hx * t


def _silu(x):
    return x * (1.0 / (1.0 + jnp.exp(-x)))


def _log_sigmoid(x):
    return jnp.minimum(x, 0.0) - jnp.log(1.0 + jnp.exp(-jnp.abs(x)))


def _staggered(gens, stagger):
    live = [True] * len(gens)
    t = 0
    while any(live):
        for i, gen in enumerate(gens):
            if live[i] and t >= i * stagger:
                try:
                    next(gen)
                except StopIteration:
                    live[i] = False
        t += 1
        yield


def _cast_rows(w):
    rows, cols = w.shape
    blk = (rows // CAST_BLOCKS, cols)
    return dict(w=w, src_block=blk, src_index=lambda r: (r, 0), dst_block=blk,
                dst_index=lambda r: (r, 0), out=jax.ShapeDtypeStruct(w.shape, BF), transposed=None)


def _cast_transposed(w_t, row0, n_rows, out_cols, valid_cols=None):
    cols = w_t.shape[1]
    first, last = row0 // n_rows, (row0 + out_cols) // n_rows - 1
    return dict(w=w_t, src_block=(n_rows, cols), dst_block=(cols, n_rows),
                src_index=lambda r: (jnp.minimum(first + r, last), 0),
                dst_index=lambda r: (0, jnp.minimum(r, last - first)),
                out=jax.ShapeDtypeStruct((cols, out_cols), BF),
                transposed=n_rows if valid_cols is None else valid_cols)


def _cast_specs(jobs, row_block):
    in_specs = [pl.BlockSpec(j["src_block"], lambda *ids, j=j: j["src_index"](row_block(*ids)))
                for j in jobs]
    out_specs = [pl.BlockSpec(j["dst_block"], lambda *ids, j=j: j["dst_index"](row_block(*ids)))
                 for j in jobs]
    return in_specs, out_specs, [j["out"] for j in jobs]


def _cast_blocks(src_refs, dst_refs, transposed):
    for src, dst, valid in zip(src_refs, dst_refs, transposed):
        x = src[...]
        if valid is not None:
            x = x.T
            if valid < x.shape[1]:
                lane = lax.broadcasted_iota(jnp.int32, x.shape, 1)
                x = jnp.where(lane < valid, x, 0.0)
        dst[...] = x.astype(BF)


def _weight_chunks(wi_hbm, wo_hbm, wi_ref, wo_ref):
    chunks = []
    for r0 in range(0, D_MODEL, WEIGHT_SLOT_ROWS):
        for c0 in range(0, 2 * D_FF, D_MODEL):
            w = min(D_MODEL, 2 * D_FF - c0)
            view = (pl.ds(r0, WEIGHT_SLOT_ROWS), pl.ds(c0, w))
            chunks.append((wi_hbm.at[view], wi_ref.at[view], w))
    for r0 in range(0, D_FF, WEIGHT_SLOT_ROWS):
        view = (pl.ds(r0, WEIGHT_SLOT_ROWS), pl.ds(0, D_MODEL))
        chunks.append((wo_hbm.at[view], wo_ref.at[view], D_MODEL))
    return chunks


def _load_weights_bf16(wi_hbm, wo_hbm, wi_ref, wo_ref, stage_ref, sem):
    chunks = _weight_chunks(wi_hbm, wo_hbm, wi_ref, wo_ref)
    n_slots = stage_ref.shape[0] // WEIGHT_SLOT_ROWS

    def slot(k):
        return stage_ref.at[pl.ds((k % n_slots) * WEIGHT_SLOT_ROWS, WEIGHT_SLOT_ROWS),
                            pl.ds(0, chunks[k][2])]

    def copy(k):
        return pltpu.make_async_copy(chunks[k][0], slot(k), sem.at[k % n_slots])

    def start(k):
        copy(k).start(priority=k % 2)

    for k in range(n_slots):
        start(k)
    for k in range(len(chunks)):
        copy(k).wait()
        chunks[k][1][...] = slot(k)[...].astype(BF)
        if k + n_slots < len(chunks):
            start(k + n_slots)


def _ffn_rows(load_x, g_ref, wi_ref, wo_ref, acc_ref, rows, finish):
    h = _rmsnorm(load_x(rows), g_ref[...]).astype(BF)
    yield
    for j in range(N_FF_CHUNKS):
        cs = slice(j * FF_CHUNK, (j + 1) * FF_CHUNK)
        gate = _dot(h, wi_ref[:, cs])
        up = _dot(h, wi_ref[:, D_FF + j * FF_CHUNK:D_FF + (j + 1) * FF_CHUNK])
        act = (_silu(gate) * up).astype(BF)
        part = _dot(act, wo_ref[cs, :])
        if j == 0:
            acc_ref[rows, :] = part
        else:
            acc_ref[rows, :] += part
        yield
    finish(load_x(rows), acc_ref[rows, :])


def _ffn_kernel(*refs, final_norm, stream_weights, cast_modes):
    n_cast = len(cast_modes)
    xa_ref, xb_ref, g_ref, wi_in, wo_in, gf_ref = refs[:6]
    cast_src = refs[6:6 + n_cast]
    oa_ref, ob_ref = refs[6 + n_cast:8 + n_cast]
    cast_dst = refs[8 + n_cast:8 + 2 * n_cast]
    scratch = refs[8 + 2 * n_cast:]
    first = pl.program_id(0) == 0

    if stream_weights:
        wi_ref, wo_ref, acc_ref, sem = scratch

        @pl.when(first)
        def _():
            _load_weights_bf16(wi_in, wo_in, wi_ref, wo_ref, acc_ref, sem)
    else:
        (acc_ref,) = scratch
        wi_ref, wo_ref = wi_in, wo_in

    def load_x(rows):
        return jnp.where(first, xb_ref[rows, :], xa_ref[rows, :])

    def half(i):
        rows = slice(i * FFN_HALF, (i + 1) * FFN_HALF)

        def finish(x, acc):
            y = x + 0.5 * acc
            oa_ref[rows, :] = _rmsnorm(y, gf_ref[...]) if final_norm else y

        return _ffn_rows(load_x, g_ref, wi_ref, wo_ref, acc_ref, rows, finish)

    for _ in _staggered([half(i) for i in range(FFN_TILE // FFN_HALF)], 1):
        pass
    _cast_blocks(cast_src, cast_dst, cast_modes)

    @pl.when(first)
    def _():
        ob_ref[...] = oa_ref[...]


def _resident(shape):
    nd = len(shape)
    return pl.BlockSpec(shape, lambda *_: (0,) * nd, pipeline_mode=pl.Buffered(1))


def _ffn_call(xa, xb, g, wi, wo, gf, final_norm, name, cast_jobs=()):
    ta = xa.shape[0]
    n_a = ta // FFN_TILE
    assert ta % FFN_TILE == 0 and xb.shape[0] == FFN_TILE
    assert not cast_jobs or n_a == CAST_BLOCKS
    stream_weights = wi.dtype == F32
    tile_a = lambda i: (jnp.maximum(i - 1, 0), 0)
    cast_in, cast_out, cast_shapes = _cast_specs(cast_jobs, lambda i: jnp.maximum(i - 1, 0))
    if stream_weights:
        w_specs = [pl.BlockSpec(memory_space=pl.ANY)] * 2
        w_scratch = [pltpu.VMEM(wi.shape, BF), pltpu.VMEM(wo.shape, BF)]
        sems = [pltpu.SemaphoreType.DMA((FFN_TILE // WEIGHT_SLOT_ROWS,))]
    else:
        w_specs = [_resident(wi.shape), _resident(wo.shape)]
        w_scratch, sems = [], []
    return pl.pallas_call(
        functools.partial(_ffn_kernel, final_norm=final_norm, stream_weights=stream_weights,
                          cast_modes=tuple(j["transposed"] for j in cast_jobs)),
        grid=(1 + n_a,),
        in_specs=[
            pl.BlockSpec((FFN_TILE, D_MODEL), tile_a),
            _resident((FFN_TILE, D_MODEL)),
            _resident((1, D_MODEL)),
            *w_specs,
            _resident((1, D_MODEL)),
            *cast_in,
        ],
        out_specs=[
            pl.BlockSpec((FFN_TILE, D_MODEL), tile_a),
            pl.BlockSpec((FFN_TILE, D_MODEL), lambda i: (0, 0)),
            *cast_out,
        ],
        out_shape=[
            jax.ShapeDtypeStruct((ta, D_MODEL), F32),
            jax.ShapeDtypeStruct((FFN_TILE, D_MODEL), F32),
            *cast_shapes,
        ],
        scratch_shapes=w_scratch + [pltpu.VMEM((FFN_TILE, D_MODEL), F32)] + sems,
        compiler_params=pltpu.CompilerParams(
            dimension_semantics=("arbitrary",), vmem_limit_bytes=VMEM_LIMIT),
        name=name,
    )(xa, xb, g, wi, wo, gf, *[j["w"] for j in cast_jobs])


def _cum_matrices(chunk):
    r = np.arange(ROWS)[:, None]
    k = np.arange(ROWS)[None, :]
    same = (r // chunk) == (k // chunk)
    m_cum = same & (k <= r)
    m_ref = same & ((k % chunk) <= (chunk // 2))
    m_last = same
    m = np.concatenate([m_cum, m_ref, m_last], axis=0).astype(np.float32)
    return np.concatenate([m, m], axis=1)


def _mix_block(cols, chunk, consts, get_state, put_state, emit):
    (wmix_ref, bias_ref, vnorm_ref, anorm_ref, wa2_ref, ba_ref, onorm_ref, cum_ref) = consts
    n_chunks = ROWS // chunk
    shift = int(np.log2(chunk))

    z = _dot(cols(OFF_LR, LANES).astype(BF), wa2_ref[...]) + ba_ref[...]
    la = _log_sigmoid(z) * (1.0 / B_GATE_NORMALIZER)
    la_hi = la.astype(BF)
    la_lo = (la - la_hi.astype(F32)).astype(BF)
    sums = _dot(cum_ref[...], jnp.concatenate([la_hi, la_lo], axis=0))
    b = sums[0:ROWS]
    b_ref = sums[ROWS:2 * ROWS]
    b_last = sums[2 * ROWS:3 * ROWS]
    yield

    ya = []
    va = []
    ssq = None
    for g in range(A_GROUPS):
        gs = slice(g * LANES, (g + 1) * LANES)
        u = _gelu_tanh(cols(OFF_U + g * LANES, LANES))
        v = _rmsnorm(_gelu_tanh(cols(OFF_V + g * LANES, LANES)), vnorm_ref[:, gs])
        yield
        zg = _dot(wmix_ref[g], v.astype(BF)) + bias_ref[g]
        yg = u * zg
        s = jnp.sum(yg * yg, axis=-1, keepdims=True)
        ssq = s if ssq is None else ssq + s
        ya.append(yg)
        va.append(v)
        yield
    inv = lax.rsqrt(ssq * (1.0 / A_WIDTH) + EPS)
    ya = [(ya[g] * inv) * anorm_ref[:, g * LANES:(g + 1) * LANES] for g in range(A_GROUPS)]
    yield
    gate = []
    for h in range(B_HEADS):
        gate.append(_silu(cols(OFF_R + h * B_DV, B_DV)))
        if h % 2 == 1:
            yield

    ri = lax.broadcasted_iota(jnp.int32, (ROWS, ROWS), 0)
    ci = lax.broadcasted_iota(jnp.int32, (ROWS, ROWS), 1)
    rchunk = lax.shift_right_logical(ri, shift)
    cchunk = lax.shift_right_logical(ci, shift)
    causal = (rchunk == cchunk) & (ci <= ri)
    lane = lax.broadcasted_iota(jnp.int32, (ROWS, LANES), 1)
    head0 = lane < B_DK

    yb = []
    for pair in range(B_HEADS // 2):
        ls = slice(pair * LANES, (pair + 1) * LANES)
        q = cols(OFF_Q + pair * LANES, LANES) * np.float32(B_DK ** -0.5)
        k = cols(OFF_K + pair * LANES, LANES)
        vv_bf = cols(OFF_VB + pair * 2 * B_DV, 2 * B_DV).astype(BF)
        bp, brp, blp = b[:, ls], b_ref[:, ls], b_last[:, ls]
        qs = q * jnp.exp(bp - brp)
        ks = (k * jnp.exp(brp - bp)).astype(BF)
        kl_t = (k * jnp.exp(blp - bp)).T
        bl_t = blp.T
        qb = (q * jnp.exp(bp)).astype(BF)
        yield

        o_heads = []
        for hh in range(2):
            msk = head0 if hh == 0 else jnp.logical_not(head0)
            qh = jnp.where(msk, qs, 0.0).astype(BF)
            sc = jnp.where(causal, _dot_nt(qh, ks), 0.0).astype(BF)
            o_heads.append(_dot(sc, vv_bf[:, hh * B_DV:(hh + 1) * B_DV]))
            yield
        o = jnp.concatenate(o_heads, axis=1)

        o_inter = []
        for j in range(n_chunks):
            s_prev = get_state(pair, j)
            o_inter.append(_dot(qb[j * chunk:(j + 1) * chunk], _blockdiag_bf16(*s_prev)))
            in_chunk = cchunk == j
            upd = _dot(jnp.where(in_chunk, kl_t, 0.0).astype(BF), vv_bf)
            dec = jnp.exp(bl_t[:, j * chunk:j * chunk + 1])
            put_state(pair, j, tuple(
                dec[hh * B_DK:(hh + 1) * B_DK] * s_prev[hh]
                + upd[hh * B_DK:(hh + 1) * B_DK, hh * B_DV:(hh + 1) * B_DV] for hh in range(2)))
        o = o + jnp.concatenate(o_inter, axis=0)
        yield

        for hh in range(2):
            h_idx = 2 * pair + hh
            hs = slice(h_idx * B_DV, (h_idx + 1) * B_DV)
            oh = _rmsnorm(o[:, hh * B_DV:(hh + 1) * B_DV], onorm_ref[:, hs])
            yb.append(oh * gate[h_idx])
        yield

    emit(jnp.concatenate(ya + yb, axis=1).astype(BF), jnp.concatenate(va, axis=1))


def _blockdiag_bf16(s0, s1):
    s0 = s0.astype(BF)
    s1 = s1.astype(BF)
    zero = jnp.zeros_like(s0)
    return jnp.concatenate(
        [jnp.concatenate([s0, zero], axis=1), jnp.concatenate([zero, s1], axis=1)], axis=0)


def _in_proj(x1_ref, g2_ref, wp_refs, p_scr, g, rows):
    wp_ref, wpt_ref = wp_refs
    h = _rmsnorm(x1_ref[rows, :], g2_ref[...]).astype(BF)
    for c0 in range(0, P_MAIN, MXU_DIM):
        cs = slice(c0, c0 + MXU_DIM)
        p_scr[g, :, cs] = _dot(h, wp_ref[:, cs])
    p_scr[g, :, P_MAIN:P_COLS] = _dot(h, wpt_ref[...])


def _out_proj(x1_ref, y_scr, wout_ref, x2_ref, g, rows):
    y = y_scr[g]
    for c0 in range(0, D_MODEL, MXU_DIM):
        cs = slice(c0, c0 + MXU_DIM)
        x2_ref[rows, cs] = x1_ref[rows, cs] + _dot(y, wout_ref[:, cs])


def _mixproj_tile(x1_ref, g2_ref, wp_refs, wout_ref, x2_ref, p_scr, y_scr, group, n_groups,
                  make_block):
    rows = [slice(g * group, (g + 1) * group) for g in range(n_groups)]
    for g in range(n_groups):
        _in_proj(x1_ref, g2_ref, wp_refs, p_scr, g, rows[g])
    blocks = [make_block(g, i) for g in range(n_groups) for i in range(group // ROWS)]
    for _ in _staggered(blocks, MIX_STAGGER):
        pass
    for g in range(n_groups):
        _out_proj(x1_ref, y_scr, wout_ref, x2_ref, g, rows[g])


def _block_cols(p_scr, g, i):
    return lambda off, width: p_scr[g, i * ROWS:(i + 1) * ROWS, off:off + width]


def _mixproj_prompt_kernel(*refs, group, n_groups, cast_modes):
    n_cast = len(cast_modes)
    (x1_ref, g2_ref, wp_ref, wpt_ref, wout_ref, wmix_ref, bias_ref, vnorm_ref, anorm_ref, wa2_ref,
     ba_ref, onorm_ref, cum_ref) = refs[:13]
    cast_src = refs[13:13 + n_cast]
    x2_ref, st_ref = refs[13 + n_cast:15 + n_cast]
    cast_dst = refs[15 + n_cast:15 + 2 * n_cast]
    p_scr, y_scr, s_scr = refs[15 + 2 * n_cast:]

    @pl.when(pl.program_id(1) == 0)
    def _():
        s_scr[...] = jnp.zeros_like(s_scr)

    consts = (wmix_ref, bias_ref, vnorm_ref, anorm_ref, wa2_ref, ba_ref, onorm_ref, cum_ref)
    cur = {pair: (s_scr[2 * pair], s_scr[2 * pair + 1]) for pair in range(B_HEADS // 2)}

    def get_state(pair, j):
        return cur[pair]

    def put_state(pair, j, s):
        cur[pair] = s

    def make_block(g, i):
        def emit(y, va):
            y_scr[g, i * ROWS:(i + 1) * ROWS, :] = y
        return _mix_block(_block_cols(p_scr, g, i), B_CHUNK, consts, get_state, put_state, emit)

    _mixproj_tile(x1_ref, g2_ref, (wp_ref, wpt_ref), wout_ref, x2_ref, p_scr, y_scr, group,
                  n_groups, make_block)
    _cast_blocks(cast_src, cast_dst, cast_modes)

    for pair in range(B_HEADS // 2):
        for hh in range(2):
            s_scr[2 * pair + hh] = cur[pair][hh]
            st_ref[0, 2 * pair + hh] = cur[pair][hh]


def _mixproj_sample_kernel(x1_ref, s0_ref, g2_ref, wp_ref, wpt_ref, wout_ref, wmix_ref, bias_ref,
                           vnorm_ref, anorm_ref, wa2_ref, ba_ref, onorm_ref, cum_ref, x2_ref,
                           va_ref, st_ref, p_scr, y_scr, *, chunk, group, n_groups):
    consts = (wmix_ref, bias_ref, vnorm_ref, anorm_ref, wa2_ref, ba_ref, onorm_ref, cum_ref)
    n_blk = group // ROWS
    per_blk = ROWS // chunk

    def make_block(g, i):
        blk = g * n_blk + i

        def get_state(pair, j):
            n = blk * per_blk + j
            return s0_ref[n, 2 * pair], s0_ref[n, 2 * pair + 1]

        def put_state(pair, j, s):
            n = blk * per_blk + j
            st_ref[n, 2 * pair] = s[0]
            st_ref[n, 2 * pair + 1] = s[1]

        def emit(y, va):
            y_scr[g, i * ROWS:(i + 1) * ROWS, :] = y
            va_ref[blk * ROWS:(blk + 1) * ROWS, :] = va

        return _mix_block(_block_cols(p_scr, g, i), chunk, consts, get_state, put_state, emit)

    _mixproj_tile(x1_ref, g2_ref, (wp_ref, wpt_ref), wout_ref, x2_ref, p_scr, y_scr, group,
                  n_groups, make_block)


def _mix_const_specs():
    return [
        _resident((A_GROUPS, ROWS, ROWS)),
        _resident((A_GROUPS, ROWS, LANES)),
        _resident((1, A_WIDTH)),
        _resident((1, A_WIDTH)),
        _resident((LANES, B_HEADS * B_DK)),
        _resident((1, B_HEADS * B_DK)),
        _resident((1, B_HEADS * B_DV)),
        _resident((3 * ROWS, 2 * ROWS)),
    ]


def _proj_specs():
    return [
        _resident((1, D_MODEL)),
        _resident((D_MODEL, P_MAIN)),
        _resident((D_MODEL, P_COLS - P_MAIN)),
        _resident((D_MODEL, D_MODEL)),
    ]


def _mixproj_scratch(group, n_groups):
    return [pltpu.VMEM((n_groups, group, P_COLS), F32),
            pltpu.VMEM((n_groups, group, D_MODEL), BF)]


def _mixproj_prompt_call(x1, g2, wp, wpt, wout, consts, batch, seq, tile, group, cast_jobs=()):
    n_tiles = seq // tile
    n_groups = tile // group
    assert not cast_jobs or batch * n_tiles == CAST_BLOCKS
    row = lambda b, s: (b * n_tiles + s, 0)
    cast_in, cast_out, cast_shapes = _cast_specs(cast_jobs, lambda b, s: b * n_tiles + s)
    return pl.pallas_call(
        functools.partial(_mixproj_prompt_kernel, group=group, n_groups=n_groups,
                          cast_modes=tuple(j["transposed"] for j in cast_jobs)),
        grid=(batch, n_tiles),
        in_specs=[pl.BlockSpec((tile, D_MODEL), row)] + _proj_specs() + _mix_const_specs()
        + cast_in,
        out_specs=[
            pl.BlockSpec((tile, D_MODEL), row),
            pl.BlockSpec((1, B_HEADS, B_DK, B_DV), lambda b, s: (b, 0, 0, 0)),
            *cast_out,
        ],
        out_shape=[
            jax.ShapeDtypeStruct((batch * seq, D_MODEL), F32),
            jax.ShapeDtypeStruct((batch, B_HEADS, B_DK, B_DV), F32),
            *cast_shapes,
        ],
        scratch_shapes=_mixproj_scratch(group, n_groups)
        + [pltpu.VMEM((B_HEADS, B_DK, B_DV), F32)],
        compiler_params=pltpu.CompilerParams(
            dimension_semantics=("arbitrary", "arbitrary"), vmem_limit_bytes=VMEM_LIMIT),
        name="mixproj_prompt",
    )(x1, g2, wp, wpt, wout, *consts, *[j["w"] for j in cast_jobs])


def _mixproj_sample_call(x1, s0, g2, wp, wpt, wout, consts, n_seq, seq, tile, group):
    seq_per_step = tile // seq
    n_groups = tile // group
    row = lambda i: (i, 0)
    state = lambda i: (i, 0, 0, 0)
    return pl.pallas_call(
        functools.partial(_mixproj_sample_kernel, chunk=seq, group=group, n_groups=n_groups),
        grid=(n_seq // seq_per_step,),
        in_specs=[
            pl.BlockSpec((tile, D_MODEL), row),
            pl.BlockSpec((seq_per_step, B_HEADS, B_DK, B_DV), state),
        ] + _proj_specs() + _mix_const_specs(),
        out_specs=[
            pl.BlockSpec((tile, D_MODEL), row),
            pl.BlockSpec((tile, A_WIDTH), row),
            pl.BlockSpec((seq_per_step, B_HEADS, B_DK, B_DV), state),
        ],
        out_shape=[
            jax.ShapeDtypeStruct((n_seq * seq, D_MODEL), F32),
            jax.ShapeDtypeStruct((n_seq * seq, A_WIDTH), F32),
            jax.ShapeDtypeStruct((n_seq, B_HEADS, B_DK, B_DV), F32),
        ],
        scratch_shapes=_mixproj_scratch(group, n_groups),
        compiler_params=pltpu.CompilerParams(
            dimension_semantics=("arbitrary",), vmem_limit_bytes=VMEM_LIMIT),
        name="mixproj_sample",
    )(x1, s0, g2, wp, wpt, wout, *consts)


def _mix_consts(a_ws, a_bs, a_vnorm, a_onorm, b_wa2, b_ba, b_onorm, chunk, seq_len):
    n_rep = ROWS // seq_len
    tril = jnp.tril(jnp.ones((seq_len, seq_len), dtype=bool))
    w = jnp.where(tril[None], a_ws[:, :seq_len, :seq_len], 0.0)
    if n_rep > 1:
        w = jax.vmap(lambda wg: jnp.kron(jnp.eye(n_rep, dtype=wg.dtype), wg))(w)
    bias = jnp.tile(a_bs[:, :seq_len], (1, n_rep))
    bias = jnp.broadcast_to(bias[:, :, None], (A_GROUPS, ROWS, LANES))
    wa2 = jnp.zeros((LANES, B_HEADS * B_DK), F32).at[:B_LOWRANK].set(b_wa2)
    return (
        w.astype(BF),
        bias,
        a_vnorm.reshape(1, A_WIDTH),
        a_onorm.reshape(1, A_WIDTH),
        wa2.astype(BF),
        b_ba.reshape(1, B_HEADS * B_DK),
        b_onorm.reshape(1, B_HEADS * B_DV),
        jnp.asarray(_cum_matrices(chunk), dtype=BF),
    )


def kernel(x_prompt, x_sample, state_gla, ffn1_norm, ffn1_w_in, ffn1_w_out, mix_norm, w_in,
           a_ws, a_bs, a_vnorm, a_onorm, b_wa2, b_ba, b_onorm, w_out, ffn2_norm, ffn2_w_in,
           ffn2_w_out, final_norm):
    batch, seq, _ = x_prompt.shape
    n_dec, dec_seq, _ = x_sample.shape
    assert ffn1_norm.shape[0] == 1, "single-layer kernel"
    assert seq % A_CHUNK == 0 and ROWS % dec_seq == 0 and dec_seq <= B_CHUNK

    g1 = ffn1_norm.reshape(1, D_MODEL)
    wi1 = ffn1_w_in.reshape(D_MODEL, 2 * D_FF)
    wo1 = ffn1_w_out.reshape(D_FF, D_MODEL)
    g2 = mix_norm.reshape(1, D_MODEL)
    w_in_t = jnp.swapaxes(w_in, 1, 2).reshape(IN_COLS, D_MODEL)
    w_out2d = w_out.reshape(D_MODEL, D_MODEL)
    g3 = ffn2_norm.reshape(1, D_MODEL)
    wi2 = ffn2_w_in.reshape(D_MODEL, 2 * D_FF)
    wo2 = ffn2_w_out.reshape(D_FF, D_MODEL)
    gf = final_norm.reshape(1, D_MODEL)
    mix_args = (a_ws[0], a_bs[0], a_vnorm[0], a_onorm[0], b_wa2[0], b_ba[0], b_onorm[0])
    consts_p = _mix_consts(*mix_args, chunk=B_CHUNK, seq_len=A_CHUNK)
    consts_s = _mix_consts(*mix_args, chunk=dec_seq, seq_len=dec_seq)

    xp = x_prompt.reshape(batch * seq, D_MODEL)
    xs = x_sample.reshape(n_dec * dec_seq, D_MODEL)
    s0 = state_gla.reshape(state_gla.shape[1:])

    x1p, x1s, wp, wpt, wout = _ffn_call(
        xp, xs, g1, wi1, wo1, gf, False, "ffn1",
        cast_jobs=(_cast_transposed(w_in_t, 0, MXU_DIM, P_MAIN),
                   _cast_transposed(w_in_t, P_MAIN, P_COLS - P_MAIN, P_COLS - P_MAIN,
                                    valid_cols=IN_COLS - P_MAIN),
                   _cast_rows(w_out2d)))
    x2p, st_p, wi2_bf, wo2_bf = _mixproj_prompt_call(
        x1p, g2, wp, wpt, wout, consts_p, batch, seq, tile=1024, group=512,
        cast_jobs=(_cast_rows(wi2), _cast_rows(wo2)))
    x2s, va_s, st_s = _mixproj_sample_call(x1s, s0, g2, wp, wpt, wout, consts_s, n_dec, dec_seq,
                                           tile=256, group=128)
    out_p, out_s = _ffn_call(x2p, x2s, g3, wi2_bf, wo2_bf, gf, True, "ffn2")

    return (
        out_p.reshape(batch, seq, D_MODEL),
        out_s.reshape(n_dec, dec_seq, D_MODEL),
        st_p[None],
        st_s[None],
        va_s.reshape(1, n_dec, dec_seq, A_WIDTH),
    )
```

```python
import functools

import numpy as np
import jax
import jax.numpy as jnp
from jax import lax
from jax.experimental import pallas as pl
from jax.experimental.pallas import tpu as pltpu

D_MODEL = 1024
D_FF = 2816
A_WIDTH = 512
A_GROUPS = 4
A_CHUNK = 128
B_HEADS = 4
B_DK = 64
B_DV = 128
B_LOWRANK = 16
B_GATE_NORMALIZER = 16.0
B_CHUNK = 64
EPS = 1e-6

OFF_U, OFF_V, OFF_Q, OFF_K, OFF_VB, OFF_R, OFF_LR = 0, 512, 1024, 1280, 1536, 2048, 2560
IN_COLS = 2576

LANES = 128
P_COLS = -(-IN_COLS // LANES) * LANES
MXU_DIM = 256
ROWS = 128
FF_CHUNK = MXU_DIM
N_FF_CHUNKS = D_FF // FF_CHUNK
VMEM_LIMIT = 56 * 1024 * 1024

FFN_TILE = 1024
FFN_HALF = FFN_TILE // 2
WEIGHT_SLOT_ROWS = 128
P_MAIN = (IN_COLS // LANES) * LANES
CAST_BLOCKS = 16
MIX_STAGGER = 10

BF = jnp.bfloat16
F32 = jnp.float32


def _dot(a, b):
    return jnp.dot(a, b, preferred_element_type=F32)


def _dot_nt(a, b):
    return lax.dot_general(a, b, (((1,), (1,)), ((), ())), preferred_element_type=F32)


def _rmsnorm(x, g):
    return (x * lax.rsqrt(jnp.mean(x * x, axis=-1, keepdims=True) + EPS)) * g


def _gelu_tanh(x):
    c = np.float32(np.sqrt(2.0 / np.pi))
    ca = np.float32(np.sqrt(2.0 / np.pi) * 0.044715)
    t = jnp.tanh(x * (c + ca * (x * x)))
    hx = 0.5 * x
    return hx + hx * t


def _silu(x):
    return x * (1.0 / (1.0 + jnp.exp(-x)))


def _log_sigmoid(x):
    return jnp.minimum(x, 0.0) - jnp.log(1.0 + jnp.exp(-jnp.abs(x)))


def _staggered(gens, stagger):
    live = [True] * len(gens)
    t = 0
    while any(live):
        for i, gen in enumerate(gens):
            if live[i] and t >= i * stagger:
                try:
                    next(gen)
                except StopIteration:
                    live[i] = False
        t += 1
        yield


def _cast_rows(w):
    rows, cols = w.shape
    blk = (rows // CAST_BLOCKS, cols)
    return dict(w=w, src_block=blk, src_index=lambda r: (r, 0), dst_block=blk,
                dst_index=lambda r: (r, 0), out=jax.ShapeDtypeStruct(w.shape, BF), transposed=None)


def _cast_transposed(w_t, row0, n_rows, out_cols, valid_cols=None):
    cols = w_t.shape[1]
    first, last = row0 // n_rows, (row0 + out_cols) // n_rows - 1
    return dict(w=w_t, src_block=(n_rows, cols), dst_block=(cols, n_rows),
                src_index=lambda r: (jnp.minimum(first + r, last), 0),
                dst_index=lambda r: (0, jnp.minimum(r, last - first)),
                out=jax.ShapeDtypeStruct((cols, out_cols), BF),
                transposed=n_rows if valid_cols is None else valid_cols)


def _cast_specs(jobs, row_block):
    in_specs = [pl.BlockSpec(j["src_block"], lambda *ids, j=j: j["src_index"](row_block(*ids)))
                for j in jobs]
    out_specs = [pl.BlockSpec(j["dst_block"], lambda *ids, j=j: j["dst_index"](row_block(*ids)))
                 for j in jobs]
    return in_specs, out_specs, [j["out"] for j in jobs]


def _cast_blocks(src_refs, dst_refs, transposed):
    for src, dst, valid in zip(src_refs, dst_refs, transposed):
        x = src[...]
        if valid is not None:
            x = x.T
            if valid < x.shape[1]:
                lane = lax.broadcasted_iota(jnp.int32, x.shape, 1)
                x = jnp.where(lane < valid, x, 0.0)
        dst[...] = x.astype(BF)


def _weight_chunks(wi_hbm, wo_hbm, wi_ref, wo_ref):
    chunks = []
    for r0 in range(0, D_MODEL, WEIGHT_SLOT_ROWS):
        for c0 in range(0, 2 * D_FF, D_MODEL):
            w = min(D_MODEL, 2 * D_FF - c0)
            view = (pl.ds(r0, WEIGHT_SLOT_ROWS), pl.ds(c0, w))
            chunks.append((wi_hbm.at[view], wi_ref.at[view], w))
    for r0 in range(0, D_FF, WEIGHT_SLOT_ROWS):
        view = (pl.ds(r0, WEIGHT_SLOT_ROWS), pl.ds(0, D_MODEL))
        chunks.append((wo_hbm.at[view], wo_ref.at[view], D_MODEL))
    return chunks


def _load_weights_bf16(wi_hbm, wo_hbm, wi_ref, wo_ref, stage_ref, sem):
    chunks = _weight_chunks(wi_hbm, wo_hbm, wi_ref, wo_ref)
    n_slots = stage_ref.shape[0] // WEIGHT_SLOT_ROWS

    def slot(k):
        return stage_ref.at[pl.ds((k % n_slots) * WEIGHT_SLOT_ROWS, WEIGHT_SLOT_ROWS),
                            pl.ds(0, chunks[k][2])]

    def copy(k):
        return pltpu.make_async_copy(chunks[k][0], slot(k), sem.at[k % n_slots])

    for k in range(n_slots):
        copy(k).start()
    for k in range(len(chunks)):
        copy(k).wait()
        chunks[k][1][...] = slot(k)[...].astype(BF)
        if k + n_slots < len(chunks):
            copy(k + n_slots).start()


def _ffn_rows(load_x, g_ref, wi_ref, wo_ref, acc_ref, rows, finish):
    h = _rmsnorm(load_x(rows), g_ref[...]).astype(BF)
    yield
    for j in range(N_FF_CHUNKS):
        cs = slice(j * FF_CHUNK, (j + 1) * FF_CHUNK)
        gate = _dot(h, wi_ref[:, cs])
        up = _dot(h, wi_ref[:, D_FF + j * FF_CHUNK:D_FF + (j + 1) * FF_CHUNK])
        act = (_silu(gate) * up).astype(BF)
        part = _dot(act, wo_ref[cs, :])
        if j == 0:
            acc_ref[rows, :] = part
        else:
            acc_ref[rows, :] += part
        yield
    finish(load_x(rows), acc_ref[rows, :])


def _ffn_kernel(*refs, final_norm, stream_weights, cast_modes):
    n_cast = len(cast_modes)
    xa_ref, xb_ref, g_ref, wi_in, wo_in, gf_ref = refs[:6]
    cast_src = refs[6:6 + n_cast]
    oa_ref, ob_ref = refs[6 + n_cast:8 + n_cast]
    cast_dst = refs[8 + n_cast:8 + 2 * n_cast]
    scratch = refs[8 + 2 * n_cast:]
    first = pl.program_id(0) == 0

    if stream_weights:
        wi_ref, wo_ref, acc_ref, sem = scratch

        @pl.when(first)
        def _():
            _load_weights_bf16(wi_in, wo_in, wi_ref, wo_ref, acc_ref, sem)
    else:
        (acc_ref,) = scratch
        wi_ref, wo_ref = wi_in, wo_in

    def load_x(rows):
        return jnp.where(first, xb_ref[rows, :], xa_ref[rows, :])

    def half(i):
        rows = slice(i * FFN_HALF, (i + 1) * FFN_HALF)

        def finish(x, acc):
            y = x + 0.5 * acc
            oa_ref[rows, :] = _rmsnorm(y, gf_ref[...]) if final_norm else y

        return _ffn_rows(load_x, g_ref, wi_ref, wo_ref, acc_ref, rows, finish)

    for _ in _staggered([half(i) for i in range(FFN_TILE // FFN_HALF)], 1):
        pass
    _cast_blocks(cast_src, cast_dst, cast_modes)

    @pl.when(first)
    def _():
        ob_ref[...] = oa_ref[...]


def _resident(shape):
    nd = len(shape)
    return pl.BlockSpec(shape, lambda *_: (0,) * nd, pipeline_mode=pl.Buffered(1))


def _ffn_call(xa, xb, g, wi, wo, gf, final_norm, name, cast_jobs=()):
    ta = xa.shape[0]
    n_a = ta // FFN_TILE
    assert ta % FFN_TILE == 0 and xb.shape[0] == FFN_TILE
    assert not cast_jobs or n_a == CAST_BLOCKS
    stream_weights = wi.dtype == F32
    tile_a = lambda i: (jnp.maximum(i - 1, 0), 0)
    cast_in, cast_out, cast_shapes = _cast_specs(cast_jobs, lambda i: jnp.maximum(i - 1, 0))
    if stream_weights:
        w_specs = [pl.BlockSpec(memory_space=pl.ANY)] * 2
        w_scratch = [pltpu.VMEM(wi.shape, BF), pltpu.VMEM(wo.shape, BF)]
        sems = [pltpu.SemaphoreType.DMA((FFN_TILE // WEIGHT_SLOT_ROWS,))]
    else:
        w_specs = [_resident(wi.shape), _resident(wo.shape)]
        w_scratch, sems = [], []
    return pl.pallas_call(
        functools.partial(_ffn_kernel, final_norm=final_norm, stream_weights=stream_weights,
                          cast_modes=tuple(j["transposed"] for j in cast_jobs)),
        grid=(1 + n_a,),
        in_specs=[
            pl.BlockSpec((FFN_TILE, D_MODEL), tile_a),
            _resident((FFN_TILE, D_MODEL)),
            _resident((1, D_MODEL)),
            *w_specs,
            _resident((1, D_MODEL)),
            *cast_in,
        ],
        out_specs=[
            pl.BlockSpec((FFN_TILE, D_MODEL), tile_a),
            pl.BlockSpec((FFN_TILE, D_MODEL), lambda i: (0, 0)),
            *cast_out,
        ],
        out_shape=[
            jax.ShapeDtypeStruct((ta, D_MODEL), F32),
            jax.ShapeDtypeStruct((FFN_TILE, D_MODEL), F32),
            *cast_shapes,
        ],
        scratch_shapes=w_scratch + [pltpu.VMEM((FFN_TILE, D_MODEL), F32)] + sems,
        compiler_params=pltpu.CompilerParams(
            dimension_semantics=("arbitrary",), vmem_limit_bytes=VMEM_LIMIT),
        name=name,
    )(xa, xb, g, wi, wo, gf, *[j["w"] for j in cast_jobs])


def _cum_matrices(chunk):
    r = np.arange(ROWS)[:, None]
    k = np.arange(ROWS)[None, :]
    same = (r // chunk) == (k // chunk)
    m_cum = same & (k <= r)
    m_ref = same & ((k % chunk) <= (chunk // 2))
    m_last = same
    m = np.concatenate([m_cum, m_ref, m_last], axis=0).astype(np.float32)
    return np.concatenate([m, m], axis=1)


def _mix_block(cols, chunk, consts, get_state, put_state, emit):
    (wmix_ref, bias_ref, vnorm_ref, anorm_ref, wa2_ref, ba_ref, onorm_ref, cum_ref) = consts
    n_chunks = ROWS // chunk
    shift = int(np.log2(chunk))

    z = _dot(cols(OFF_LR, LANES).astype(BF), wa2_ref[...]) + ba_ref[...]
    la = _log_sigmoid(z) * (1.0 / B_GATE_NORMALIZER)
    la_hi = la.astype(BF)
    la_lo = (la - la_hi.astype(F32)).astype(BF)
    sums = _dot(cum_ref[...], jnp.concatenate([la_hi, la_lo], axis=0))
    b = sums[0:ROWS]
    b_ref = sums[ROWS:2 * ROWS]
    b_last = sums[2 * ROWS:3 * ROWS]
    yield

    ya = []
    va = []
    ssq = None
    for g in range(A_GROUPS):
        gs = slice(g * LANES, (g + 1) * LANES)
        u = _gelu_tanh(cols(OFF_U + g * LANES, LANES))
        v = _rmsnorm(_gelu_tanh(cols(OFF_V + g * LANES, LANES)), vnorm_ref[:, gs])
        yield
        zg = _dot(wmix_ref[g], v.astype(BF)) + bias_ref[g]
        yg = u * zg
        s = jnp.sum(yg * yg, axis=-1, keepdims=True)
        ssq = s if ssq is None else ssq + s
        ya.append(yg)
        va.append(v)
        yield
    inv = lax.rsqrt(ssq * (1.0 / A_WIDTH) + EPS)
    ya = [(ya[g] * inv) * anorm_ref[:, g * LANES:(g + 1) * LANES] for g in range(A_GROUPS)]
    yield
    gate = []
    for h in range(B_HEADS):
        gate.append(_silu(cols(OFF_R + h * B_DV, B_DV)))
        if h % 2 == 1:
            yield

    ri = lax.broadcasted_iota(jnp.int32, (ROWS, ROWS), 0)
    ci = lax.broadcasted_iota(jnp.int32, (ROWS, ROWS), 1)
    rchunk = lax.shift_right_logical(ri, shift)
    cchunk = lax.shift_right_logical(ci, shift)
    causal = (rchunk == cchunk) & (ci <= ri)
    lane = lax.broadcasted_iota(jnp.int32, (ROWS, LANES), 1)
    head0 = lane < B_DK

    yb = []
    for pair in range(B_HEADS // 2):
        ls = slice(pair * LANES, (pair + 1) * LANES)
        q = cols(OFF_Q + pair * LANES, LANES) * np.float32(B_DK ** -0.5)
        k = cols(OFF_K + pair * LANES, LANES)
        vv_bf = cols(OFF_VB + pair * 2 * B_DV, 2 * B_DV).astype(BF)
        bp, brp, blp = b[:, ls], b_ref[:, ls], b_last[:, ls]
        qs = q * jnp.exp(bp - brp)
        ks = (k * jnp.exp(brp - bp)).astype(BF)
        kl_t = (k * jnp.exp(blp - bp)).T
        bl_t = blp.T
        qb = (q * jnp.exp(bp)).astype(BF)
        yield

        o_heads = []
        for hh in range(2):
            msk = head0 if hh == 0 else jnp.logical_not(head0)
            qh = jnp.where(msk, qs, 0.0).astype(BF)
            sc = jnp.where(causal, _dot_nt(qh, ks), 0.0).astype(BF)
            o_heads.append(_dot(sc, vv_bf[:, hh * B_DV:(hh + 1) * B_DV]))
            yield
        o = jnp.concatenate(o_heads, axis=1)

        o_inter = []
        for j in range(n_chunks):
            s_prev = get_state(pair, j)
            o_inter.append(_dot(qb[j * chunk:(j + 1) * chunk], _blockdiag_bf16(*s_prev)))
            in_chunk = cchunk == j
            upd = _dot(jnp.where(in_chunk, kl_t, 0.0).astype(BF), vv_bf)
            dec = jnp.exp(bl_t[:, j * chunk:j * chunk + 1])
            put_state(pair, j, tuple(
                dec[hh * B_DK:(hh + 1) * B_DK] * s_prev[hh]
                + upd[hh * B_DK:(hh + 1) * B_DK, hh * B_DV:(hh + 1) * B_DV] for hh in range(2)))
        o = o + jnp.concatenate(o_inter, axis=0)
        yield

        for hh in range(2):
            h_idx = 2 * pair + hh
            hs = slice(h_idx * B_DV, (h_idx + 1) * B_DV)
            oh = _rmsnorm(o[:, hh * B_DV:(hh + 1) * B_DV], onorm_ref[:, hs])
            yb.append(oh * gate[h_idx])
        yield

    emit(jnp.concatenate(ya + yb, axis=1).astype(BF), jnp.concatenate(va, axis=1))


def _blockdiag_bf16(s0, s1):
    s0 = s0.astype(BF)
    s1 = s1.astype(BF)
    zero = jnp.zeros_like(s0)
    return jnp.concatenate(
        [jnp.concatenate([s0, zero], axis=1), jnp.concatenate([zero, s1], axis=1)], axis=0)


def _in_proj(x1_ref, g2_ref, wp_refs, p_scr, g, rows):
    wp_ref, wpt_ref = wp_refs
    h = _rmsnorm(x1_ref[rows, :], g2_ref[...]).astype(BF)
    for c0 in range(0, P_MAIN, MXU_DIM):
        cs = slice(c0, c0 + MXU_DIM)
        p_scr[g, :, cs] = _dot(h, wp_ref[:, cs])
    p_scr[g, :, P_MAIN:P_COLS] = _dot(h, wpt_ref[...])


def _out_proj(x1_ref, y_scr, wout_ref, x2_ref, g, rows):
    y = y_scr[g]
    for c0 in range(0, D_MODEL, MXU_DIM):
        cs = slice(c0, c0 + MXU_DIM)
        x2_ref[rows, cs] = x1_ref[rows, cs] + _dot(y, wout_ref[:, cs])


def _mixproj_tile(x1_ref, g2_ref, wp_refs, wout_ref, x2_ref, p_scr, y_scr, group, n_groups,
                  make_block):
    rows = [slice(g * group, (g + 1) * group) for g in range(n_groups)]
    for g in range(n_groups):
        _in_proj(x1_ref, g2_ref, wp_refs, p_scr, g, rows[g])
    blocks = [make_block(g, i) for g in range(n_groups) for i in range(group // ROWS)]
    for _ in _staggered(blocks, MIX_STAGGER):
        pass
    for g in range(n_groups):
        _out_proj(x1_ref, y_scr, wout_ref, x2_ref, g, rows[g])


def _block_cols(p_scr, g, i):
    return lambda off, width: p_scr[g, i * ROWS:(i + 1) * ROWS, off:off + width]


def _mixproj_prompt_kernel(*refs, group, n_groups, cast_modes):
    n_cast = len(cast_modes)
    (x1_ref, g2_ref, wp_ref, wpt_ref, wout_ref, wmix_ref, bias_ref, vnorm_ref, anorm_ref, wa2_ref,
     ba_ref, onorm_ref, cum_ref) = refs[:13]
    cast_src = refs[13:13 + n_cast]
    x2_ref, st_ref = refs[13 + n_cast:15 + n_cast]
    cast_dst = refs[15 + n_cast:15 + 2 * n_cast]
    p_scr, y_scr, s_scr = refs[15 + 2 * n_cast:]

    @pl.when(pl.program_id(1) == 0)
    def _():
        s_scr[...] = jnp.zeros_like(s_scr)

    consts = (wmix_ref, bias_ref, vnorm_ref, anorm_ref, wa2_ref, ba_ref, onorm_ref, cum_ref)
    cur = {pair: (s_scr[2 * pair], s_scr[2 * pair + 1]) for pair in range(B_HEADS // 2)}

    def get_state(pair, j):
        return cur[pair]

    def put_state(pair, j, s):
        cur[pair] = s

    def make_block(g, i):
        def emit(y, va):
            y_scr[g, i * ROWS:(i + 1) * ROWS, :] = y
        return _mix_block(_block_cols(p_scr, g, i), B_CHUNK, consts, get_state, put_state, emit)

    _mixproj_tile(x1_ref, g2_ref, (wp_ref, wpt_ref), wout_ref, x2_ref, p_scr, y_scr, group,
                  n_groups, make_block)
    _cast_blocks(cast_src, cast_dst, cast_modes)

    for pair in range(B_HEADS // 2):
        for hh in range(2):
            s_scr[2 * pair + hh] = cur[pair][hh]
            st_ref[0, 2 * pair + hh] = cur[pair][hh]


def _mixproj_sample_kernel(x1_ref, s0_ref, g2_ref, wp_ref, wpt_ref, wout_ref, wmix_ref, bias_ref,
                           vnorm_ref, anorm_ref, wa2_ref, ba_ref, onorm_ref, cum_ref, x2_ref,
                           va_ref, st_ref, p_scr, y_scr, *, chunk, group, n_groups):
    consts = (wmix_ref, bias_ref, vnorm_ref, anorm_ref, wa2_ref, ba_ref, onorm_ref, cum_ref)
    n_blk = group // ROWS
    per_blk = ROWS // chunk

    def make_block(g, i):
        blk = g * n_blk + i

        def get_state(pair, j):
            n = blk * per_blk + j
            return s0_ref[n, 2 * pair], s0_ref[n, 2 * pair + 1]

        def put_state(pair, j, s):
            n = blk * per_blk + j
            st_ref[n, 2 * pair] = s[0]
            st_ref[n, 2 * pair + 1] = s[1]

        def emit(y, va):
            y_scr[g, i * ROWS:(i + 1) * ROWS, :] = y
            va_ref[blk * ROWS:(blk + 1) * ROWS, :] = va

        return _mix_block(_block_cols(p_scr, g, i), chunk, consts, get_state, put_state, emit)

    _mixproj_tile(x1_ref, g2_ref, (wp_ref, wpt_ref), wout_ref, x2_ref, p_scr, y_scr, group,
                  n_groups, make_block)


def _mix_const_specs():
    return [
        _resident((A_GROUPS, ROWS, ROWS)),
        _resident((A_GROUPS, ROWS, LANES)),
        _resident((1, A_WIDTH)),
        _resident((1, A_WIDTH)),
        _resident((LANES, B_HEADS * B_DK)),
        _resident((1, B_HEADS * B_DK)),
        _resident((1, B_HEADS * B_DV)),
        _resident((3 * ROWS, 2 * ROWS)),
    ]


def _proj_specs():
    return [
        _resident((1, D_MODEL)),
        _resident((D_MODEL, P_MAIN)),
        _resident((D_MODEL, P_COLS - P_MAIN)),
        _resident((D_MODEL, D_MODEL)),
    ]


def _mixproj_scratch(group, n_groups):
    return [pltpu.VMEM((n_groups, group, P_COLS), F32),
            pltpu.VMEM((n_groups, group, D_MODEL), BF)]


def _mixproj_prompt_call(x1, g2, wp, wpt, wout, consts, batch, seq, tile, group, cast_jobs=()):
    n_tiles = seq // tile
    n_groups = tile // group
    assert not cast_jobs or batch * n_tiles == CAST_BLOCKS
    row = lambda b, s: (b * n_tiles + s, 0)
    cast_in, cast_out, cast_shapes = _cast_specs(cast_jobs, lambda b, s: b * n_tiles + s)
    return pl.pallas_call(
        functools.partial(_mixproj_prompt_kernel, group=group, n_groups=n_groups,
                          cast_modes=tuple(j["transposed"] for j in cast_jobs)),
        grid=(batch, n_tiles),
        in_specs=[pl.BlockSpec((tile, D_MODEL), row)] + _proj_specs() + _mix_const_specs()
        + cast_in,
        out_specs=[
            pl.BlockSpec((tile, D_MODEL), row),
            pl.BlockSpec((1, B_HEADS, B_DK, B_DV), lambda b, s: (b, 0, 0, 0)),
            *cast_out,
        ],
        out_shape=[
            jax.ShapeDtypeStruct((batch * seq, D_MODEL), F32),
            jax.ShapeDtypeStruct((batch, B_HEADS, B_DK, B_DV), F32),
            *cast_shapes,
        ],
        scratch_shapes=_mixproj_scratch(group, n_groups)
        + [pltpu.VMEM((B_HEADS, B_DK, B_DV), F32)],
        compiler_params=pltpu.CompilerParams(
            dimension_semantics=("arbitrary", "arbitrary"), vmem_limit_bytes=VMEM_LIMIT),
        name="mixproj_prompt",
    )(x1, g2, wp, wpt, wout, *consts, *[j["w"] for j in cast_jobs])


def _mixproj_sample_call(x1, s0, g2, wp, wpt, wout, consts, n_seq, seq, tile, group):
    seq_per_step = tile // seq
    n_groups = tile // group
    row = lambda i: (i, 0)
    state = lambda i: (i, 0, 0, 0)
    return pl.pallas_call(
        functools.partial(_mixproj_sample_kernel, chunk=seq, group=group, n_groups=n_groups),
        grid=(n_seq // seq_per_step,),
        in_specs=[
            pl.BlockSpec((tile, D_MODEL), row),
            pl.BlockSpec((seq_per_step, B_HEADS, B_DK, B_DV), state),
        ] + _proj_specs() + _mix_const_specs(),
        out_specs=[
            pl.BlockSpec((tile, D_MODEL), row),
            pl.BlockSpec((tile, A_WIDTH), row),
            pl.BlockSpec((seq_per_step, B_HEADS, B_DK, B_DV), state),
        ],
        out_shape=[
            jax.ShapeDtypeStruct((n_seq * seq, D_MODEL), F32),
            jax.ShapeDtypeStruct((n_seq * seq, A_WIDTH), F32),
            jax.ShapeDtypeStruct((n_seq, B_HEADS, B_DK, B_DV), F32),
        ],
        scratch_shapes=_mixproj_scratch(group, n_groups),
        compiler_params=pltpu.CompilerParams(
            dimension_semantics=("arbitrary",), vmem_limit_bytes=VMEM_LIMIT),
        name="mixproj_sample",
    )(x1, s0, g2, wp, wpt, wout, *consts)


def _mix_consts(a_ws, a_bs, a_vnorm, a_onorm, b_wa2, b_ba, b_onorm, chunk, seq_len):
    n_rep = ROWS // seq_len
    tril = jnp.tril(jnp.ones((seq_len, seq_len), dtype=bool))
    w = jnp.where(tril[None], a_ws[:, :seq_len, :seq_len], 0.0)
    if n_rep > 1:
        w = jax.vmap(lambda wg: jnp.kron(jnp.eye(n_rep, dtype=wg.dtype), wg))(w)
    bias = jnp.tile(a_bs[:, :seq_len], (1, n_rep))
    bias = jnp.broadcast_to(bias[:, :, None], (A_GROUPS, ROWS, LANES))
    wa2 = jnp.zeros((LANES, B_HEADS * B_DK), F32).at[:B_LOWRANK].set(b_wa2)
    return (
        w.astype(BF),
        bias,
        a_vnorm.reshape(1, A_WIDTH),
        a_onorm.reshape(1, A_WIDTH),
        wa2.astype(BF),
        b_ba.reshape(1, B_HEADS * B_DK),
        b_onorm.reshape(1, B_HEADS * B_DV),
        jnp.asarray(_cum_matrices(chunk), dtype=BF),
    )


def kernel(x_prompt, x_sample, state_gla, ffn1_norm, ffn1_w_in, ffn1_w_out, mix_norm, w_in,
           a_ws, a_bs, a_vnorm, a_onorm, b_wa2, b_ba, b_onorm, w_out, ffn2_norm, ffn2_w_in,
           ffn2_w_out, final_norm):
    batch, seq, _ = x_prompt.shape
    n_dec, dec_seq, _ = x_sample.shape
    assert ffn1_norm.shape[0] == 1, "single-layer kernel"
    assert seq % A_CHUNK == 0 and ROWS % dec_seq == 0 and dec_seq <= B_CHUNK

    g1 = ffn1_norm.reshape(1, D_MODEL)
    wi1 = ffn1_w_in.reshape(D_MODEL, 2 * D_FF)
    wo1 = ffn1_w_out.reshape(D_FF, D_MODEL)
    g2 = mix_norm.reshape(1, D_MODEL)
    w_in_t = jnp.swapaxes(w_in, 1, 2).reshape(IN_COLS, D_MODEL)
    w_out2d = w_out.reshape(D_MODEL, D_MODEL)
    g3 = ffn2_norm.reshape(1, D_MODEL)
    wi2 = ffn2_w_in.reshape(D_MODEL, 2 * D_FF)
    wo2 = ffn2_w_out.reshape(D_FF, D_MODEL)
    gf = final_norm.reshape(1, D_MODEL)
    mix_args = (a_ws[0], a_bs[0], a_vnorm[0], a_onorm[0], b_wa2[0], b_ba[0], b_onorm[0])
    consts_p = _mix_consts(*mix_args, chunk=B_CHUNK, seq_len=A_CHUNK)
    consts_s = _mix_consts(*mix_args, chunk=dec_seq, seq_len=dec_seq)

    xp = x_prompt.reshape(batch * seq, D_MODEL)
    xs = x_sample.reshape(n_dec * dec_seq, D_MODEL)
    s0 = state_gla.reshape(state_gla.shape[1:])

    x1p, x1s, wp, wpt, wout = _ffn_call(
        xp, xs, g1, wi1, wo1, gf, False, "ffn1",
        cast_jobs=(_cast_transposed(w_in_t, 0, MXU_DIM, P_MAIN),
                   _cast_transposed(w_in_t, P_MAIN, P_COLS - P_MAIN, P_COLS - P_MAIN,
                                    valid_cols=IN_COLS - P_MAIN),
                   _cast_rows(w_out2d)))
    x2p, st_p, wi2_bf, wo2_bf = _mixproj_prompt_call(
        x1p, g2, wp, wpt, wout, consts_p, batch, seq, tile=1024, group=512,
        cast_jobs=(_cast_rows(wi2), _cast_rows(wo2)))
    x2s, va_s, st_s = _mixproj_sample_call(x1s, s0, g2, wp, wpt, wout, consts_s, n_dec, dec_seq,
                                           tile=256, group=128)
    out_p, out_s = _ffn_call(x2p, x2s, g3, wi2_bf, wo2_bf, gf, True, "ffn2")

    return (
        out_p.reshape(batch, seq, D_MODEL),
        out_s.reshape(n_dec, dec_seq, D_MODEL),
        st_p[None],
        st_s[None],
        va_s.reshape(1, n_dec, dec_seq, A_WIDTH),
    )
```

```python
import functools

import numpy as np
import jax
import jax.numpy as jnp
from jax import lax
from jax.experimental import pallas as pl
from jax.experimental.pallas import tpu as pltpu

D_MODEL = 1024
D_FF = 2816
A_WIDTH = 512
A_GROUPS = 4
A_CHUNK = 128
B_HEADS = 4
B_DK = 64
B_DV = 128
B_LOWRANK = 16
B_GATE_NORMALIZER = 16.0
B_CHUNK = 64
EPS = 1e-6

OFF_U, OFF_V, OFF_Q, OFF_K, OFF_VB, OFF_R, OFF_LR = 0, 512, 1024, 1280, 1536, 2048, 2560
IN_COLS = 2576

LANES = 128
P_COLS = -(-IN_COLS // LANES) * LANES
MXU_DIM = 256
ROWS = 128
FF_CHUNK = MXU_DIM
N_FF_CHUNKS = D_FF // FF_CHUNK
VMEM_LIMIT = 56 * 1024 * 1024

FFN_TILE = 1024
FFN_HALF = FFN_TILE // 2
WEIGHT_SLOT_ROWS = 64
P_MAIN = (IN_COLS // LANES) * LANES
CAST_BLOCKS = 16
MIX_STAGGER = 10

BF = jnp.bfloat16
F32 = jnp.float32


def _dot(a, b):
    return jnp.dot(a, b, preferred_element_type=F32)


def _dot_nt(a, b):
    return lax.dot_general(a, b, (((1,), (1,)), ((), ())), preferred_element_type=F32)


def _rmsnorm(x, g):
    return (x * lax.rsqrt(jnp.mean(x * x, axis=-1, keepdims=True) + EPS)) * g


def _gelu_tanh(x):
    c = np.float32(np.sqrt(2.0 / np.pi))
    ca = np.float32(np.sqrt(2.0 / np.pi) * 0.044715)
    t = jnp.tanh(x * (c + ca * (x * x)))
    hx = 0.5 * x
    return hx + hx * t


def _silu(x):
    return x * (1.0 / (1.0 + jnp.exp(-x)))


def _log_sigmoid(x):
    return jnp.minimum(x, 0.0) - jnp.log(1.0 + jnp.exp(-jnp.abs(x)))


def _staggered(gens, stagger):
    live = [True] * len(gens)
    t = 0
    while any(live):
        for i, gen in enumerate(gens):
            if live[i] and t >= i * stagger:
                try:
                    next(gen)
                except StopIteration:
                    live[i] = False
        t += 1
        yield


def _cast_rows(w):
    rows, cols = w.shape
    blk = (rows // CAST_BLOCKS, cols)
    return dict(w=w, src_block=blk, src_index=lambda r: (r, 0), dst_block=blk,
                dst_index=lambda r: (r, 0), out=jax.ShapeDtypeStruct(w.shape, BF), transposed=None)


def _cast_transposed(w_t, row0, n_rows, out_cols, valid_cols=None):
    cols = w_t.shape[1]
    first, last = row0 // n_rows, (row0 + out_cols) // n_rows - 1
    return dict(w=w_t, src_block=(n_rows, cols), dst_block=(cols, n_rows),
                src_index=lambda r: (jnp.minimum(first + r, last), 0),
                dst_index=lambda r: (0, jnp.minimum(r, last - first)),
                out=jax.ShapeDtypeStruct((cols, out_cols), BF),
                transposed=n_rows if valid_cols is None else valid_cols)


def _cast_specs(jobs, row_block):
    in_specs = [pl.BlockSpec(j["src_block"], lambda *ids, j=j: j["src_index"](row_block(*ids)))
                for j in jobs]
    out_specs = [pl.BlockSpec(j["dst_block"], lambda *ids, j=j: j["dst_index"](row_block(*ids)))
                 for j in jobs]
    return in_specs, out_specs, [j["out"] for j in jobs]


def _cast_blocks(src_refs, dst_refs, transposed):
    for src, dst, valid in zip(src_refs, dst_refs, transposed):
        x = src[...]
        if valid is not None:
            x = x.T
            if valid < x.shape[1]:
                lane = lax.broadcasted_iota(jnp.int32, x.shape, 1)
                x = jnp.where(lane < valid, x, 0.0)
        dst[...] = x.astype(BF)


def _weight_chunks(wi_hbm, wo_hbm, wi_ref, wo_ref):
    chunks = []
    for r0 in range(0, D_MODEL, WEIGHT_SLOT_ROWS):
        for c0 in range(0, 2 * D_FF, D_MODEL):
            w = min(D_MODEL, 2 * D_FF - c0)
            view = (pl.ds(r0, WEIGHT_SLOT_ROWS), pl.ds(c0, w))
            chunks.append((wi_hbm.at[view], wi_ref.at[view], w))
    for r0 in range(0, D_FF, WEIGHT_SLOT_ROWS):
        view = (pl.ds(r0, WEIGHT_SLOT_ROWS), pl.ds(0, D_MODEL))
        chunks.append((wo_hbm.at[view], wo_ref.at[view], D_MODEL))
    return chunks


def _load_weights_bf16(wi_hbm, wo_hbm, wi_ref, wo_ref, stage_ref, sem):
    chunks = _weight_chunks(wi_hbm, wo_hbm, wi_ref, wo_ref)
    n_slots = stage_ref.shape[0] // WEIGHT_SLOT_ROWS

    def slot(k):
        return stage_ref.at[pl.ds((k % n_slots) * WEIGHT_SLOT_ROWS, WEIGHT_SLOT_ROWS),
                            pl.ds(0, chunks[k][2])]

    def copy(k):
        return pltpu.make_async_copy(chunks[k][0], slot(k), sem.at[k % n_slots])

    for k in range(n_slots):
        copy(k).start()
    for k in range(len(chunks)):
        copy(k).wait()
        chunks[k][1][...] = slot(k)[...].astype(BF)
        if k + n_slots < len(chunks):
            copy(k + n_slots).start()


def _ffn_rows(load_x, g_ref, wi_ref, wo_ref, acc_ref, rows, finish):
    h = _rmsnorm(load_x(rows), g_ref[...]).astype(BF)
    yield
    for j in range(N_FF_CHUNKS):
        cs = slice(j * FF_CHUNK, (j + 1) * FF_CHUNK)
        gate = _dot(h, wi_ref[:, cs])
        up = _dot(h, wi_ref[:, D_FF + j * FF_CHUNK:D_FF + (j + 1) * FF_CHUNK])
        act = (_silu(gate) * up).astype(BF)
        part = _dot(act, wo_ref[cs, :])
        if j == 0:
            acc_ref[rows, :] = part
        else:
            acc_ref[rows, :] += part
        yield
    finish(load_x(rows), acc_ref[rows, :])


def _ffn_kernel(*refs, final_norm, stream_weights, cast_modes):
    n_cast = len(cast_modes)
    xa_ref, xb_ref, g_ref, wi_in, wo_in, gf_ref = refs[:6]
    cast_src = refs[6:6 + n_cast]
    oa_ref, ob_ref = refs[6 + n_cast:8 + n_cast]
    cast_dst = refs[8 + n_cast:8 + 2 * n_cast]
    scratch = refs[8 + 2 * n_cast:]
    first = pl.program_id(0) == 0

    if stream_weights:
        wi_ref, wo_ref, acc_ref, sem = scratch

        @pl.when(first)
        def _():
            _load_weights_bf16(wi_in, wo_in, wi_ref, wo_ref, acc_ref, sem)
    else:
        (acc_ref,) = scratch
        wi_ref, wo_ref = wi_in, wo_in

    def load_x(rows):
        return jnp.where(first, xb_ref[rows, :], xa_ref[rows, :])

    def half(i):
        rows = slice(i * FFN_HALF, (i + 1) * FFN_HALF)

        def finish(x, acc):
            y = x + 0.5 * acc
            oa_ref[rows, :] = _rmsnorm(y, gf_ref[...]) if final_norm else y

        return _ffn_rows(load_x, g_ref, wi_ref, wo_ref, acc_ref, rows, finish)

    for _ in _staggered([half(i) for i in range(FFN_TILE // FFN_HALF)], 1):
        pass
    _cast_blocks(cast_src, cast_dst, cast_modes)

    @pl.when(first)
    def _():
        ob_ref[...] = oa_ref[...]


def _resident(shape):
    nd = len(shape)
    return pl.BlockSpec(shape, lambda *_: (0,) * nd, pipeline_mode=pl.Buffered(1))


def _ffn_call(xa, xb, g, wi, wo, gf, final_norm, name, cast_jobs=()):
    ta = xa.shape[0]
    n_a = ta // FFN_TILE
    assert ta % FFN_TILE == 0 and xb.shape[0] == FFN_TILE
    assert not cast_jobs or n_a == CAST_BLOCKS
    stream_weights = wi.dtype == F32
    tile_a = lambda i: (jnp.maximum(i - 1, 0), 0)
    cast_in, cast_out, cast_shapes = _cast_specs(cast_jobs, lambda i: jnp.maximum(i - 1, 0))
    if stream_weights:
        w_specs = [pl.BlockSpec(memory_space=pl.ANY)] * 2
        w_scratch = [pltpu.VMEM(wi.shape, BF), pltpu.VMEM(wo.shape, BF)]
        sems = [pltpu.SemaphoreType.DMA((FFN_TILE // WEIGHT_SLOT_ROWS,))]
    else:
        w_specs = [_resident(wi.shape), _resident(wo.shape)]
        w_scratch, sems = [], []
    return pl.pallas_call(
        functools.partial(_ffn_kernel, final_norm=final_norm, stream_weights=stream_weights,
                          cast_modes=tuple(j["transposed"] for j in cast_jobs)),
        grid=(1 + n_a,),
        in_specs=[
            pl.BlockSpec((FFN_TILE, D_MODEL), tile_a),
            _resident((FFN_TILE, D_MODEL)),
            _resident((1, D_MODEL)),
            *w_specs,
            _resident((1, D_MODEL)),
            *cast_in,
        ],
        out_specs=[
            pl.BlockSpec((FFN_TILE, D_MODEL), tile_a),
            pl.BlockSpec((FFN_TILE, D_MODEL), lambda i: (0, 0)),
            *cast_out,
        ],
        out_shape=[
            jax.ShapeDtypeStruct((ta, D_MODEL), F32),
            jax.ShapeDtypeStruct((FFN_TILE, D_MODEL), F32),
            *cast_shapes,
        ],
        scratch_shapes=w_scratch + [pltpu.VMEM((FFN_TILE, D_MODEL), F32)] + sems,
        compiler_params=pltpu.CompilerParams(
            dimension_semantics=("arbitrary",), vmem_limit_bytes=VMEM_LIMIT),
        name=name,
    )(xa, xb, g, wi, wo, gf, *[j["w"] for j in cast_jobs])


def _cum_matrices(chunk):
    r = np.arange(ROWS)[:, None]
    k = np.arange(ROWS)[None, :]
    same = (r // chunk) == (k // chunk)
    m_cum = same & (k <= r)
    m_ref = same & ((k % chunk) <= (chunk // 2))
    m_last = same
    m = np.concatenate([m_cum, m_ref, m_last], axis=0).astype(np.float32)
    return np.concatenate([m, m], axis=1)


def _mix_block(cols, chunk, consts, get_state, put_state, emit):
    (wmix_ref, bias_ref, vnorm_ref, anorm_ref, wa2_ref, ba_ref, onorm_ref, cum_ref) = consts
    n_chunks = ROWS // chunk
    shift = int(np.log2(chunk))

    z = _dot(cols(OFF_LR, LANES).astype(BF), wa2_ref[...]) + ba_ref[...]
    la = _log_sigmoid(z) * (1.0 / B_GATE_NORMALIZER)
    la_hi = la.astype(BF)
    la_lo = (la - la_hi.astype(F32)).astype(BF)
    sums = _dot(cum_ref[...], jnp.concatenate([la_hi, la_lo], axis=0))
    b = sums[0:ROWS]
    b_ref = sums[ROWS:2 * ROWS]
    b_last = sums[2 * ROWS:3 * ROWS]
    yield

    ya = []
    va = []
    ssq = None
    for g in range(A_GROUPS):
        gs = slice(g * LANES, (g + 1) * LANES)
        u = _gelu_tanh(cols(OFF_U + g * LANES, LANES))
        v = _rmsnorm(_gelu_tanh(cols(OFF_V + g * LANES, LANES)), vnorm_ref[:, gs])
        yield
        zg = _dot(wmix_ref[g], v.astype(BF)) + bias_ref[g]
        yg = u * zg
        s = jnp.sum(yg * yg, axis=-1, keepdims=True)
        ssq = s if ssq is None else ssq + s
        ya.append(yg)
        va.append(v)
        yield
    inv = lax.rsqrt(ssq * (1.0 / A_WIDTH) + EPS)
    ya = [(ya[g] * inv) * anorm_ref[:, g * LANES:(g + 1) * LANES] for g in range(A_GROUPS)]
    yield
    gate = []
    for h in range(B_HEADS):
        gate.append(_silu(cols(OFF_R + h * B_DV, B_DV)))
        if h % 2 == 1:
            yield

    ri = lax.broadcasted_iota(jnp.int32, (ROWS, ROWS), 0)
    ci = lax.broadcasted_iota(jnp.int32, (ROWS, ROWS), 1)
    rchunk = lax.shift_right_logical(ri, shift)
    cchunk = lax.shift_right_logical(ci, shift)
    causal = (rchunk == cchunk) & (ci <= ri)
    lane = lax.broadcasted_iota(jnp.int32, (ROWS, LANES), 1)
    head0 = lane < B_DK

    yb = []
    for pair in range(B_HEADS // 2):
        ls = slice(pair * LANES, (pair + 1) * LANES)
        q = cols(OFF_Q + pair * LANES, LANES) * np.float32(B_DK ** -0.5)
        k = cols(OFF_K + pair * LANES, LANES)
        vv_bf = cols(OFF_VB + pair * 2 * B_DV, 2 * B_DV).astype(BF)
        bp, brp, blp = b[:, ls], b_ref[:, ls], b_last[:, ls]
        qs = q * jnp.exp(bp - brp)
        ks = (k * jnp.exp(brp - bp)).astype(BF)
        kl_t = (k * jnp.exp(blp - bp)).T
        bl_t = blp.T
        qb = (q * jnp.exp(bp)).astype(BF)
        yield

        o_heads = []
        for hh in range(2):
            msk = head0 if hh == 0 else jnp.logical_not(head0)
            qh = jnp.where(msk, qs, 0.0).astype(BF)
            sc = jnp.where(causal, _dot_nt(qh, ks), 0.0).astype(BF)
            o_heads.append(_dot(sc, vv_bf[:, hh * B_DV:(hh + 1) * B_DV]))
            yield
        o = jnp.concatenate(o_heads, axis=1)

        o_inter = []
        for j in range(n_chunks):
            s_prev = get_state(pair, j)
            o_inter.append(_dot(qb[j * chunk:(j + 1) * chunk], _blockdiag_bf16(*s_prev)))
            in_chunk = cchunk == j
            upd = _dot(jnp.where(in_chunk, kl_t, 0.0).astype(BF), vv_bf)
            dec = jnp.exp(bl_t[:, j * chunk:j * chunk + 1])
            put_state(pair, j, tuple(
                dec[hh * B_DK:(hh + 1) * B_DK] * s_prev[hh]
                + upd[hh * B_DK:(hh + 1) * B_DK, hh * B_DV:(hh + 1) * B_DV] for hh in range(2)))
        o = o + jnp.concatenate(o_inter, axis=0)
        yield

        for hh in range(2):
            h_idx = 2 * pair + hh
            hs = slice(h_idx * B_DV, (h_idx + 1) * B_DV)
            oh = _rmsnorm(o[:, hh * B_DV:(hh + 1) * B_DV], onorm_ref[:, hs])
            yb.append(oh * gate[h_idx])
        yield

    emit(jnp.concatenate(ya + yb, axis=1).astype(BF), jnp.concatenate(va, axis=1))


def _blockdiag_bf16(s0, s1):
    s0 = s0.astype(BF)
    s1 = s1.astype(BF)
    zero = jnp.zeros_like(s0)
    return jnp.concatenate(
        [jnp.concatenate([s0, zero], axis=1), jnp.concatenate([zero, s1], axis=1)], axis=0)


def _in_proj(x1_ref, g2_ref, wp_refs, p_scr, g, rows):
    wp_ref, wpt_ref = wp_refs
    h = _rmsnorm(x1_ref[rows, :], g2_ref[...]).astype(BF)
    for c0 in range(0, P_MAIN, MXU_DIM):
        cs = slice(c0, c0 + MXU_DIM)
        p_scr[g, :, cs] = _dot(h, wp_ref[:, cs])
    p_scr[g, :, P_MAIN:P_COLS] = _dot(h, wpt_ref[...])


def _out_proj(x1_ref, y_scr, wout_ref, x2_ref, g, rows):
    y = y_scr[g]
    for c0 in range(0, D_MODEL, MXU_DIM):
        cs = slice(c0, c0 + MXU_DIM)
        x2_ref[rows, cs] = x1_ref[rows, cs] + _dot(y, wout_ref[:, cs])


def _mixproj_tile(x1_ref, g2_ref, wp_refs, wout_ref, x2_ref, p_scr, y_scr, group, n_groups,
                  make_block):
    rows = [slice(g * group, (g + 1) * group) for g in range(n_groups)]
    for g in range(n_groups):
        _in_proj(x1_ref, g2_ref, wp_refs, p_scr, g, rows[g])
    blocks = [make_block(g, i) for g in range(n_groups) for i in range(group // ROWS)]
    for _ in _staggered(blocks, MIX_STAGGER):
        pass
    for g in range(n_groups):
        _out_proj(x1_ref, y_scr, wout_ref, x2_ref, g, rows[g])


def _block_cols(p_scr, g, i):
    return lambda off, width: p_scr[g, i * ROWS:(i + 1) * ROWS, off:off + width]


def _mixproj_prompt_kernel(*refs, group, n_groups, cast_modes):
    n_cast = len(cast_modes)
    (x1_ref, g2_ref, wp_ref, wpt_ref, wout_ref, wmix_ref, bias_ref, vnorm_ref, anorm_ref, wa2_ref,
     ba_ref, onorm_ref, cum_ref) = refs[:13]
    cast_src = refs[13:13 + n_cast]
    x2_ref, st_ref = refs[13 + n_cast:15 + n_cast]
    cast_dst = refs[15 + n_cast:15 + 2 * n_cast]
    p_scr, y_scr, s_scr = refs[15 + 2 * n_cast:]

    @pl.when(pl.program_id(1) == 0)
    def _():
        s_scr[...] = jnp.zeros_like(s_scr)

    consts = (wmix_ref, bias_ref, vnorm_ref, anorm_ref, wa2_ref, ba_ref, onorm_ref, cum_ref)
    cur = {pair: (s_scr[2 * pair], s_scr[2 * pair + 1]) for pair in range(B_HEADS // 2)}

    def get_state(pair, j):
        return cur[pair]

    def put_state(pair, j, s):
        cur[pair] = s

    def make_block(g, i):
        def emit(y, va):
            y_scr[g, i * ROWS:(i + 1) * ROWS, :] = y
        return _mix_block(_block_cols(p_scr, g, i), B_CHUNK, consts, get_state, put_state, emit)

    _mixproj_tile(x1_ref, g2_ref, (wp_ref, wpt_ref), wout_ref, x2_ref, p_scr, y_scr, group,
                  n_groups, make_block)
    _cast_blocks(cast_src, cast_dst, cast_modes)

    for pair in range(B_HEADS // 2):
        for hh in range(2):
            s_scr[2 * pair + hh] = cur[pair][hh]
            st_ref[0, 2 * pair + hh] = cur[pair][hh]


def _mixproj_sample_kernel(x1_ref, s0_ref, g2_ref, wp_ref, wpt_ref, wout_ref, wmix_ref, bias_ref,
                           vnorm_ref, anorm_ref, wa2_ref, ba_ref, onorm_ref, cum_ref, x2_ref,
                           va_ref, st_ref, p_scr, y_scr, *, chunk, group, n_groups):
    consts = (wmix_ref, bias_ref, vnorm_ref, anorm_ref, wa2_ref, ba_ref, onorm_ref, cum_ref)
    n_blk = group // ROWS
    per_blk = ROWS // chunk

    def make_block(g, i):
        blk = g * n_blk + i

        def get_state(pair, j):
            n = blk * per_blk + j
            return s0_ref[n, 2 * pair], s0_ref[n, 2 * pair + 1]

        def put_state(pair, j, s):
            n = blk * per_blk + j
            st_ref[n, 2 * pair] = s[0]
            st_ref[n, 2 * pair + 1] = s[1]

        def emit(y, va):
            y_scr[g, i * ROWS:(i + 1) * ROWS, :] = y
            va_ref[blk * ROWS:(blk + 1) * ROWS, :] = va

        return _mix_block(_block_cols(p_scr, g, i), chunk, consts, get_state, put_state, emit)

    _mixproj_tile(x1_ref, g2_ref, (wp_ref, wpt_ref), wout_ref, x2_ref, p_scr, y_scr, group,
                  n_groups, make_block)


def _mix_const_specs():
    return [
        _resident((A_GROUPS, ROWS, ROWS)),
        _resident((A_GROUPS, ROWS, LANES)),
        _resident((1, A_WIDTH)),
        _resident((1, A_WIDTH)),
        _resident((LANES, B_HEADS * B_DK)),
        _resident((1, B_HEADS * B_DK)),
        _resident((1, B_HEADS * B_DV)),
        _resident((3 * ROWS, 2 * ROWS)),
    ]


def _proj_specs():
    return [
        _resident((1, D_MODEL)),
        _resident((D_MODEL, P_MAIN)),
        _resident((D_MODEL, P_COLS - P_MAIN)),
        _resident((D_MODEL, D_MODEL)),
    ]


def _mixproj_scratch(group, n_groups):
    return [pltpu.VMEM((n_groups, group, P_COLS), F32),
            pltpu.VMEM((n_groups, group, D_MODEL), BF)]


def _mixproj_prompt_call(x1, g2, wp, wpt, wout, consts, batch, seq, tile, group, cast_jobs=()):
    n_tiles = seq // tile
    n_groups = tile // group
    assert not cast_jobs or batch * n_tiles == CAST_BLOCKS
    row = lambda b, s: (b * n_tiles + s, 0)
    cast_in, cast_out, cast_shapes = _cast_specs(cast_jobs, lambda b, s: b * n_tiles + s)
    return pl.pallas_call(
        functools.partial(_mixproj_prompt_kernel, group=group, n_groups=n_groups,
                          cast_modes=tuple(j["transposed"] for j in cast_jobs)),
        grid=(batch, n_tiles),
        in_specs=[pl.BlockSpec((tile, D_MODEL), row)] + _proj_specs() + _mix_const_specs()
        + cast_in,
        out_specs=[
            pl.BlockSpec((tile, D_MODEL), row),
            pl.BlockSpec((1, B_HEADS, B_DK, B_DV), lambda b, s: (b, 0, 0, 0)),
            *cast_out,
        ],
        out_shape=[
            jax.ShapeDtypeStruct((batch * seq, D_MODEL), F32),
            jax.ShapeDtypeStruct((batch, B_HEADS, B_DK, B_DV), F32),
            *cast_shapes,
        ],
        scratch_shapes=_mixproj_scratch(group, n_groups)
        + [pltpu.VMEM((B_HEADS, B_DK, B_DV), F32)],
        compiler_params=pltpu.CompilerParams(
            dimension_semantics=("arbitrary", "arbitrary"), vmem_limit_bytes=VMEM_LIMIT),
        name="mixproj_prompt",
    )(x1, g2, wp, wpt, wout, *consts, *[j["w"] for j in cast_jobs])


def _mixproj_sample_call(x1, s0, g2, wp, wpt, wout, consts, n_seq, seq, tile, group):
    seq_per_step = tile // seq
    n_groups = tile // group
    row = lambda i: (i, 0)
    state = lambda i: (i, 0, 0, 0)
    return pl.pallas_call(
        functools.partial(_mixproj_sample_kernel, chunk=seq, group=group, n_groups=n_groups),
        grid=(n_seq // seq_per_step,),
        in_specs=[
            pl.BlockSpec((tile, D_MODEL), row),
            pl.BlockSpec((seq_per_step, B_HEADS, B_DK, B_DV), state),
        ] + _proj_specs() + _mix_const_specs(),
        out_specs=[
            pl.BlockSpec((tile, D_MODEL), row),
            pl.BlockSpec((tile, A_WIDTH), row),
            pl.BlockSpec((seq_per_step, B_HEADS, B_DK, B_DV), state),
        ],
        out_shape=[
            jax.ShapeDtypeStruct((n_seq * seq, D_MODEL), F32),
            jax.ShapeDtypeStruct((n_seq * seq, A_WIDTH), F32),
            jax.ShapeDtypeStruct((n_seq, B_HEADS, B_DK, B_DV), F32),
        ],
        scratch_shapes=_mixproj_scratch(group, n_groups),
        compiler_params=pltpu.CompilerParams(
            dimension_semantics=("arbitrary",), vmem_limit_bytes=VMEM_LIMIT),
        name="mixproj_sample",
    )(x1, s0, g2, wp, wpt, wout, *consts)


def _mix_consts(a_ws, a_bs, a_vnorm, a_onorm, b_wa2, b_ba, b_onorm, chunk, seq_len):
    n_rep = ROWS // seq_len
    tril = jnp.tril(jnp.ones((seq_len, seq_len), dtype=bool))
    w = jnp.where(tril[None], a_ws[:, :seq_len, :seq_len], 0.0)
    if n_rep > 1:
        w = jax.vmap(lambda wg: jnp.kron(jnp.eye(n_rep, dtype=wg.dtype), wg))(w)
    bias = jnp.tile(a_bs[:, :seq_len], (1, n_rep))
    bias = jnp.broadcast_to(bias[:, :, None], (A_GROUPS, ROWS, LANES))
    wa2 = jnp.zeros((LANES, B_HEADS * B_DK), F32).at[:B_LOWRANK].set(b_wa2)
    return (
        w.astype(BF),
        bias,
        a_vnorm.reshape(1, A_WIDTH),
        a_onorm.reshape(1, A_WIDTH),
        wa2.astype(BF),
        b_ba.reshape(1, B_HEADS * B_DK),
        b_onorm.reshape(1, B_HEADS * B_DV),
        jnp.asarray(_cum_matrices(chunk), dtype=BF),
    )


def kernel(x_prompt, x_sample, state_gla, ffn1_norm, ffn1_w_in, ffn1_w_out, mix_norm, w_in,
           a_ws, a_bs, a_vnorm, a_onorm, b_wa2, b_ba, b_onorm, w_out, ffn2_norm, ffn2_w_in,
           ffn2_w_out, final_norm):
    batch, seq, _ = x_prompt.shape
    n_dec, dec_seq, _ = x_sample.shape
    assert ffn1_norm.shape[0] == 1, "single-layer kernel"
    assert seq % A_CHUNK == 0 and ROWS % dec_seq == 0 and dec_seq <= B_CHUNK

    g1 = ffn1_norm.reshape(1, D_MODEL)
    wi1 = ffn1_w_in.reshape(D_MODEL, 2 * D_FF)
    wo1 = ffn1_w_out.reshape(D_FF, D_MODEL)
    g2 = mix_norm.reshape(1, D_MODEL)
    w_in_t = jnp.swapaxes(w_in, 1, 2).reshape(IN_COLS, D_MODEL)
    w_out2d = w_out.reshape(D_MODEL, D_MODEL)
    g3 = ffn2_norm.reshape(1, D_MODEL)
    wi2 = ffn2_w_in.reshape(D_MODEL, 2 * D_FF)
    wo2 = ffn2_w_out.reshape(D_FF, D_MODEL)
    gf = final_norm.reshape(1, D_MODEL)
    mix_args = (a_ws[0], a_bs[0], a_vnorm[0], a_onorm[0], b_wa2[0], b_ba[0], b_onorm[0])
    consts_p = _mix_consts(*mix_args, chunk=B_CHUNK, seq_len=A_CHUNK)
    consts_s = _mix_consts(*mix_args, chunk=dec_seq, seq_len=dec_seq)

    xp = x_prompt.reshape(batch * seq, D_MODEL)
    xs = x_sample.reshape(n_dec * dec_seq, D_MODEL)
    s0 = state_gla.reshape(state_gla.shape[1:])

    x1p, x1s, wp, wpt, wout = _ffn_call(
        xp, xs, g1, wi1, wo1, gf, False, "ffn1",
        cast_jobs=(_cast_transposed(w_in_t, 0, MXU_DIM, P_MAIN),
                   _cast_transposed(w_in_t, P_MAIN, P_COLS - P_MAIN, P_COLS - P_MAIN,
                                    valid_cols=IN_COLS - P_MAIN),
                   _cast_rows(w_out2d)))
    x2p, st_p, wi2_bf, wo2_bf = _mixproj_prompt_call(
        x1p, g2, wp, wpt, wout, consts_p, batch, seq, tile=1024, group=512,
        cast_jobs=(_cast_rows(wi2), _cast_rows(wo2)))
    x2s, va_s, st_s = _mixproj_sample_call(x1s, s0, g2, wp, wpt, wout, consts_s, n_dec, dec_seq,
                                           tile=256, group=128)
    out_p, out_s = _ffn_call(x2p, x2s, g3, wi2_bf, wo2_bf, gf, True, "ffn2")

    return (
        out_p.reshape(batch, seq, D_MODEL),
        out_s.reshape(n_dec, dec_seq, D_MODEL),
        st_p[None],
        st_s[None],
        va_s.reshape(1, n_dec, dec_seq, A_WIDTH),
    )
```

```python
import functools

import numpy as np
import jax
import jax.numpy as jnp
from jax import lax
from jax.experimental import pallas as pl
from jax.experimental.pallas import tpu as pltpu

D_MODEL = 1024
D_FF = 2816
A_WIDTH = 512
A_GROUPS = 4
A_CHUNK = 128
B_HEADS = 4
B_DK = 64
B_DV = 128
B_LOWRANK = 16
B_GATE_NORMALIZER = 16.0
B_CHUNK = 64
EPS = 1e-6

OFF_U, OFF_V, OFF_Q, OFF_K, OFF_VB, OFF_R, OFF_LR = 0, 512, 1024, 1280, 1536, 2048, 2560
IN_COLS = 2576

LANES = 128
P_COLS = -(-IN_COLS // LANES) * LANES
MXU_DIM = 256
ROWS = 128
FF_CHUNK = MXU_DIM
N_FF_CHUNKS = D_FF // FF_CHUNK
VMEM_LIMIT = 56 * 1024 * 1024

FFN_TILE = 1024
FFN_HALF = FFN_TILE // 2
WEIGHT_SLOT_ROWS = 128
P_MAIN = (IN_COLS // LANES) * LANES
CAST_BLOCKS = 16
MIX_STAGGER = 10

BF = jnp.bfloat16
F32 = jnp.float32


def _dot(a, b):
    return jnp.dot(a, b, preferred_element_type=F32)


def _dot_nt(a, b):
    return lax.dot_general(a, b, (((1,), (1,)), ((), ())), preferred_element_type=F32)


def _rmsnorm(x, g):
    return (x * lax.rsqrt(jnp.mean(x * x, axis=-1, keepdims=True) + EPS)) * g


def _gelu_tanh(x):
    c = np.float32(np.sqrt(2.0 / np.pi))
    ca = np.float32(np.sqrt(2.0 / np.pi) * 0.044715)
    t = jnp.tanh(x * (c + ca * (x * x)))
    hx = 0.5 * x
    return hx + hx * t


def _silu(x):
    return x * (1.0 / (1.0 + jnp.exp(-x)))


def _log_sigmoid(x):
    return jnp.minimum(x, 0.0) - jnp.log(1.0 + jnp.exp(-jnp.abs(x)))


def _staggered(gens, stagger):
    live = [True] * len(gens)
    t = 0
    while any(live):
        for i, gen in enumerate(gens):
            if live[i] and t >= i * stagger:
                try:
                    next(gen)
                except StopIteration:
                    live[i] = False
        t += 1
        yield


def _cast_rows(w):
    rows, cols = w.shape
    blk = (rows // CAST_BLOCKS, cols)
    return dict(w=w, src_block=blk, src_index=lambda r: (r, 0), dst_block=blk,
                dst_index=lambda r: (r, 0), out=jax.ShapeDtypeStruct(w.shape, BF), transposed=None)


def _cast_transposed(w_t, row0, n_rows, out_cols, valid_cols=None):
    cols = w_t.shape[1]
    first, last = row0 // n_rows, (row0 + out_cols) // n_rows - 1
    return dict(w=w_t, src_block=(n_rows, cols), dst_block=(cols, n_rows),
                src_index=lambda r: (jnp.minimum(first + r, last), 0),
                dst_index=lambda r: (0, jnp.minimum(r, last - first)),
                out=jax.ShapeDtypeStruct((cols, out_cols), BF),
                transposed=n_rows if valid_cols is None else valid_cols)


def _cast_specs(jobs, row_block):
    in_specs = [pl.BlockSpec(j["src_block"], lambda *ids, j=j: j["src_index"](row_block(*ids)))
                for j in jobs]
    out_specs = [pl.BlockSpec(j["dst_block"], lambda *ids, j=j: j["dst_index"](row_block(*ids)))
                 for j in jobs]
    return in_specs, out_specs, [j["out"] for j in jobs]


def _cast_blocks(src_refs, dst_refs, transposed):
    for src, dst, valid in zip(src_refs, dst_refs, transposed):
        x = src[...]
        if valid is not None:
            x = x.T
            if valid < x.shape[1]:
                lane = lax.broadcasted_iota(jnp.int32, x.shape, 1)
                x = jnp.where(lane < valid, x, 0.0)
        dst[...] = x.astype(BF)


def _weight_chunks(wi_hbm, wo_hbm, wi_ref, wo_ref):
    chunks = []
    for r0 in range(0, D_MODEL, WEIGHT_SLOT_ROWS):
        for c0 in range(0, 2 * D_FF, D_MODEL):
            w = min(D_MODEL, 2 * D_FF - c0)
            view = (pl.ds(r0, WEIGHT_SLOT_ROWS), pl.ds(c0, w))
            chunks.append((wi_hbm.at[view], wi_ref.at[view], w))
    for r0 in range(0, D_FF, WEIGHT_SLOT_ROWS):
        view = (pl.ds(r0, WEIGHT_SLOT_ROWS), pl.ds(0, D_MODEL))
        chunks.append((wo_hbm.at[view], wo_ref.at[view], D_MODEL))
    return chunks


def _load_weights_bf16(wi_hbm, wo_hbm, wi_ref, wo_ref, stage_ref, sem):
    chunks = _weight_chunks(wi_hbm, wo_hbm, wi_ref, wo_ref)
    n_slots = stage_ref.shape[0] // WEIGHT_SLOT_ROWS

    def slot(k):
        return stage_ref.at[pl.ds((k % n_slots) * WEIGHT_SLOT_ROWS, WEIGHT_SLOT_ROWS),
                            pl.ds(0, chunks[k][2])]

    def copy(k):
        return pltpu.make_async_copy(chunks[k][0], slot(k), sem.at[k % n_slots])

    for k in range(n_slots):
        copy(k).start()
    for k in range(len(chunks)):
        copy(k).wait()
        chunks[k][1][...] = slot(k)[...].astype(BF)
        if k + n_slots < len(chunks):
            copy(k + n_slots).start()


def _ffn_rows(load_x, g_ref, wi_ref, wo_ref, acc_ref, rows, finish):
    h = _rmsnorm(load_x(rows), g_ref[...]).astype(BF)
    yield
    for j in range(N_FF_CHUNKS):
        cs = slice(j * FF_CHUNK, (j + 1) * FF_CHUNK)
        gate = _dot(h, wi_ref[:, cs])
        up = _dot(h, wi_ref[:, D_FF + j * FF_CHUNK:D_FF + (j + 1) * FF_CHUNK])
        act = (_silu(gate) * up).astype(BF)
        part = _dot(act, wo_ref[cs, :])
        if j == 0:
            acc_ref[rows, :] = part
        else:
            acc_ref[rows, :] += part
        yield
    finish(load_x(rows), acc_ref[rows, :])


def _ffn_kernel(*refs, final_norm, stream_weights, cast_modes):
    n_cast = len(cast_modes)
    xa_ref, xb_ref, g_ref, wi_in, wo_in, gf_ref = refs[:6]
    cast_src = refs[6:6 + n_cast]
    oa_ref, ob_ref = refs[6 + n_cast:8 + n_cast]
    cast_dst = refs[8 + n_cast:8 + 2 * n_cast]
    scratch = refs[8 + 2 * n_cast:]
    first = pl.program_id(0) == 0

    if stream_weights:
        wi_ref, wo_ref, acc_ref, sem = scratch

        @pl.when(first)
        def _():
            _load_weights_bf16(wi_in, wo_in, wi_ref, wo_ref, acc_ref, sem)
    else:
        (acc_ref,) = scratch
        wi_ref, wo_ref = wi_in, wo_in

    def load_x(rows):
        return jnp.where(first, xb_ref[rows, :], xa_ref[rows, :])

    def half(i):
        rows = slice(i * FFN_HALF, (i + 1) * FFN_HALF)

        def finish(x, acc):
            y = x + 0.5 * acc
            oa_ref[rows, :] = _rmsnorm(y, gf_ref[...]) if final_norm else y

        return _ffn_rows(load_x, g_ref, wi_ref, wo_ref, acc_ref, rows, finish)

    for _ in _staggered([half(i) for i in range(FFN_TILE // FFN_HALF)], 1):
        pass
    _cast_blocks(cast_src, cast_dst, cast_modes)

    @pl.when(first)
    def _():
        ob_ref[...] = oa_ref[...]


def _resident(shape):
    nd = len(shape)
    return pl.BlockSpec(shape, lambda *_: (0,) * nd, pipeline_mode=pl.Buffered(1))


def _ffn_call(xa, xb, g, wi, wo, gf, final_norm, name, cast_jobs=()):
    ta = xa.shape[0]
    n_a = ta // FFN_TILE
    assert ta % FFN_TILE == 0 and xb.shape[0] == FFN_TILE
    assert not cast_jobs or n_a == CAST_BLOCKS
    stream_weights = wi.dtype == F32
    tile_a = lambda i: (jnp.maximum(i - 1, 0), 0)
    cast_in, cast_out, cast_shapes = _cast_specs(cast_jobs, lambda i: jnp.maximum(i - 1, 0))
    if stream_weights:
        w_specs = [pl.BlockSpec(memory_space=pl.ANY)] * 2
        w_scratch = [pltpu.VMEM(wi.shape, BF), pltpu.VMEM(wo.shape, BF)]
        sems = [pltpu.SemaphoreType.DMA((FFN_TILE // WEIGHT_SLOT_ROWS,))]
    else:
        w_specs = [_resident(wi.shape), _resident(wo.shape)]
        w_scratch, sems = [], []
    return pl.pallas_call(
        functools.partial(_ffn_kernel, final_norm=final_norm, stream_weights=stream_weights,
                          cast_modes=tuple(j["transposed"] for j in cast_jobs)),
        grid=(1 + n_a,),
        in_specs=[
            pl.BlockSpec((FFN_TILE, D_MODEL), tile_a),
            _resident((FFN_TILE, D_MODEL)),
            _resident((1, D_MODEL)),
            *w_specs,
            _resident((1, D_MODEL)),
            *cast_in,
        ],
        out_specs=[
            pl.BlockSpec((FFN_TILE, D_MODEL), tile_a),
            pl.BlockSpec((FFN_TILE, D_MODEL), lambda i: (0, 0)),
            *cast_out,
        ],
        out_shape=[
            jax.ShapeDtypeStruct((ta, D_MODEL), F32),
            jax.ShapeDtypeStruct((FFN_TILE, D_MODEL), F32),
            *cast_shapes,
        ],
        scratch_shapes=w_scratch + [pltpu.VMEM((FFN_TILE, D_MODEL), F32)] + sems,
        compiler_params=pltpu.CompilerParams(
            dimension_semantics=("arbitrary",), vmem_limit_bytes=VMEM_LIMIT),
        name=name,
    )(xa, xb, g, wi, wo, gf, *[j["w"] for j in cast_jobs])


def _mix_block(cols, chunk, consts, get_state, put_state, emit):
    (wmix_ref, bias_ref, vnorm_ref, anorm_ref, wa2_ref, ba_ref, onorm_ref, cum_ref) = consts
    n_chunks = ROWS // chunk
    shift = int(np.log2(chunk))

    z = _dot(cols(OFF_LR, LANES).astype(BF), wa2_ref[...]) + ba_ref[...]
    la = _log_sigmoid(z) * (1.0 / B_GATE_NORMALIZER)
    la_hi = la.astype(BF)
    la_lo = (la - la_hi.astype(F32)).astype(BF)
    sums = _dot(cum_ref[...], jnp.concatenate([la_hi, la_lo], axis=0))
    b = sums[0:ROWS]
    b_ref = sums[ROWS:2 * ROWS]
    b_last = sums[2 * ROWS:3 * ROWS]
    yield

    ya = []
    va = []
    ssq = None
    for g in range(A_GROUPS):
        gs = slice(g * LANES, (g + 1) * LANES)
        u = _gelu_tanh(cols(OFF_U + g * LANES, LANES))
        v = _rmsnorm(_gelu_tanh(cols(OFF_V + g * LANES, LANES)), vnorm_ref[:, gs])
        yield
        zg = _dot(wmix_ref[g], v.astype(BF)) + bias_ref[g]
        yg = u * zg
        s = jnp.sum(yg * yg, axis=-1, keepdims=True)
        ssq = s if ssq is None else ssq + s
        ya.append(yg)
        va.append(v)
        yield
    inv = lax.rsqrt(ssq * (1.0 / A_WIDTH) + EPS)
    ya = [(ya[g] * inv) * anorm_ref[:, g * LANES:(g + 1) * LANES] for g in range(A_GROUPS)]
    yield
    gate = []
    for h in range(B_HEADS):
        gate.append(_silu(cols(OFF_R + h * B_DV, B_DV)))
        if h % 2 == 1:
            yield

    ri = lax.broadcasted_iota(jnp.int32, (ROWS, ROWS), 0)
    ci = lax.broadcasted_iota(jnp.int32, (ROWS, ROWS), 1)
    rchunk = lax.shift_right_logical(ri, shift)
    cchunk = lax.shift_right_logical(ci, shift)
    causal = (rchunk == cchunk) & (ci <= ri)
    lane = lax.broadcasted_iota(jnp.int32, (ROWS, LANES), 1)
    head0 = lane < B_DK

    yb = []
    for pair in range(B_HEADS // 2):
        ls = slice(pair * LANES, (pair + 1) * LANES)
        q = cols(OFF_Q + pair * LANES, LANES) * np.float32(B_DK ** -0.5)
        k = cols(OFF_K + pair * LANES, LANES)
        vv_bf = cols(OFF_VB + pair * 2 * B_DV, 2 * B_DV).astype(BF)
        bp, brp, blp = b[:, ls], b_ref[:, ls], b_last[:, ls]
        qs = q * jnp.exp(bp - brp)
        ks = (k * jnp.exp(brp - bp)).astype(BF)
        kl_t = (k * jnp.exp(blp - bp)).T
        bl_t = blp.T
        qb = (q * jnp.exp(bp)).astype(BF)
        yield

        o_heads = []
        for hh in range(2):
            msk = head0 if hh == 0 else jnp.logical_not(head0)
            qh = jnp.where(msk, qs, 0.0).astype(BF)
            sc = jnp.where(causal, _dot_nt(qh, ks), 0.0).astype(BF)
            o_heads.append(_dot(sc, vv_bf[:, hh * B_DV:(hh + 1) * B_DV]))
            yield
        o = jnp.concatenate(o_heads, axis=1)

        o_inter = []
        for j in range(n_chunks):
            s_prev = get_state(pair, j)
            o_inter.append(_dot(qb[j * chunk:(j + 1) * chunk], _blockdiag_bf16(*s_prev)))
            in_chunk = cchunk == j
            upd = _dot(jnp.where(in_chunk, kl_t, 0.0).astype(BF), vv_bf)
            dec = jnp.exp(bl_t[:, j * chunk:j * chunk + 1])
            put_state(pair, j, tuple(
                dec[hh * B_DK:(hh + 1) * B_DK] * s_prev[hh]
                + upd[hh * B_DK:(hh + 1) * B_DK, hh * B_DV:(hh + 1) * B_DV] for hh in range(2)))
        o = o + jnp.concatenate(o_inter, axis=0)
        yield

        for hh in range(2):
            h_idx = 2 * pair + hh
            hs = slice(h_idx * B_DV, (h_idx + 1) * B_DV)
            oh = _rmsnorm(o[:, hh * B_DV:(hh + 1) * B_DV], onorm_ref[:, hs])
            yb.append(oh * gate[h_idx])
        yield

    emit(jnp.concatenate(ya + yb, axis=1).astype(BF), jnp.concatenate(va, axis=1))


def _blockdiag_bf16(s0, s1):
    s0 = s0.astype(BF)
    s1 = s1.astype(BF)
    zero = jnp.zeros_like(s0)
    return jnp.concatenate(
        [jnp.concatenate([s0, zero], axis=1), jnp.concatenate([zero, s1], axis=1)], axis=0)


def _in_proj(x1_ref, g2_ref, wp_refs, p_scr, g, rows):
    wp_ref, wpt_ref = wp_refs
    h = _rmsnorm(x1_ref[rows, :], g2_ref[...]).astype(BF)
    for c0 in range(0, P_MAIN, MXU_DIM):
        cs = slice(c0, c0 + MXU_DIM)
        p_scr[g, :, cs] = _dot(h, wp_ref[:, cs])
    p_scr[g, :, P_MAIN:P_COLS] = _dot(h, wpt_ref[...])


def _out_proj(x1_ref, y_scr, wout_ref, x2_ref, g, rows):
    y = y_scr[g]
    for c0 in range(0, D_MODEL, MXU_DIM):
        cs = slice(c0, c0 + MXU_DIM)
        x2_ref[rows, cs] = x1_ref[rows, cs] + _dot(y, wout_ref[:, cs])


def _mixproj_tile(x1_ref, g2_ref, wp_refs, wout_ref, x2_ref, p_scr, y_scr, group, n_groups,
                  make_block):
    rows = [slice(g * group, (g + 1) * group) for g in range(n_groups)]
    for g in range(n_groups):
        _in_proj(x1_ref, g2_ref, wp_refs, p_scr, g, rows[g])
    blocks = [make_block(g, i) for g in range(n_groups) for i in range(group // ROWS)]
    for _ in _staggered(blocks, MIX_STAGGER):
        pass
    for g in range(n_groups):
        _out_proj(x1_ref, y_scr, wout_ref, x2_ref, g, rows[g])


def _block_cols(p_scr, g, i):
    return lambda off, width: p_scr[g, i * ROWS:(i + 1) * ROWS, off:off + width]


def _mix_setup(a_ws_ref, a_bs_ref, b_wa2_ref, wmix_scr, bias_scr, wa2_scr, cum_scr, seq_len, chunk):
    rr = lax.broadcasted_iota(jnp.int32, (ROWS, 2 * ROWS), 0)
    kk = lax.broadcasted_iota(jnp.int32, (ROWS, 2 * ROWS), 1) & (ROWS - 1)
    cshift = int(np.log2(chunk))
    same = lax.shift_right_logical(rr, cshift) == lax.shift_right_logical(kk, cshift)
    for i, keep in enumerate((same & (kk <= rr), same & ((kk & (chunk - 1)) <= chunk // 2), same)):
        cum_scr[i * ROWS:(i + 1) * ROWS, :] = jnp.where(keep, 1.0, 0.0).astype(BF)

    r = lax.broadcasted_iota(jnp.int32, (ROWS, ROWS), 0)
    c = lax.broadcasted_iota(jnp.int32, (ROWS, ROWS), 1)
    lane = lax.broadcasted_iota(jnp.int32, (8, LANES), 1)
    shift = int(np.log2(seq_len))
    for g in range(A_GROUPS):
        w = a_ws_ref[g]
        if seq_len == ROWS:
            wm = jnp.where(c <= r, w, 0.0).astype(BF)
        else:
            w8 = jnp.where((r < seq_len) & (c <= r), w, 0.0).astype(BF)
            sel = ((r & (seq_len - 1)) == c).astype(BF)
            tiled = _dot_nt(_dot(sel, w8).astype(BF), sel)
            same = lax.shift_right_logical(r, shift) == lax.shift_right_logical(c, shift)
            wm = jnp.where(same, tiled, 0.0).astype(BF)
        wmix_scr[g] = wm
        b = jnp.broadcast_to(a_bs_ref[g:g + 1, :], (8, LANES))
        if seq_len < ROWS:
            b = jnp.where(lane < seq_len, b, 0.0)
            k = seq_len
            while k < ROWS:
                b = b + pltpu.roll(b, k, axis=1)
                k *= 2
        bias_scr[g] = jnp.broadcast_to(b[0:1, :], (ROWS, LANES)).T
    wa2_scr[...] = jnp.zeros_like(wa2_scr)
    wa2_scr[0:B_LOWRANK, :] = b_wa2_ref[...].astype(BF)


def _mixproj_prompt_kernel(*refs, group, n_groups, cast_modes):
    n_cast = len(cast_modes)
    (x1_ref, g2_ref, wp_ref, wpt_ref, wout_ref, a_ws_ref, a_bs_ref, vnorm_ref, anorm_ref,
     b_wa2_ref, ba_ref, onorm_ref) = refs[:12]
    cast_src = refs[12:12 + n_cast]
    x2_ref, st_ref = refs[12 + n_cast:14 + n_cast]
    cast_dst = refs[14 + n_cast:14 + 2 * n_cast]
    p_scr, y_scr, wmix_scr, bias_scr, wa2_scr, cum_scr, s_scr = refs[14 + 2 * n_cast:]

    @pl.when((pl.program_id(0) == 0) & (pl.program_id(1) == 0))
    def _():
        _mix_setup(a_ws_ref, a_bs_ref, b_wa2_ref, wmix_scr, bias_scr, wa2_scr, cum_scr, A_CHUNK,
                   B_CHUNK)

    @pl.when(pl.program_id(1) == 0)
    def _():
        s_scr[...] = jnp.zeros_like(s_scr)

    consts = (wmix_scr, bias_scr, vnorm_ref, anorm_ref, wa2_scr, ba_ref, onorm_ref, cum_scr)
    cur = {pair: (s_scr[2 * pair], s_scr[2 * pair + 1]) for pair in range(B_HEADS // 2)}

    def get_state(pair, j):
        return cur[pair]

    def put_state(pair, j, s):
        cur[pair] = s

    def make_block(g, i):
        def emit(y, va):
            y_scr[g, i * ROWS:(i + 1) * ROWS, :] = y
        return _mix_block(_block_cols(p_scr, g, i), B_CHUNK, consts, get_state, put_state, emit)

    _mixproj_tile(x1_ref, g2_ref, (wp_ref, wpt_ref), wout_ref, x2_ref, p_scr, y_scr, group,
                  n_groups, make_block)
    _cast_blocks(cast_src, cast_dst, cast_modes)

    for pair in range(B_HEADS // 2):
        for hh in range(2):
            s_scr[2 * pair + hh] = cur[pair][hh]
            st_ref[0, 2 * pair + hh] = cur[pair][hh]


def _mixproj_sample_kernel(x1_ref, s0_ref, g2_ref, wp_ref, wpt_ref, wout_ref, a_ws_ref, a_bs_ref,
                           vnorm_ref, anorm_ref, b_wa2_ref, ba_ref, onorm_ref, x2_ref, va_ref,
                           st_ref, p_scr, y_scr, wmix_scr, bias_scr, wa2_scr, cum_scr, *, chunk,
                           group, n_groups):
    @pl.when(pl.program_id(0) == 0)
    def _():
        _mix_setup(a_ws_ref, a_bs_ref, b_wa2_ref, wmix_scr, bias_scr, wa2_scr, cum_scr, chunk, chunk)

    consts = (wmix_scr, bias_scr, vnorm_ref, anorm_ref, wa2_scr, ba_ref, onorm_ref, cum_scr)
    n_blk = group // ROWS
    per_blk = ROWS // chunk

    def make_block(g, i):
        blk = g * n_blk + i

        def get_state(pair, j):
            n = blk * per_blk + j
            return s0_ref[n, 2 * pair], s0_ref[n, 2 * pair + 1]

        def put_state(pair, j, s):
            n = blk * per_blk + j
            st_ref[n, 2 * pair] = s[0]
            st_ref[n, 2 * pair + 1] = s[1]

        def emit(y, va):
            y_scr[g, i * ROWS:(i + 1) * ROWS, :] = y
            va_ref[blk * ROWS:(blk + 1) * ROWS, :] = va

        return _mix_block(_block_cols(p_scr, g, i), chunk, consts, get_state, put_state, emit)

    _mixproj_tile(x1_ref, g2_ref, (wp_ref, wpt_ref), wout_ref, x2_ref, p_scr, y_scr, group,
                  n_groups, make_block)


def _mix_param_specs():
    return [
        _resident((A_GROUPS, A_CHUNK, A_CHUNK)),
        _resident((A_GROUPS, A_CHUNK)),
        _resident((1, A_WIDTH)),
        _resident((1, A_WIDTH)),
        _resident((B_LOWRANK, B_HEADS * B_DK)),
        _resident((1, B_HEADS * B_DK)),
        _resident((1, B_HEADS * B_DV)),
    ]


def _proj_specs():
    return [
        _resident((1, D_MODEL)),
        _resident((D_MODEL, P_MAIN)),
        _resident((D_MODEL, P_COLS - P_MAIN)),
        _resident((D_MODEL, D_MODEL)),
    ]


def _mixproj_scratch(group, n_groups):
    return [pltpu.VMEM((n_groups, group, P_COLS), F32),
            pltpu.VMEM((n_groups, group, D_MODEL), BF),
            pltpu.VMEM((A_GROUPS, ROWS, ROWS), BF),
            pltpu.VMEM((A_GROUPS, ROWS, LANES), F32),
            pltpu.VMEM((LANES, B_HEADS * B_DK), BF),
            pltpu.VMEM((3 * ROWS, 2 * ROWS), BF)]


def _mixproj_prompt_call(x1, g2, wp, wpt, wout, consts, batch, seq, tile, group, cast_jobs=()):
    n_tiles = seq // tile
    n_groups = tile // group
    assert not cast_jobs or batch * n_tiles == CAST_BLOCKS
    row = lambda b, s: (b * n_tiles + s, 0)
    cast_in, cast_out, cast_shapes = _cast_specs(cast_jobs, lambda b, s: b * n_tiles + s)
    return pl.pallas_call(
        functools.partial(_mixproj_prompt_kernel, group=group, n_groups=n_groups,
                          cast_modes=tuple(j["transposed"] for j in cast_jobs)),
        grid=(batch, n_tiles),
        in_specs=[pl.BlockSpec((tile, D_MODEL), row)] + _proj_specs() + _mix_param_specs()
        + cast_in,
        out_specs=[
            pl.BlockSpec((tile, D_MODEL), row),
            pl.BlockSpec((1, B_HEADS, B_DK, B_DV), lambda b, s: (b, 0, 0, 0)),
            *cast_out,
        ],
        out_shape=[
            jax.ShapeDtypeStruct((batch * seq, D_MODEL), F32),
            jax.ShapeDtypeStruct((batch, B_HEADS, B_DK, B_DV), F32),
            *cast_shapes,
        ],
        scratch_shapes=_mixproj_scratch(group, n_groups)
        + [pltpu.VMEM((B_HEADS, B_DK, B_DV), F32)],
        compiler_params=pltpu.CompilerParams(
            dimension_semantics=("arbitrary", "arbitrary"), vmem_limit_bytes=VMEM_LIMIT),
        name="mixproj_prompt",
    )(x1, g2, wp, wpt, wout, *consts, *[j["w"] for j in cast_jobs])


def _mixproj_sample_call(x1, s0, g2, wp, wpt, wout, consts, n_seq, seq, tile, group):
    seq_per_step = tile // seq
    n_groups = tile // group
    row = lambda i: (i, 0)
    state = lambda i: (i, 0, 0, 0)
    return pl.pallas_call(
        functools.partial(_mixproj_sample_kernel, chunk=seq, group=group, n_groups=n_groups),
        grid=(n_seq // seq_per_step,),
        in_specs=[
            pl.BlockSpec((tile, D_MODEL), row),
            pl.BlockSpec((seq_per_step, B_HEADS, B_DK, B_DV), state),
        ] + _proj_specs() + _mix_param_specs(),
        out_specs=[
            pl.BlockSpec((tile, D_MODEL), row),
            pl.BlockSpec((tile, A_WIDTH), row),
            pl.BlockSpec((seq_per_step, B_HEADS, B_DK, B_DV), state),
        ],
        out_shape=[
            jax.ShapeDtypeStruct((n_seq * seq, D_MODEL), F32),
            jax.ShapeDtypeStruct((n_seq * seq, A_WIDTH), F32),
            jax.ShapeDtypeStruct((n_seq, B_HEADS, B_DK, B_DV), F32),
        ],
        scratch_shapes=_mixproj_scratch(group, n_groups),
        compiler_params=pltpu.CompilerParams(
            dimension_semantics=("arbitrary",), vmem_limit_bytes=VMEM_LIMIT),
        name="mixproj_sample",
    )(x1, s0, g2, wp, wpt, wout, *consts)


def _mix_params(a_ws, a_bs, a_vnorm, a_onorm, b_wa2, b_ba, b_onorm):
    return (
        a_ws.reshape(A_GROUPS, A_CHUNK, A_CHUNK),
        a_bs.reshape(A_GROUPS, A_CHUNK),
        a_vnorm.reshape(1, A_WIDTH),
        a_onorm.reshape(1, A_WIDTH),
        b_wa2.reshape(B_LOWRANK, B_HEADS * B_DK),
        b_ba.reshape(1, B_HEADS * B_DK),
        b_onorm.reshape(1, B_HEADS * B_DV),
    )


def kernel(x_prompt, x_sample, state_gla, ffn1_norm, ffn1_w_in, ffn1_w_out, mix_norm, w_in,
           a_ws, a_bs, a_vnorm, a_onorm, b_wa2, b_ba, b_onorm, w_out, ffn2_norm, ffn2_w_in,
           ffn2_w_out, final_norm):
    batch, seq, _ = x_prompt.shape
    n_dec, dec_seq, _ = x_sample.shape
    assert ffn1_norm.shape[0] == 1, "single-layer kernel"
    assert seq % A_CHUNK == 0 and ROWS % dec_seq == 0 and dec_seq <= B_CHUNK

    g1 = ffn1_norm.reshape(1, D_MODEL)
    wi1 = ffn1_w_in.reshape(D_MODEL, 2 * D_FF)
    wo1 = ffn1_w_out.reshape(D_FF, D_MODEL)
    g2 = mix_norm.reshape(1, D_MODEL)
    w_in_t = jnp.swapaxes(w_in, 1, 2).reshape(IN_COLS, D_MODEL)
    w_out2d = w_out.reshape(D_MODEL, D_MODEL)
    g3 = ffn2_norm.reshape(1, D_MODEL)
    wi2 = ffn2_w_in.reshape(D_MODEL, 2 * D_FF)
    wo2 = ffn2_w_out.reshape(D_FF, D_MODEL)
    gf = final_norm.reshape(1, D_MODEL)
    mix_params = _mix_params(a_ws, a_bs, a_vnorm, a_onorm, b_wa2, b_ba, b_onorm)

    xp = x_prompt.reshape(batch * seq, D_MODEL)
    xs = x_sample.reshape(n_dec * dec_seq, D_MODEL)
    s0 = state_gla.reshape(state_gla.shape[1:])

    x1p, x1s, wp, wpt, wout = _ffn_call(
        xp, xs, g1, wi1, wo1, gf, False, "ffn1",
        cast_jobs=(_cast_transposed(w_in_t, 0, MXU_DIM, P_MAIN),
                   _cast_transposed(w_in_t, P_MAIN, P_COLS - P_MAIN, P_COLS - P_MAIN,
                                    valid_cols=IN_COLS - P_MAIN),
                   _cast_rows(w_out2d)))
    x2p, st_p, wi2_bf, wo2_bf = _mixproj_prompt_call(
        x1p, g2, wp, wpt, wout, mix_params, batch, seq, tile=1024, group=512,
        cast_jobs=(_cast_rows(wi2), _cast_rows(wo2)))
    x2s, va_s, st_s = _mixproj_sample_call(x1s, s0, g2, wp, wpt, wout, mix_params, n_dec, dec_seq,
                                           tile=256, group=128)
    out_p, out_s = _ffn_call(x2p, x2s, g3, wi2_bf, wo2_bf, gf, True, "ffn2")

    return (
        out_p.reshape(batch, seq, D_MODEL),
        out_s.reshape(n_dec, dec_seq, D_MODEL),
        st_p[None],
        st_s[None],
        va_s.reshape(1, n_dec, dec_seq, A_WIDTH),
    )
```

```python
import functools

import numpy as np
import jax
import jax.numpy as jnp
from jax import lax
from jax.experimental import pallas as pl
from jax.experimental.pallas import tpu as pltpu

D_MODEL = 1024
D_FF = 2816
A_WIDTH = 512
A_GROUPS = 4
A_CHUNK = 128
B_HEADS = 4
B_DK = 64
B_DV = 128
B_LOWRANK = 16
B_GATE_NORMALIZER = 16.0
B_CHUNK = 64
EPS = 1e-6

OFF_U, OFF_V, OFF_Q, OFF_K, OFF_VB, OFF_R, OFF_LR = 0, 512, 1024, 1280, 1536, 2048, 2560
IN_COLS = 2576

LANES = 128
P_COLS = -(-IN_COLS // LANES) * LANES
MXU_DIM = 256
ROWS = 128
FF_CHUNK = MXU_DIM
N_FF_CHUNKS = D_FF // FF_CHUNK
VMEM_LIMIT = 56 * 1024 * 1024

FFN_TILE = 1024
FFN_HALF = FFN_TILE // 2
STREAM_PARTS = 2
P_MAIN = (IN_COLS // LANES) * LANES
CAST_BLOCKS = 16
MIX_STAGGER = 10

BF = jnp.bfloat16
F32 = jnp.float32


def _dot(a, b):
    return jnp.dot(a, b, preferred_element_type=F32)


def _dot_nt(a, b):
    return lax.dot_general(a, b, (((1,), (1,)), ((), ())), preferred_element_type=F32)


def _rmsnorm(x, g):
    return (x * lax.rsqrt(jnp.mean(x * x, axis=-1, keepdims=True) + EPS)) * g


def _gelu_tanh(x):
    c = np.float32(np.sqrt(2.0 / np.pi))
    ca = np.float32(np.sqrt(2.0 / np.pi) * 0.044715)
    t = jnp.tanh(x * (c + ca * (x * x)))
    hx = 0.5 * x
    return hx + hx * t


def _silu(x):
    return x * (1.0 / (1.0 + jnp.exp(-x)))


def _log_sigmoid(x):
    return jnp.minimum(x, 0.0) - jnp.log(1.0 + jnp.exp(-jnp.abs(x)))


def _staggered(gens, stagger):
    live = [True] * len(gens)
    t = 0
    while any(live):
        for i, gen in enumerate(gens):
            if live[i] and t >= i * stagger:
                try:
                    next(gen)
                except StopIteration:
                    live[i] = False
        t += 1
        yield


def _cast_rows(w):
    rows, cols = w.shape
    blk = (rows // CAST_BLOCKS, cols)
    return dict(w=w, src_block=blk, src_index=lambda r: (r, 0), dst_block=blk,
                dst_index=lambda r: (r, 0), out=jax.ShapeDtypeStruct(w.shape, BF), transposed=None)


def _cast_transposed(w_t, row0, n_rows, out_cols, valid_cols=None):
    cols = w_t.shape[1]
    first, last = row0 // n_rows, (row0 + out_cols) // n_rows - 1
    return dict(w=w_t, src_block=(n_rows, cols), dst_block=(cols, n_rows),
                src_index=lambda r: (jnp.minimum(first + r, last), 0),
                dst_index=lambda r: (0, jnp.minimum(r, last - first)),
                out=jax.ShapeDtypeStruct((cols, out_cols), BF),
                transposed=n_rows if valid_cols is None else valid_cols)


def _cast_specs(jobs, row_block):
    in_specs = [pl.BlockSpec(j["src_block"], lambda *ids, j=j: j["src_index"](row_block(*ids)))
                for j in jobs]
    out_specs = [pl.BlockSpec(j["dst_block"], lambda *ids, j=j: j["dst_index"](row_block(*ids)))
                 for j in jobs]
    return in_specs, out_specs, [j["out"] for j in jobs]


def _cast_blocks(src_refs, dst_refs, transposed):
    for src, dst, valid in zip(src_refs, dst_refs, transposed):
        x = src[...]
        if valid is not None:
            x = x.T
            if valid < x.shape[1]:
                lane = lax.broadcasted_iota(jnp.int32, x.shape, 1)
                x = jnp.where(lane < valid, x, 0.0)
        dst[...] = x.astype(BF)


def _ffn_chunk(h, wi_ref, wo_ref, acc_ref, rows, j):
    cs = slice(j * FF_CHUNK, (j + 1) * FF_CHUNK)
    gate = _dot(h, wi_ref[:, cs])
    up = _dot(h, wi_ref[:, D_FF + j * FF_CHUNK:D_FF + (j + 1) * FF_CHUNK])
    act = (_silu(gate) * up).astype(BF)
    part = _dot(act, wo_ref[cs, :])
    if j == 0:
        acc_ref[rows, :] = part
    else:
        acc_ref[rows, :] += part


def _ffn_streamed_tile(x_ref, g_ref, wi_hbm, wo_hbm, wi_ref, wo_ref, acc_ref, stage, sem, finish):
    def parts(j):
        cols = (pl.ds(j * FF_CHUNK, FF_CHUNK), pl.ds(D_FF + j * FF_CHUNK, FF_CHUNK))
        out = []
        for buf, (w_hbm, n_rows, col) in zip(stage, ((wi_hbm, D_MODEL, cols[0]),
                                                      (wi_hbm, D_MODEL, cols[1]),
                                                      (wo_hbm, FF_CHUNK, None))):
            n = n_rows // STREAM_PARTS
            for p in range(STREAM_PARTS):
                rows = pl.ds(p * n, n)
                src = w_hbm.at[rows, col] if col is not None else \
                    w_hbm.at[pl.ds(j * FF_CHUNK + p * n, n), pl.ds(0, D_MODEL)]
                out.append((src, buf.at[rows, pl.ds(0, buf.shape[1])]))
        return out

    def copy(j, k):
        src, dst = parts(j)[k]
        return pltpu.make_async_copy(src, dst, sem.at[k])

    n_parts = 3 * STREAM_PARTS
    for k in range(n_parts):
        copy(0, k).start()
    halves = [slice(i * FFN_HALF, (i + 1) * FFN_HALF) for i in range(FFN_TILE // FFN_HALF)]
    h = [_rmsnorm(x_ref[rows, :], g_ref[...]).astype(BF) for rows in halves]
    for j in range(N_FF_CHUNKS):
        cs = slice(j * FF_CHUNK, (j + 1) * FF_CHUNK)
        for k in range(n_parts):
            copy(j, k).wait()
        wi_ref[:, cs] = stage[0][...].astype(BF)
        wi_ref[:, D_FF + j * FF_CHUNK:D_FF + (j + 1) * FF_CHUNK] = stage[1][...].astype(BF)
        wo_ref[cs, :] = stage[2][...].astype(BF)
        if j + 1 < N_FF_CHUNKS:
            for k in range(n_parts):
                copy(j + 1, k).start()
        for rows, hh in zip(halves, h):
            _ffn_chunk(hh, wi_ref, wo_ref, acc_ref, rows, j)
    for rows in halves:
        finish(rows, x_ref[rows, :], acc_ref[rows, :])


def _ffn_rows(x_ref, g_ref, wi_ref, wo_ref, acc_ref, rows, finish):
    h = _rmsnorm(x_ref[rows, :], g_ref[...]).astype(BF)
    yield
    for j in range(N_FF_CHUNKS):
        _ffn_chunk(h, wi_ref, wo_ref, acc_ref, rows, j)
        yield
    finish(rows, x_ref[rows, :], acc_ref[rows, :])


def _ffn_tile(x_ref, g_ref, wi_ref, wo_ref, acc_ref, finish):
    halves = [_ffn_rows(x_ref, g_ref, wi_ref, wo_ref, acc_ref,
                        slice(i * FFN_HALF, (i + 1) * FFN_HALF), finish)
              for i in range(FFN_TILE // FFN_HALF)]
    for _ in _staggered(halves, 1):
        pass


def _ffn_kernel(*refs, final_norm, stream_weights, cast_modes):
    n_cast = len(cast_modes)
    xa_ref, xb_ref, g_ref, wi_in, wo_in, gf_ref = refs[:6]
    cast_src = refs[6:6 + n_cast]
    oa_ref, ob_ref = refs[6 + n_cast:8 + n_cast]
    cast_dst = refs[8 + n_cast:8 + 2 * n_cast]
    scratch = refs[8 + 2 * n_cast:]
    first = pl.program_id(0) == 0

    def finish_into(o_ref):
        def finish(rows, x, acc):
            y = x + 0.5 * acc
            o_ref[rows, :] = _rmsnorm(y, gf_ref[...]) if final_norm else y
        return finish

    if stream_weights:
        wi_ref, wo_ref, acc_ref, *stage, sem = scratch

        @pl.when(first)
        def _():
            _ffn_streamed_tile(xb_ref, g_ref, wi_in, wo_in, wi_ref, wo_ref, acc_ref, stage, sem,
                               finish_into(ob_ref))
    else:
        (acc_ref,) = scratch
        wi_ref, wo_ref = wi_in, wo_in

        @pl.when(first)
        def _():
            _ffn_tile(xb_ref, g_ref, wi_ref, wo_ref, acc_ref, finish_into(ob_ref))

    @pl.when(jnp.logical_not(first))
    def _():
        _ffn_tile(xa_ref, g_ref, wi_ref, wo_ref, acc_ref, finish_into(oa_ref))

    _cast_blocks(cast_src, cast_dst, cast_modes)


def _resident(shape):
    nd = len(shape)
    return pl.BlockSpec(shape, lambda *_: (0,) * nd, pipeline_mode=pl.Buffered(1))


def _ffn_call(xa, xb, g, wi, wo, gf, final_norm, name, cast_jobs=()):
    ta = xa.shape[0]
    n_a = ta // FFN_TILE
    assert ta % FFN_TILE == 0 and xb.shape[0] == FFN_TILE
    assert not cast_jobs or n_a == CAST_BLOCKS
    stream_weights = wi.dtype == F32
    tile_a = lambda i: (jnp.maximum(i - 1, 0), 0)
    cast_in, cast_out, cast_shapes = _cast_specs(cast_jobs, lambda i: jnp.maximum(i - 1, 0))
    if stream_weights:
        w_specs = [pl.BlockSpec(memory_space=pl.ANY)] * 2
        w_scratch = [pltpu.VMEM(wi.shape, BF), pltpu.VMEM(wo.shape, BF)]
        stage = [pltpu.VMEM((D_MODEL, FF_CHUNK), F32), pltpu.VMEM((D_MODEL, FF_CHUNK), F32),
                 pltpu.VMEM((FF_CHUNK, D_MODEL), F32)]
        sems = stage + [pltpu.SemaphoreType.DMA((3 * STREAM_PARTS,))]
    else:
        w_specs = [_resident(wi.shape), _resident(wo.shape)]
        w_scratch, sems = [], []
    return pl.pallas_call(
        functools.partial(_ffn_kernel, final_norm=final_norm, stream_weights=stream_weights,
                          cast_modes=tuple(j["transposed"] for j in cast_jobs)),
        grid=(1 + n_a,),
        in_specs=[
            pl.BlockSpec((FFN_TILE, D_MODEL), tile_a),
            _resident((FFN_TILE, D_MODEL)),
            _resident((1, D_MODEL)),
            *w_specs,
            _resident((1, D_MODEL)),
            *cast_in,
        ],
        out_specs=[
            pl.BlockSpec((FFN_TILE, D_MODEL), tile_a),
            pl.BlockSpec((FFN_TILE, D_MODEL), lambda i: (0, 0)),
            *cast_out,
        ],
        out_shape=[
            jax.ShapeDtypeStruct((ta, D_MODEL), F32),
            jax.ShapeDtypeStruct((FFN_TILE, D_MODEL), F32),
            *cast_shapes,
        ],
        scratch_shapes=w_scratch + [pltpu.VMEM((FFN_TILE, D_MODEL), F32)] + sems,
        compiler_params=pltpu.CompilerParams(
            dimension_semantics=("arbitrary",), vmem_limit_bytes=VMEM_LIMIT),
        name=name,
    )(xa, xb, g, wi, wo, gf, *[j["w"] for j in cast_jobs])


def _mix_block(cols, chunk, consts, get_state, put_state, emit):
    (wmix_ref, bias_ref, vnorm_ref, anorm_ref, wa2_ref, ba_ref, onorm_ref, cum_ref) = consts
    n_chunks = ROWS // chunk
    shift = int(np.log2(chunk))

    z = _dot(cols(OFF_LR, LANES).astype(BF), wa2_ref[...]) + ba_ref[...]
    la = _log_sigmoid(z) * (1.0 / B_GATE_NORMALIZER)
    la_hi = la.astype(BF)
    la_lo = (la - la_hi.astype(F32)).astype(BF)
    sums = _dot(cum_ref[...], jnp.concatenate([la_hi, la_lo], axis=0))
    b = sums[0:ROWS]
    b_ref = sums[ROWS:2 * ROWS]
    b_last = sums[2 * ROWS:3 * ROWS]
    yield

    ya = []
    va = []
    ssq = None
    for g in range(A_GROUPS):
        gs = slice(g * LANES, (g + 1) * LANES)
        u = _gelu_tanh(cols(OFF_U + g * LANES, LANES))
        v = _rmsnorm(_gelu_tanh(cols(OFF_V + g * LANES, LANES)), vnorm_ref[:, gs])
        yield
        zg = _dot(wmix_ref[g], v.astype(BF)) + bias_ref[g]
        yg = u * zg
        s = jnp.sum(yg * yg, axis=-1, keepdims=True)
        ssq = s if ssq is None else ssq + s
        ya.append(yg)
        va.append(v)
        yield
    inv = lax.rsqrt(ssq * (1.0 / A_WIDTH) + EPS)
    ya = [(ya[g] * inv) * anorm_ref[:, g * LANES:(g + 1) * LANES] for g in range(A_GROUPS)]
    yield
    gate = []
    for h in range(B_HEADS):
        gate.append(_silu(cols(OFF_R + h * B_DV, B_DV)))
        if h % 2 == 1:
            yield

    ri = lax.broadcasted_iota(jnp.int32, (ROWS, ROWS), 0)
    ci = lax.broadcasted_iota(jnp.int32, (ROWS, ROWS), 1)
    rchunk = lax.shift_right_logical(ri, shift)
    cchunk = lax.shift_right_logical(ci, shift)
    causal = (rchunk == cchunk) & (ci <= ri)
    lane = lax.broadcasted_iota(jnp.int32, (ROWS, LANES), 1)
    head0 = lane < B_DK

    yb = []
    for pair in range(B_HEADS // 2):
        ls = slice(pair * LANES, (pair + 1) * LANES)
        q = cols(OFF_Q + pair * LANES, LANES) * np.float32(B_DK ** -0.5)
        k = cols(OFF_K + pair * LANES, LANES)
        vv_bf = cols(OFF_VB + pair * 2 * B_DV, 2 * B_DV).astype(BF)
        bp, brp, blp = b[:, ls], b_ref[:, ls], b_last[:, ls]
        qs = q * jnp.exp(bp - brp)
        ks = (k * jnp.exp(brp - bp)).astype(BF)
        kl_t = (k * jnp.exp(blp - bp)).T
        bl_t = blp.T
        qb = (q * jnp.exp(bp)).astype(BF)
        yield

        o_heads = []
        for hh in range(2):
            msk = head0 if hh == 0 else jnp.logical_not(head0)
            qh = jnp.where(msk, qs, 0.0).astype(BF)
            sc = jnp.where(causal, _dot_nt(qh, ks), 0.0).astype(BF)
            o_heads.append(_dot(sc, vv_bf[:, hh * B_DV:(hh + 1) * B_DV]))
            yield
        o = jnp.concatenate(o_heads, axis=1)

        o_inter = []
        for j in range(n_chunks):
            s_prev = get_state(pair, j)
            o_inter.append(_dot(qb[j * chunk:(j + 1) * chunk], _blockdiag_bf16(*s_prev)))
            in_chunk = cchunk == j
            upd = _dot(jnp.where(in_chunk, kl_t, 0.0).astype(BF), vv_bf)
            dec = jnp.exp(bl_t[:, j * chunk:j * chunk + 1])
            put_state(pair, j, tuple(
                dec[hh * B_DK:(hh + 1) * B_DK] * s_prev[hh]
                + upd[hh * B_DK:(hh + 1) * B_DK, hh * B_DV:(hh + 1) * B_DV] for hh in range(2)))
        o = o + jnp.concatenate(o_inter, axis=0)
        yield

        for hh in range(2):
            h_idx = 2 * pair + hh
            hs = slice(h_idx * B_DV, (h_idx + 1) * B_DV)
            oh = _rmsnorm(o[:, hh * B_DV:(hh + 1) * B_DV], onorm_ref[:, hs])
            yb.append(oh * gate[h_idx])
        yield

    emit(jnp.concatenate(ya + yb, axis=1).astype(BF), jnp.concatenate(va, axis=1))


def _blockdiag_bf16(s0, s1):
    s0 = s0.astype(BF)
    s1 = s1.astype(BF)
    zero = jnp.zeros_like(s0)
    return jnp.concatenate(
        [jnp.concatenate([s0, zero], axis=1), jnp.concatenate([zero, s1], axis=1)], axis=0)


def _in_proj(x1_ref, g2_ref, wp_refs, p_scr, g, rows):
    wp_ref, wpt_ref = wp_refs
    h = _rmsnorm(x1_ref[rows, :], g2_ref[...]).astype(BF)
    for c0 in range(0, P_MAIN, MXU_DIM):
        cs = slice(c0, c0 + MXU_DIM)
        p_scr[g, :, cs] = _dot(h, wp_ref[:, cs])
    p_scr[g, :, P_MAIN:P_COLS] = _dot(h, wpt_ref[...])


def _out_proj(x1_ref, y_scr, wout_ref, x2_ref, g, rows):
    y = y_scr[g]
    for c0 in range(0, D_MODEL, MXU_DIM):
        cs = slice(c0, c0 + MXU_DIM)
        x2_ref[rows, cs] = x1_ref[rows, cs] + _dot(y, wout_ref[:, cs])


def _mixproj_tile(x1_ref, g2_ref, wp_refs, wout_ref, x2_ref, p_scr, y_scr, group, n_groups,
                  make_block):
    rows = [slice(g * group, (g + 1) * group) for g in range(n_groups)]
    for g in range(n_groups):
        _in_proj(x1_ref, g2_ref, wp_refs, p_scr, g, rows[g])
    blocks = [make_block(g, i) for g in range(n_groups) for i in range(group // ROWS)]
    for _ in _staggered(blocks, MIX_STAGGER):
        pass
    for g in range(n_groups):
        _out_proj(x1_ref, y_scr, wout_ref, x2_ref, g, rows[g])


def _block_cols(p_scr, g, i):
    return lambda off, width: p_scr[g, i * ROWS:(i + 1) * ROWS, off:off + width]


def _mix_setup(a_ws_ref, a_bs_ref, b_wa2_ref, wmix_scr, bias_scr, wa2_scr, cum_scr, seq_len, chunk):
    rr = lax.broadcasted_iota(jnp.int32, (ROWS, 2 * ROWS), 0)
    kk = lax.broadcasted_iota(jnp.int32, (ROWS, 2 * ROWS), 1) & (ROWS - 1)
    cshift = int(np.log2(chunk))
    same = lax.shift_right_logical(rr, cshift) == lax.shift_right_logical(kk, cshift)
    for i, keep in enumerate((same & (kk <= rr), same & ((kk & (chunk - 1)) <= chunk // 2), same)):
        cum_scr[i * ROWS:(i + 1) * ROWS, :] = jnp.where(keep, 1.0, 0.0).astype(BF)

    r = lax.broadcasted_iota(jnp.int32, (ROWS, ROWS), 0)
    c = lax.broadcasted_iota(jnp.int32, (ROWS, ROWS), 1)
    lane = lax.broadcasted_iota(jnp.int32, (8, LANES), 1)
    shift = int(np.log2(seq_len))
    for g in range(A_GROUPS):
        w = a_ws_ref[g]
        if seq_len == ROWS:
            wm = jnp.where(c <= r, w, 0.0).astype(BF)
        else:
            w8 = jnp.where((r < seq_len) & (c <= r), w, 0.0).astype(BF)
            sel = ((r & (seq_len - 1)) == c).astype(BF)
            tiled = _dot_nt(_dot(sel, w8).astype(BF), sel)
            same = lax.shift_right_logical(r, shift) == lax.shift_right_logical(c, shift)
            wm = jnp.where(same, tiled, 0.0).astype(BF)
        wmix_scr[g] = wm
        b = jnp.broadcast_to(a_bs_ref[g:g + 1, :], (8, LANES))
        if seq_len < ROWS:
            b = jnp.where(lane < seq_len, b, 0.0)
            k = seq_len
            while k < ROWS:
                b = b + pltpu.roll(b, k, axis=1)
                k *= 2
        bias_scr[g] = jnp.broadcast_to(b[0:1, :], (ROWS, LANES)).T
    wa2_scr[...] = jnp.zeros_like(wa2_scr)
    wa2_scr[0:B_LOWRANK, :] = b_wa2_ref[...].astype(BF)


def _mixproj_prompt_kernel(*refs, group, n_groups, cast_modes):
    n_cast = len(cast_modes)
    (x1_ref, g2_ref, wp_ref, wpt_ref, wout_ref, a_ws_ref, a_bs_ref, vnorm_ref, anorm_ref,
     b_wa2_ref, ba_ref, onorm_ref) = refs[:12]
    cast_src = refs[12:12 + n_cast]
    x2_ref, st_ref = refs[12 + n_cast:14 + n_cast]
    cast_dst = refs[14 + n_cast:14 + 2 * n_cast]
    p_scr, y_scr, wmix_scr, bias_scr, wa2_scr, cum_scr, s_scr = refs[14 + 2 * n_cast:]

    @pl.when((pl.program_id(0) == 0) & (pl.program_id(1) == 0))
    def _():
        _mix_setup(a_ws_ref, a_bs_ref, b_wa2_ref, wmix_scr, bias_scr, wa2_scr, cum_scr, A_CHUNK,
                   B_CHUNK)

    @pl.when(pl.program_id(1) == 0)
    def _():
        s_scr[...] = jnp.zeros_like(s_scr)

    consts = (wmix_scr, bias_scr, vnorm_ref, anorm_ref, wa2_scr, ba_ref, onorm_ref, cum_scr)
    cur = {pair: (s_scr[2 * pair], s_scr[2 * pair + 1]) for pair in range(B_HEADS // 2)}

    def get_state(pair, j):
        return cur[pair]

    def put_state(pair, j, s):
        cur[pair] = s

    def make_block(g, i):
        def emit(y, va):
            y_scr[g, i * ROWS:(i + 1) * ROWS, :] = y
        return _mix_block(_block_cols(p_scr, g, i), B_CHUNK, consts, get_state, put_state, emit)

    _mixproj_tile(x1_ref, g2_ref, (wp_ref, wpt_ref), wout_ref, x2_ref, p_scr, y_scr, group,
                  n_groups, make_block)
    _cast_blocks(cast_src, cast_dst, cast_modes)

    for pair in range(B_HEADS // 2):
        for hh in range(2):
            s_scr[2 * pair + hh] = cur[pair][hh]
            st_ref[0, 2 * pair + hh] = cur[pair][hh]


def _mixproj_sample_kernel(x1_ref, s0_ref, g2_ref, wp_ref, wpt_ref, wout_ref, a_ws_ref, a_bs_ref,
                           vnorm_ref, anorm_ref, b_wa2_ref, ba_ref, onorm_ref, x2_ref, va_ref,
                           st_ref, p_scr, y_scr, wmix_scr, bias_scr, wa2_scr, cum_scr, *, chunk,
                           group, n_groups):
    @pl.when(pl.program_id(0) == 0)
    def _():
        _mix_setup(a_ws_ref, a_bs_ref, b_wa2_ref, wmix_scr, bias_scr, wa2_scr, cum_scr, chunk, chunk)

    consts = (wmix_scr, bias_scr, vnorm_ref, anorm_ref, wa2_scr, ba_ref, onorm_ref, cum_scr)
    n_blk = group // ROWS
    per_blk = ROWS // chunk

    def make_block(g, i):
        blk = g * n_blk + i

        def get_state(pair, j):
            n = blk * per_blk + j
            return s0_ref[n, 2 * pair], s0_ref[n, 2 * pair + 1]

        def put_state(pair, j, s):
            n = blk * per_blk + j
            st_ref[n, 2 * pair] = s[0]
            st_ref[n, 2 * pair + 1] = s[1]

        def emit(y, va):
            y_scr[g, i * ROWS:(i + 1) * ROWS, :] = y
            va_ref[blk * ROWS:(blk + 1) * ROWS, :] = va

        return _mix_block(_block_cols(p_scr, g, i), chunk, consts, get_state, put_state, emit)

    _mixproj_tile(x1_ref, g2_ref, (wp_ref, wpt_ref), wout_ref, x2_ref, p_scr, y_scr, group,
                  n_groups, make_block)


def _mix_param_specs():
    return [
        _resident((A_GROUPS, A_CHUNK, A_CHUNK)),
        _resident((A_GROUPS, A_CHUNK)),
        _resident((1, A_WIDTH)),
        _resident((1, A_WIDTH)),
        _resident((B_LOWRANK, B_HEADS * B_DK)),
        _resident((1, B_HEADS * B_DK)),
        _resident((1, B_HEADS * B_DV)),
    ]


def _proj_specs():
    return [
        _resident((1, D_MODEL)),
        _resident((D_MODEL, P_MAIN)),
        _resident((D_MODEL, P_COLS - P_MAIN)),
        _resident((D_MODEL, D_MODEL)),
    ]


def _mixproj_scratch(group, n_groups):
    return [pltpu.VMEM((n_groups, group, P_COLS), F32),
            pltpu.VMEM((n_groups, group, D_MODEL), BF),
            pltpu.VMEM((A_GROUPS, ROWS, ROWS), BF),
            pltpu.VMEM((A_GROUPS, ROWS, LANES), F32),
            pltpu.VMEM((LANES, B_HEADS * B_DK), BF),
            pltpu.VMEM((3 * ROWS, 2 * ROWS), BF)]


def _mixproj_prompt_call(x1, g2, wp, wpt, wout, consts, batch, seq, tile, group, cast_jobs=()):
    n_tiles = seq // tile
    n_groups = tile // group
    assert not cast_jobs or batch * n_tiles == CAST_BLOCKS
    row = lambda b, s: (b * n_tiles + s, 0)
    cast_in, cast_out, cast_shapes = _cast_specs(cast_jobs, lambda b, s: b * n_tiles + s)
    return pl.pallas_call(
        functools.partial(_mixproj_prompt_kernel, group=group, n_groups=n_groups,
                          cast_modes=tuple(j["transposed"] for j in cast_jobs)),
        grid=(batch, n_tiles),
        in_specs=[pl.BlockSpec((tile, D_MODEL), row)] + _proj_specs() + _mix_param_specs()
        + cast_in,
        out_specs=[
            pl.BlockSpec((tile, D_MODEL), row),
            pl.BlockSpec((1, B_HEADS, B_DK, B_DV), lambda b, s: (b, 0, 0, 0)),
            *cast_out,
        ],
        out_shape=[
            jax.ShapeDtypeStruct((batch * seq, D_MODEL), F32),
            jax.ShapeDtypeStruct((batch, B_HEADS, B_DK, B_DV), F32),
            *cast_shapes,
        ],
        scratch_shapes=_mixproj_scratch(group, n_groups)
        + [pltpu.VMEM((B_HEADS, B_DK, B_DV), F32)],
        compiler_params=pltpu.CompilerParams(
            dimension_semantics=("arbitrary", "arbitrary"), vmem_limit_bytes=VMEM_LIMIT),
        name="mixproj_prompt",
    )(x1, g2, wp, wpt, wout, *consts, *[j["w"] for j in cast_jobs])


def _mixproj_sample_call(x1, s0, g2, wp, wpt, wout, consts, n_seq, seq, tile, group):
    seq_per_step = tile // seq
    n_groups = tile // group
    row = lambda i: (i, 0)
    state = lambda i: (i, 0, 0, 0)
    return pl.pallas_call(
        functools.partial(_mixproj_sample_kernel, chunk=seq, group=group, n_groups=n_groups),
        grid=(n_seq // seq_per_step,),
        in_specs=[
            pl.BlockSpec((tile, D_MODEL), row),
            pl.BlockSpec((seq_per_step, B_HEADS, B_DK, B_DV), state),
        ] + _proj_specs() + _mix_param_specs(),
        out_specs=[
            pl.BlockSpec((tile, D_MODEL), row),
            pl.BlockSpec((tile, A_WIDTH), row),
            pl.BlockSpec((seq_per_step, B_HEADS, B_DK, B_DV), state),
        ],
        out_shape=[
            jax.ShapeDtypeStruct((n_seq * seq, D_MODEL), F32),
            jax.ShapeDtypeStruct((n_seq * seq, A_WIDTH), F32),
            jax.ShapeDtypeStruct((n_seq, B_HEADS, B_DK, B_DV), F32),
        ],
        scratch_shapes=_mixproj_scratch(group, n_groups),
        compiler_params=pltpu.CompilerParams(
            dimension_semantics=("arbitrary",), vmem_limit_bytes=VMEM_LIMIT),
        name="mixproj_sample",
    )(x1, s0, g2, wp, wpt, wout, *consts)


def _mix_params(a_ws, a_bs, a_vnorm, a_onorm, b_wa2, b_ba, b_onorm):
    return (
        a_ws.reshape(A_GROUPS, A_CHUNK, A_CHUNK),
        a_bs.reshape(A_GROUPS, A_CHUNK),
        a_vnorm.reshape(1, A_WIDTH),
        a_onorm.reshape(1, A_WIDTH),
        b_wa2.reshape(B_LOWRANK, B_HEADS * B_DK),
        b_ba.reshape(1, B_HEADS * B_DK),
        b_onorm.reshape(1, B_HEADS * B_DV),
    )


def kernel(x_prompt, x_sample, state_gla, ffn1_norm, ffn1_w_in, ffn1_w_out, mix_norm, w_in,
           a_ws, a_bs, a_vnorm, a_onorm, b_wa2, b_ba, b_onorm, w_out, ffn2_norm, ffn2_w_in,
           ffn2_w_out, final_norm):
    batch, seq, _ = x_prompt.shape
    n_dec, dec_seq, _ = x_sample.shape
    assert ffn1_norm.shape[0] == 1, "single-layer kernel"
    assert seq % A_CHUNK == 0 and ROWS % dec_seq == 0 and dec_seq <= B_CHUNK

    g1 = ffn1_norm.reshape(1, D_MODEL)
    wi1 = ffn1_w_in.reshape(D_MODEL, 2 * D_FF)
    wo1 = ffn1_w_out.reshape(D_FF, D_MODEL)
    g2 = mix_norm.reshape(1, D_MODEL)
    w_in_t = jnp.swapaxes(w_in, 1, 2).reshape(IN_COLS, D_MODEL)
    w_out2d = w_out.reshape(D_MODEL, D_MODEL)
    g3 = ffn2_norm.reshape(1, D_MODEL)
    wi2 = ffn2_w_in.reshape(D_MODEL, 2 * D_FF)
    wo2 = ffn2_w_out.reshape(D_FF, D_MODEL)
    gf = final_norm.reshape(1, D_MODEL)
    mix_params = _mix_params(a_ws, a_bs, a_vnorm, a_onorm, b_wa2, b_ba, b_onorm)

    xp = x_prompt.reshape(batch * seq, D_MODEL)
    xs = x_sample.reshape(n_dec * dec_seq, D_MODEL)
    s0 = state_gla.reshape(state_gla.shape[1:])

    x1p, x1s, wp, wpt, wout = _ffn_call(
        xp, xs, g1, wi1, wo1, gf, False, "ffn1",
        cast_jobs=(_cast_transposed(w_in_t, 0, MXU_DIM, P_MAIN),
                   _cast_transposed(w_in_t, P_MAIN, P_COLS - P_MAIN, P_COLS - P_MAIN,
                                    valid_cols=IN_COLS - P_MAIN),
                   _cast_rows(w_out2d)))
    x2p, st_p, wi2_bf, wo2_bf = _mixproj_prompt_call(
        x1p, g2, wp, wpt, wout, mix_params, batch, seq, tile=1024, group=512,
        cast_jobs=(_cast_rows(wi2), _cast_rows(wo2)))
    x2s, va_s, st_s = _mixproj_sample_call(x1s, s0, g2, wp, wpt, wout, mix_params, n_dec, dec_seq,
                                           tile=256, group=128)
    out_p, out_s = _ffn_call(x2p, x2s, g3, wi2_bf, wo2_bf, gf, True, "ffn2")

    return (
        out_p.reshape(batch, seq, D_MODEL),
        out_s.reshape(n_dec, dec_seq, D_MODEL),
        st_p[None],
        st_s[None],
        va_s.reshape(1, n_dec, dec_seq, A_WIDTH),
    )
```

```python
import functools

import numpy as np
import jax
import jax.numpy as jnp
from jax import lax
from jax.experimental import pallas as pl
from jax.experimental.pallas import tpu as pltpu

D_MODEL = 1024
D_FF = 2816
A_WIDTH = 512
A_GROUPS = 4
A_CHUNK = 128
B_HEADS = 4
B_DK = 64
B_DV = 128
B_LOWRANK = 16
B_GATE_NORMALIZER = 16.0
B_CHUNK = 64
EPS = 1e-6

OFF_U, OFF_V, OFF_Q, OFF_K, OFF_VB, OFF_R, OFF_LR = 0, 512, 1024, 1280, 1536, 2048, 2560
IN_COLS = 2576

LANES = 128
P_COLS = -(-IN_COLS // LANES) * LANES
MXU_DIM = 256
ROWS = 128
FF_CHUNK = MXU_DIM
N_FF_CHUNKS = D_FF // FF_CHUNK
VMEM_LIMIT = 56 * 1024 * 1024

FFN_TILE = 1024
FFN_HALF = FFN_TILE // 2
STREAM_PARTS = 4
P_MAIN = (IN_COLS // LANES) * LANES
CAST_BLOCKS = 16
MIX_STAGGER = 10

BF = jnp.bfloat16
F32 = jnp.float32


def _dot(a, b):
    return jnp.dot(a, b, preferred_element_type=F32)


def _dot_nt(a, b):
    return lax.dot_general(a, b, (((1,), (1,)), ((), ())), preferred_element_type=F32)


def _rmsnorm(x, g):
    return (x * lax.rsqrt(jnp.mean(x * x, axis=-1, keepdims=True) + EPS)) * g


def _gelu_tanh(x):
    c = np.float32(np.sqrt(2.0 / np.pi))
    ca = np.float32(np.sqrt(2.0 / np.pi) * 0.044715)
    t = jnp.tanh(x * (c + ca * (x * x)))
    hx = 0.5 * x
    return hx + hx * t


def _silu(x):
    return x * (1.0 / (1.0 + jnp.exp(-x)))


def _log_sigmoid(x):
    return jnp.minimum(x, 0.0) - jnp.log(1.0 + jnp.exp(-jnp.abs(x)))


def _staggered(gens, stagger):
    live = [True] * len(gens)
    t = 0
    while any(live):
        for i, gen in enumerate(gens):
            if live[i] and t >= i * stagger:
                try:
                    next(gen)
                except StopIteration:
                    live[i] = False
        t += 1
        yield


def _cast_rows(w):
    rows, cols = w.shape
    blk = (rows // CAST_BLOCKS, cols)
    return dict(w=w, src_block=blk, src_index=lambda r: (r, 0), dst_block=blk,
                dst_index=lambda r: (r, 0), out=jax.ShapeDtypeStruct(w.shape, BF), transposed=None)


def _cast_transposed(w_t, row0, n_rows, out_cols, valid_cols=None):
    cols = w_t.shape[1]
    first, last = row0 // n_rows, (row0 + out_cols) // n_rows - 1
    return dict(w=w_t, src_block=(n_rows, cols), dst_block=(cols, n_rows),
                src_index=lambda r: (jnp.minimum(first + r, last), 0),
                dst_index=lambda r: (0, jnp.minimum(r, last - first)),
                out=jax.ShapeDtypeStruct((cols, out_cols), BF),
                transposed=n_rows if valid_cols is None else valid_cols)


def _cast_specs(jobs, row_block):
    in_specs = [pl.BlockSpec(j["src_block"], lambda *ids, j=j: j["src_index"](row_block(*ids)))
                for j in jobs]
    out_specs = [pl.BlockSpec(j["dst_block"], lambda *ids, j=j: j["dst_index"](row_block(*ids)))
                 for j in jobs]
    return in_specs, out_specs, [j["out"] for j in jobs]


def _cast_blocks(src_refs, dst_refs, transposed):
    for src, dst, valid in zip(src_refs, dst_refs, transposed):
        x = src[...]
        if valid is not None:
            x = x.T
            if valid < x.shape[1]:
                lane = lax.broadcasted_iota(jnp.int32, x.shape, 1)
                x = jnp.where(lane < valid, x, 0.0)
        dst[...] = x.astype(BF)


def _ffn_chunk(h, wi_ref, wo_ref, acc_ref, rows, j):
    cs = slice(j * FF_CHUNK, (j + 1) * FF_CHUNK)
    gate = _dot(h, wi_ref[:, cs])
    up = _dot(h, wi_ref[:, D_FF + j * FF_CHUNK:D_FF + (j + 1) * FF_CHUNK])
    act = (_silu(gate) * up).astype(BF)
    part = _dot(act, wo_ref[cs, :])
    if j == 0:
        acc_ref[rows, :] = part
    else:
        acc_ref[rows, :] += part


def _ffn_streamed_tile(x_ref, g_ref, wi_hbm, wo_hbm, wi_ref, wo_ref, acc_ref, stage, sem, finish):
    def parts(j):
        cols = (pl.ds(j * FF_CHUNK, FF_CHUNK), pl.ds(D_FF + j * FF_CHUNK, FF_CHUNK))
        out = []
        for buf, (w_hbm, n_rows, col) in zip(stage, ((wi_hbm, D_MODEL, cols[0]),
                                                      (wi_hbm, D_MODEL, cols[1]),
                                                      (wo_hbm, FF_CHUNK, None))):
            n = n_rows // STREAM_PARTS
            for p in range(STREAM_PARTS):
                rows = pl.ds(p * n, n)
                src = w_hbm.at[rows, col] if col is not None else \
                    w_hbm.at[pl.ds(j * FF_CHUNK + p * n, n), pl.ds(0, D_MODEL)]
                out.append((src, buf.at[rows, pl.ds(0, buf.shape[1])]))
        return out

    def copy(j, k):
        src, dst = parts(j)[k]
        return pltpu.make_async_copy(src, dst, sem.at[k])

    n_parts = 3 * STREAM_PARTS
    for k in range(n_parts):
        copy(0, k).start()
    halves = [slice(i * FFN_HALF, (i + 1) * FFN_HALF) for i in range(FFN_TILE // FFN_HALF)]
    h = [_rmsnorm(x_ref[rows, :], g_ref[...]).astype(BF) for rows in halves]
    for j in range(N_FF_CHUNKS):
        cs = slice(j * FF_CHUNK, (j + 1) * FF_CHUNK)
        for k in range(n_parts):
            copy(j, k).wait()
        wi_ref[:, cs] = stage[0][...].astype(BF)
        wi_ref[:, D_FF + j * FF_CHUNK:D_FF + (j + 1) * FF_CHUNK] = stage[1][...].astype(BF)
        wo_ref[cs, :] = stage[2][...].astype(BF)
        if j + 1 < N_FF_CHUNKS:
            for k in range(n_parts):
                copy(j + 1, k).start()
        for rows, hh in zip(halves, h):
            _ffn_chunk(hh, wi_ref, wo_ref, acc_ref, rows, j)
    for rows in halves:
        finish(rows, x_ref[rows, :], acc_ref[rows, :])


def _ffn_rows(x_ref, g_ref, wi_ref, wo_ref, acc_ref, rows, finish):
    h = _rmsnorm(x_ref[rows, :], g_ref[...]).astype(BF)
    yield
    for j in range(N_FF_CHUNKS):
        _ffn_chunk(h, wi_ref, wo_ref, acc_ref, rows, j)
        yield
    finish(rows, x_ref[rows, :], acc_ref[rows, :])


def _ffn_tile(x_ref, g_ref, wi_ref, wo_ref, acc_ref, finish):
    halves = [_ffn_rows(x_ref, g_ref, wi_ref, wo_ref, acc_ref,
                        slice(i * FFN_HALF, (i + 1) * FFN_HALF), finish)
              for i in range(FFN_TILE // FFN_HALF)]
    for _ in _staggered(halves, 1):
        pass


def _ffn_kernel(*refs, final_norm, stream_weights, cast_modes):
    n_cast = len(cast_modes)
    xa_ref, xb_ref, g_ref, wi_in, wo_in, gf_ref = refs[:6]
    cast_src = refs[6:6 + n_cast]
    oa_ref, ob_ref = refs[6 + n_cast:8 + n_cast]
    cast_dst = refs[8 + n_cast:8 + 2 * n_cast]
    scratch = refs[8 + 2 * n_cast:]
    first = pl.program_id(0) == 0

    def finish_into(o_ref):
        def finish(rows, x, acc):
            y = x + 0.5 * acc
            o_ref[rows, :] = _rmsnorm(y, gf_ref[...]) if final_norm else y
        return finish

    if stream_weights:
        wi_ref, wo_ref, acc_ref, *stage, sem = scratch

        @pl.when(first)
        def _():
            _ffn_streamed_tile(xb_ref, g_ref, wi_in, wo_in, wi_ref, wo_ref, acc_ref, stage, sem,
                               finish_into(ob_ref))
    else:
        (acc_ref,) = scratch
        wi_ref, wo_ref = wi_in, wo_in

        @pl.when(first)
        def _():
            _ffn_tile(xb_ref, g_ref, wi_ref, wo_ref, acc_ref, finish_into(ob_ref))

    @pl.when(jnp.logical_not(first))
    def _():
        _ffn_tile(xa_ref, g_ref, wi_ref, wo_ref, acc_ref, finish_into(oa_ref))

    _cast_blocks(cast_src, cast_dst, cast_modes)


def _resident(shape):
    nd = len(shape)
    return pl.BlockSpec(shape, lambda *_: (0,) * nd, pipeline_mode=pl.Buffered(1))


def _ffn_call(xa, xb, g, wi, wo, gf, final_norm, name, cast_jobs=()):
    ta = xa.shape[0]
    n_a = ta // FFN_TILE
    assert ta % FFN_TILE == 0 and xb.shape[0] == FFN_TILE
    assert not cast_jobs or n_a == CAST_BLOCKS
    stream_weights = wi.dtype == F32
    tile_a = lambda i: (jnp.maximum(i - 1, 0), 0)
    cast_in, cast_out, cast_shapes = _cast_specs(cast_jobs, lambda i: jnp.maximum(i - 1, 0))
    if stream_weights:
        w_specs = [pl.BlockSpec(memory_space=pl.ANY)] * 2
        w_scratch = [pltpu.VMEM(wi.shape, BF), pltpu.VMEM(wo.shape, BF)]
        stage = [pltpu.VMEM((D_MODEL, FF_CHUNK), F32), pltpu.VMEM((D_MODEL, FF_CHUNK), F32),
                 pltpu.VMEM((FF_CHUNK, D_MODEL), F32)]
        sems = stage + [pltpu.SemaphoreType.DMA((3 * STREAM_PARTS,))]
    else:
        w_specs = [_resident(wi.shape), _resident(wo.shape)]
        w_scratch, sems = [], []
    return pl.pallas_call(
        functools.partial(_ffn_kernel, final_norm=final_norm, stream_weights=stream_weights,
                          cast_modes=tuple(j["transposed"] for j in cast_jobs)),
        grid=(1 + n_a,),
        in_specs=[
            pl.BlockSpec((FFN_TILE, D_MODEL), tile_a),
            _resident((FFN_TILE, D_MODEL)),
            _resident((1, D_MODEL)),
            *w_specs,
            _resident((1, D_MODEL)),
            *cast_in,
        ],
        out_specs=[
            pl.BlockSpec((FFN_TILE, D_MODEL), tile_a),
            pl.BlockSpec((FFN_TILE, D_MODEL), lambda i: (0, 0)),
            *cast_out,
        ],
        out_shape=[
            jax.ShapeDtypeStruct((ta, D_MODEL), F32),
            jax.ShapeDtypeStruct((FFN_TILE, D_MODEL), F32),
            *cast_shapes,
        ],
        scratch_shapes=w_scratch + [pltpu.VMEM((FFN_TILE, D_MODEL), F32)] + sems,
        compiler_params=pltpu.CompilerParams(
            dimension_semantics=("arbitrary",), vmem_limit_bytes=VMEM_LIMIT),
        name=name,
    )(xa, xb, g, wi, wo, gf, *[j["w"] for j in cast_jobs])


def _mix_block(cols, chunk, consts, get_state, put_state, emit):
    (wmix_ref, bias_ref, vnorm_ref, anorm_ref, wa2_ref, ba_ref, onorm_ref, cum_ref) = consts
    n_chunks = ROWS // chunk
    shift = int(np.log2(chunk))

    z = _dot(cols(OFF_LR, LANES).astype(BF), wa2_ref[...]) + ba_ref[...]
    la = _log_sigmoid(z) * (1.0 / B_GATE_NORMALIZER)
    la_hi = la.astype(BF)
    la_lo = (la - la_hi.astype(F32)).astype(BF)
    sums = _dot(cum_ref[...], jnp.concatenate([la_hi, la_lo], axis=0))
    b = sums[0:ROWS]
    b_ref = sums[ROWS:2 * ROWS]
    b_last = sums[2 * ROWS:3 * ROWS]
    yield

    ya = []
    va = []
    ssq = None
    for g in range(A_GROUPS):
        gs = slice(g * LANES, (g + 1) * LANES)
        u = _gelu_tanh(cols(OFF_U + g * LANES, LANES))
        v = _rmsnorm(_gelu_tanh(cols(OFF_V + g * LANES, LANES)), vnorm_ref[:, gs])
        yield
        zg = _dot(wmix_ref[g], v.astype(BF)) + bias_ref[g]
        yg = u * zg
        s = jnp.sum(yg * yg, axis=-1, keepdims=True)
        ssq = s if ssq is None else ssq + s
        ya.append(yg)
        va.append(v)
        yield
    inv = lax.rsqrt(ssq * (1.0 / A_WIDTH) + EPS)
    ya = [(ya[g] * inv) * anorm_ref[:, g * LANES:(g + 1) * LANES] for g in range(A_GROUPS)]
    yield
    gate = []
    for h in range(B_HEADS):
        gate.append(_silu(cols(OFF_R + h * B_DV, B_DV)))
        if h % 2 == 1:
            yield

    ri = lax.broadcasted_iota(jnp.int32, (ROWS, ROWS), 0)
    ci = lax.broadcasted_iota(jnp.int32, (ROWS, ROWS), 1)
    rchunk = lax.shift_right_logical(ri, shift)
    cchunk = lax.shift_right_logical(ci, shift)
    causal = (rchunk == cchunk) & (ci <= ri)
    lane = lax.broadcasted_iota(jnp.int32, (ROWS, LANES), 1)
    head0 = lane < B_DK

    yb = []
    for pair in range(B_HEADS // 2):
        ls = slice(pair * LANES, (pair + 1) * LANES)
        q = cols(OFF_Q + pair * LANES, LANES) * np.float32(B_DK ** -0.5)
        k = cols(OFF_K + pair * LANES, LANES)
        vv_bf = cols(OFF_VB + pair * 2 * B_DV, 2 * B_DV).astype(BF)
        bp, brp, blp = b[:, ls], b_ref[:, ls], b_last[:, ls]
        qs = q * jnp.exp(bp - brp)
        ks = (k * jnp.exp(brp - bp)).astype(BF)
        kl_t = (k * jnp.exp(blp - bp)).T
        bl_t = blp.T
        qb = (q * jnp.exp(bp)).astype(BF)
        yield

        o_heads = []
        for hh in range(2):
            msk = head0 if hh == 0 else jnp.logical_not(head0)
            qh = jnp.where(msk, qs, 0.0).astype(BF)
            sc = jnp.where(causal, _dot_nt(qh, ks), 0.0).astype(BF)
            o_heads.append(_dot(sc, vv_bf[:, hh * B_DV:(hh + 1) * B_DV]))
            yield
        o = jnp.concatenate(o_heads, axis=1)

        o_inter = []
        for j in range(n_chunks):
            s_prev = get_state(pair, j)
            o_inter.append(_dot(qb[j * chunk:(j + 1) * chunk], _blockdiag_bf16(*s_prev)))
            in_chunk = cchunk == j
            upd = _dot(jnp.where(in_chunk, kl_t, 0.0).astype(BF), vv_bf)
            dec = jnp.exp(bl_t[:, j * chunk:j * chunk + 1])
            put_state(pair, j, tuple(
                dec[hh * B_DK:(hh + 1) * B_DK] * s_prev[hh]
                + upd[hh * B_DK:(hh + 1) * B_DK, hh * B_DV:(hh + 1) * B_DV] for hh in range(2)))
        o = o + jnp.concatenate(o_inter, axis=0)
        yield

        for hh in range(2):
            h_idx = 2 * pair + hh
            hs = slice(h_idx * B_DV, (h_idx + 1) * B_DV)
            oh = _rmsnorm(o[:, hh * B_DV:(hh + 1) * B_DV], onorm_ref[:, hs])
            yb.append(oh * gate[h_idx])
        yield

    emit(jnp.concatenate(ya + yb, axis=1).astype(BF), jnp.concatenate(va, axis=1))


def _blockdiag_bf16(s0, s1):
    s0 = s0.astype(BF)
    s1 = s1.astype(BF)
    zero = jnp.zeros_like(s0)
    return jnp.concatenate(
        [jnp.concatenate([s0, zero], axis=1), jnp.concatenate([zero, s1], axis=1)], axis=0)


def _in_proj(x1_ref, g2_ref, wp_refs, p_scr, g, rows):
    wp_ref, wpt_ref = wp_refs
    h = _rmsnorm(x1_ref[rows, :], g2_ref[...]).astype(BF)
    for c0 in range(0, P_MAIN, MXU_DIM):
        cs = slice(c0, c0 + MXU_DIM)
        p_scr[g, :, cs] = _dot(h, wp_ref[:, cs])
    p_scr[g, :, P_MAIN:P_COLS] = _dot(h, wpt_ref[...])


def _out_proj(x1_ref, y_scr, wout_ref, x2_ref, g, rows):
    y = y_scr[g]
    for c0 in range(0, D_MODEL, MXU_DIM):
        cs = slice(c0, c0 + MXU_DIM)
        x2_ref[rows, cs] = x1_ref[rows, cs] + _dot(y, wout_ref[:, cs])


def _mixproj_tile(x1_ref, g2_ref, wp_refs, wout_ref, x2_ref, p_scr, y_scr, group, n_groups,
                  make_block):
    rows = [slice(g * group, (g + 1) * group) for g in range(n_groups)]
    for g in range(n_groups):
        _in_proj(x1_ref, g2_ref, wp_refs, p_scr, g, rows[g])
    blocks = [make_block(g, i) for g in range(n_groups) for i in range(group // ROWS)]
    for _ in _staggered(blocks, MIX_STAGGER):
        pass
    for g in range(n_groups):
        _out_proj(x1_ref, y_scr, wout_ref, x2_ref, g, rows[g])


def _block_cols(p_scr, g, i):
    return lambda off, width: p_scr[g, i * ROWS:(i + 1) * ROWS, off:off + width]


def _mix_setup(a_ws_ref, a_bs_ref, b_wa2_ref, wmix_scr, bias_scr, wa2_scr, cum_scr, seq_len, chunk):
    rr = lax.broadcasted_iota(jnp.int32, (ROWS, 2 * ROWS), 0)
    kk = lax.broadcasted_iota(jnp.int32, (ROWS, 2 * ROWS), 1) & (ROWS - 1)
    cshift = int(np.log2(chunk))
    same = lax.shift_right_logical(rr, cshift) == lax.shift_right_logical(kk, cshift)
    for i, keep in enumerate((same & (kk <= rr), same & ((kk & (chunk - 1)) <= chunk // 2), same)):
        cum_scr[i * ROWS:(i + 1) * ROWS, :] = jnp.where(keep, 1.0, 0.0).astype(BF)

    r = lax.broadcasted_iota(jnp.int32, (ROWS, ROWS), 0)
    c = lax.broadcasted_iota(jnp.int32, (ROWS, ROWS), 1)
    lane = lax.broadcasted_iota(jnp.int32, (8, LANES), 1)
    shift = int(np.log2(seq_len))
    for g in range(A_GROUPS):
        w = a_ws_ref[g]
        if seq_len == ROWS:
            wm = jnp.where(c <= r, w, 0.0).astype(BF)
        else:
            w8 = jnp.where((r < seq_len) & (c <= r), w, 0.0).astype(BF)
            sel = ((r & (seq_len - 1)) == c).astype(BF)
            tiled = _dot_nt(_dot(sel, w8).astype(BF), sel)
            same = lax.shift_right_logical(r, shift) == lax.shift_right_logical(c, shift)
            wm = jnp.where(same, tiled, 0.0).astype(BF)
        wmix_scr[g] = wm
        b = jnp.broadcast_to(a_bs_ref[g:g + 1, :], (8, LANES))
        if seq_len < ROWS:
            b = jnp.where(lane < seq_len, b, 0.0)
            k = seq_len
            while k < ROWS:
                b = b + pltpu.roll(b, k, axis=1)
                k *= 2
        bias_scr[g] = jnp.broadcast_to(b[0:1, :], (ROWS, LANES)).T
    wa2_scr[...] = jnp.zeros_like(wa2_scr)
    wa2_scr[0:B_LOWRANK, :] = b_wa2_ref[...].astype(BF)


def _mixproj_prompt_kernel(*refs, group, n_groups, cast_modes):
    n_cast = len(cast_modes)
    (x1_ref, g2_ref, wp_ref, wpt_ref, wout_ref, a_ws_ref, a_bs_ref, vnorm_ref, anorm_ref,
     b_wa2_ref, ba_ref, onorm_ref) = refs[:12]
    cast_src = refs[12:12 + n_cast]
    x2_ref, st_ref = refs[12 + n_cast:14 + n_cast]
    cast_dst = refs[14 + n_cast:14 + 2 * n_cast]
    p_scr, y_scr, wmix_scr, bias_scr, wa2_scr, cum_scr, s_scr = refs[14 + 2 * n_cast:]

    @pl.when((pl.program_id(0) == 0) & (pl.program_id(1) == 0))
    def _():
        _mix_setup(a_ws_ref, a_bs_ref, b_wa2_ref, wmix_scr, bias_scr, wa2_scr, cum_scr, A_CHUNK,
                   B_CHUNK)

    @pl.when(pl.program_id(1) == 0)
    def _():
        s_scr[...] = jnp.zeros_like(s_scr)

    consts = (wmix_scr, bias_scr, vnorm_ref, anorm_ref, wa2_scr, ba_ref, onorm_ref, cum_scr)
    cur = {pair: (s_scr[2 * pair], s_scr[2 * pair + 1]) for pair in range(B_HEADS // 2)}

    def get_state(pair, j):
        return cur[pair]

    def put_state(pair, j, s):
        cur[pair] = s

    def make_block(g, i):
        def emit(y, va):
            y_scr[g, i * ROWS:(i + 1) * ROWS, :] = y
        return _mix_block(_block_cols(p_scr, g, i), B_CHUNK, consts, get_state, put_state, emit)

    _mixproj_tile(x1_ref, g2_ref, (wp_ref, wpt_ref), wout_ref, x2_ref, p_scr, y_scr, group,
                  n_groups, make_block)
    _cast_blocks(cast_src, cast_dst, cast_modes)

    for pair in range(B_HEADS // 2):
        for hh in range(2):
            s_scr[2 * pair + hh] = cur[pair][hh]
            st_ref[0, 2 * pair + hh] = cur[pair][hh]


def _mixproj_sample_kernel(x1_ref, s0_ref, g2_ref, wp_ref, wpt_ref, wout_ref, a_ws_ref, a_bs_ref,
                           vnorm_ref, anorm_ref, b_wa2_ref, ba_ref, onorm_ref, x2_ref, va_ref,
                           st_ref, p_scr, y_scr, wmix_scr, bias_scr, wa2_scr, cum_scr, *, chunk,
                           group, n_groups):
    @pl.when(pl.program_id(0) == 0)
    def _():
        _mix_setup(a_ws_ref, a_bs_ref, b_wa2_ref, wmix_scr, bias_scr, wa2_scr, cum_scr, chunk, chunk)

    consts = (wmix_scr, bias_scr, vnorm_ref, anorm_ref, wa2_scr, ba_ref, onorm_ref, cum_scr)
    n_blk = group // ROWS
    per_blk = ROWS // chunk

    def make_block(g, i):
        blk = g * n_blk + i

        def get_state(pair, j):
            n = blk * per_blk + j
            return s0_ref[n, 2 * pair], s0_ref[n, 2 * pair + 1]

        def put_state(pair, j, s):
            n = blk * per_blk + j
            st_ref[n, 2 * pair] = s[0]
            st_ref[n, 2 * pair + 1] = s[1]

        def emit(y, va):
            y_scr[g, i * ROWS:(i + 1) * ROWS, :] = y
            va_ref[blk * ROWS:(blk + 1) * ROWS, :] = va

        return _mix_block(_block_cols(p_scr, g, i), chunk, consts, get_state, put_state, emit)

    _mixproj_tile(x1_ref, g2_ref, (wp_ref, wpt_ref), wout_ref, x2_ref, p_scr, y_scr, group,
                  n_groups, make_block)


def _mix_param_specs():
    return [
        _resident((A_GROUPS, A_CHUNK, A_CHUNK)),
        _resident((A_GROUPS, A_CHUNK)),
        _resident((1, A_WIDTH)),
        _resident((1, A_WIDTH)),
        _resident((B_LOWRANK, B_HEADS * B_DK)),
        _resident((1, B_HEADS * B_DK)),
        _resident((1, B_HEADS * B_DV)),
    ]


def _proj_specs():
    return [
        _resident((1, D_MODEL)),
        _resident((D_MODEL, P_MAIN)),
        _resident((D_MODEL, P_COLS - P_MAIN)),
        _resident((D_MODEL, D_MODEL)),
    ]


def _mixproj_scratch(group, n_groups):
    return [pltpu.VMEM((n_groups, group, P_COLS), F32),
            pltpu.VMEM((n_groups, group, D_MODEL), BF),
            pltpu.VMEM((A_GROUPS, ROWS, ROWS), BF),
            pltpu.VMEM((A_GROUPS, ROWS, LANES), F32),
            pltpu.VMEM((LANES, B_HEADS * B_DK), BF),
            pltpu.VMEM((3 * ROWS, 2 * ROWS), BF)]


def _mixproj_prompt_call(x1, g2, wp, wpt, wout, consts, batch, seq, tile, group, cast_jobs=()):
    n_tiles = seq // tile
    n_groups = tile // group
    assert not cast_jobs or batch * n_tiles == CAST_BLOCKS
    row = lambda b, s: (b * n_tiles + s, 0)
    cast_in, cast_out, cast_shapes = _cast_specs(cast_jobs, lambda b, s: b * n_tiles + s)
    return pl.pallas_call(
        functools.partial(_mixproj_prompt_kernel, group=group, n_groups=n_groups,
                          cast_modes=tuple(j["transposed"] for j in cast_jobs)),
        grid=(batch, n_tiles),
        in_specs=[pl.BlockSpec((tile, D_MODEL), row)] + _proj_specs() + _mix_param_specs()
        + cast_in,
        out_specs=[
            pl.BlockSpec((tile, D_MODEL), row),
            pl.BlockSpec((1, B_HEADS, B_DK, B_DV), lambda b, s: (b, 0, 0, 0)),
            *cast_out,
        ],
        out_shape=[
            jax.ShapeDtypeStruct((batch * seq, D_MODEL), F32),
            jax.ShapeDtypeStruct((batch, B_HEADS, B_DK, B_DV), F32),
            *cast_shapes,
        ],
        scratch_shapes=_mixproj_scratch(group, n_groups)
        + [pltpu.VMEM((B_HEADS, B_DK, B_DV), F32)],
        compiler_params=pltpu.CompilerParams(
            dimension_semantics=("arbitrary", "arbitrary"), vmem_limit_bytes=VMEM_LIMIT),
        name="mixproj_prompt",
    )(x1, g2, wp, wpt, wout, *consts, *[j["w"] for j in cast_jobs])


def _mixproj_sample_call(x1, s0, g2, wp, wpt, wout, consts, n_seq, seq, tile, group):
    seq_per_step = tile // seq
    n_groups = tile // group
    row = lambda i: (i, 0)
    state = lambda i: (i, 0, 0, 0)
    return pl.pallas_call(
        functools.partial(_mixproj_sample_kernel, chunk=seq, group=group, n_groups=n_groups),
        grid=(n_seq // seq_per_step,),
        in_specs=[
            pl.BlockSpec((tile, D_MODEL), row),
            pl.BlockSpec((seq_per_step, B_HEADS, B_DK, B_DV), state),
        ] + _proj_specs() + _mix_param_specs(),
        out_specs=[
            pl.BlockSpec((tile, D_MODEL), row),
            pl.BlockSpec((tile, A_WIDTH), row),
            pl.BlockSpec((seq_per_step, B_HEADS, B_DK, B_DV), state),
        ],
        out_shape=[
            jax.ShapeDtypeStruct((n_seq * seq, D_MODEL), F32),
            jax.ShapeDtypeStruct((n_seq * seq, A_WIDTH), F32),
            jax.ShapeDtypeStruct((n_seq, B_HEADS, B_DK, B_DV), F32),
        ],
        scratch_shapes=_mixproj_scratch(group, n_groups),
        compiler_params=pltpu.CompilerParams(
            dimension_semantics=("arbitrary",), vmem_limit_bytes=VMEM_LIMIT),
        name="mixproj_sample",
    )(x1, s0, g2, wp, wpt, wout, *consts)


def _mix_params(a_ws, a_bs, a_vnorm, a_onorm, b_wa2, b_ba, b_onorm):
    return (
        a_ws.reshape(A_GROUPS, A_CHUNK, A_CHUNK),
        a_bs.reshape(A_GROUPS, A_CHUNK),
        a_vnorm.reshape(1, A_WIDTH),
        a_onorm.reshape(1, A_WIDTH),
        b_wa2.reshape(B_LOWRANK, B_HEADS * B_DK),
        b_ba.reshape(1, B_HEADS * B_DK),
        b_onorm.reshape(1, B_HEADS * B_DV),
    )


def kernel(x_prompt, x_sample, state_gla, ffn1_norm, ffn1_w_in, ffn1_w_out, mix_norm, w_in,
           a_ws, a_bs, a_vnorm, a_onorm, b_wa2, b_ba, b_onorm, w_out, ffn2_norm, ffn2_w_in,
           ffn2_w_out, final_norm):
    batch, seq, _ = x_prompt.shape
    n_dec, dec_seq, _ = x_sample.shape
    assert ffn1_norm.shape[0] == 1, "single-layer kernel"
    assert seq % A_CHUNK == 0 and ROWS % dec_seq == 0 and dec_seq <= B_CHUNK

    g1 = ffn1_norm.reshape(1, D_MODEL)
    wi1 = ffn1_w_in.reshape(D_MODEL, 2 * D_FF)
    wo1 = ffn1_w_out.reshape(D_FF, D_MODEL)
    g2 = mix_norm.reshape(1, D_MODEL)
    w_in_t = jnp.swapaxes(w_in, 1, 2).reshape(IN_COLS, D_MODEL)
    w_out2d = w_out.reshape(D_MODEL, D_MODEL)
    g3 = ffn2_norm.reshape(1, D_MODEL)
    wi2 = ffn2_w_in.reshape(D_MODEL, 2 * D_FF)
    wo2 = ffn2_w_out.reshape(D_FF, D_MODEL)
    gf = final_norm.reshape(1, D_MODEL)
    mix_params = _mix_params(a_ws, a_bs, a_vnorm, a_onorm, b_wa2, b_ba, b_onorm)

    xp = x_prompt.reshape(batch * seq, D_MODEL)
    xs = x_sample.reshape(n_dec * dec_seq, D_MODEL)
    s0 = state_gla.reshape(state_gla.shape[1:])

    x1p, x1s, wp, wpt, wout = _ffn_call(
        xp, xs, g1, wi1, wo1, gf, False, "ffn1",
        cast_jobs=(_cast_transposed(w_in_t, 0, MXU_DIM, P_MAIN),
                   _cast_transposed(w_in_t, P_MAIN, P_COLS - P_MAIN, P_COLS - P_MAIN,
                                    valid_cols=IN_COLS - P_MAIN),
                   _cast_rows(w_out2d)))
    x2p, st_p, wi2_bf, wo2_bf = _mixproj_prompt_call(
        x1p, g2, wp, wpt, wout, mix_params, batch, seq, tile=1024, group=512,
        cast_jobs=(_cast_rows(wi2), _cast_rows(wo2)))
    x2s, va_s, st_s = _mixproj_sample_call(x1s, s0, g2, wp, wpt, wout, mix_params, n_dec, dec_seq,
                                           tile=256, group=128)
    out_p, out_s = _ffn_call(x2p, x2s, g3, wi2_bf, wo2_bf, gf, True, "ffn2")

    return (
        out_p.reshape(batch, seq, D_MODEL),
        out_s.reshape(n_dec, dec_seq, D_MODEL),
        st_p[None],
        st_s[None],
        va_s.reshape(1, n_dec, dec_seq, A_WIDTH),
    )
```

```python
import functools

import numpy as np
import jax
import jax.numpy as jnp
from jax import lax
from jax.experimental import pallas as pl
from jax.experimental.pallas import tpu as pltpu

D_MODEL = 1024
D_FF = 2816
A_WIDTH = 512
A_GROUPS = 4
A_CHUNK = 128
B_HEADS = 4
B_DK = 64
B_DV = 128
B_LOWRANK = 16
B_GATE_NORMALIZER = 16.0
B_CHUNK = 64
EPS = 1e-6

OFF_U, OFF_V, OFF_Q, OFF_K, OFF_VB, OFF_R, OFF_LR = 0, 512, 1024, 1280, 1536, 2048, 2560
IN_COLS = 2576

LANES = 128
P_COLS = -(-IN_COLS // LANES) * LANES
MXU_DIM = 256
ROWS = 128
FF_CHUNK = MXU_DIM
N_FF_CHUNKS = D_FF // FF_CHUNK
VMEM_LIMIT = 56 * 1024 * 1024

FFN_TILE = 1024
FFN_HALF = FFN_TILE // 2
W_IN_SLOT_ROWS = 128
STREAM_PARTS = 2
P_MAIN = (IN_COLS // LANES) * LANES
CAST_BLOCKS = 16
MIX_STAGGER = 10

BF = jnp.bfloat16
F32 = jnp.float32


def _dot(a, b):
    return jnp.dot(a, b, preferred_element_type=F32)


def _dot_nt(a, b):
    return lax.dot_general(a, b, (((1,), (1,)), ((), ())), preferred_element_type=F32)


def _rmsnorm(x, g):
    return (x * lax.rsqrt(jnp.mean(x * x, axis=-1, keepdims=True) + EPS)) * g


def _gelu_tanh(x):
    c = np.float32(np.sqrt(2.0 / np.pi))
    ca = np.float32(np.sqrt(2.0 / np.pi) * 0.044715)
    t = jnp.tanh(x * (c + ca * (x * x)))
    hx = 0.5 * x
    return hx + hx * t


def _silu(x):
    return x * (1.0 / (1.0 + jnp.exp(-x)))


def _log_sigmoid(x):
    return jnp.minimum(x, 0.0) - jnp.log(1.0 + jnp.exp(-jnp.abs(x)))


def _staggered(gens, stagger):
    live = [True] * len(gens)
    t = 0
    while any(live):
        for i, gen in enumerate(gens):
            if live[i] and t >= i * stagger:
                try:
                    next(gen)
                except StopIteration:
                    live[i] = False
        t += 1
        yield


def _cast_rows(w):
    rows, cols = w.shape
    blk = (rows // CAST_BLOCKS, cols)
    return dict(w=w, src_block=blk, src_index=lambda r: (r, 0), dst_block=blk,
                dst_index=lambda r: (r, 0), out=jax.ShapeDtypeStruct(w.shape, BF), transposed=None)


def _cast_transposed(w_t, row0, n_rows, out_cols, valid_cols=None):
    cols = w_t.shape[1]
    first, last = row0 // n_rows, (row0 + out_cols) // n_rows - 1
    return dict(w=w_t, src_block=(n_rows, cols), dst_block=(cols, n_rows),
                src_index=lambda r: (jnp.minimum(first + r, last), 0),
                dst_index=lambda r: (0, jnp.minimum(r, last - first)),
                out=jax.ShapeDtypeStruct((cols, out_cols), BF),
                transposed=n_rows if valid_cols is None else valid_cols)


def _cast_specs(jobs, row_block):
    in_specs = [pl.BlockSpec(j["src_block"], lambda *ids, j=j: j["src_index"](row_block(*ids)))
                for j in jobs]
    out_specs = [pl.BlockSpec(j["dst_block"], lambda *ids, j=j: j["dst_index"](row_block(*ids)))
                 for j in jobs]
    return in_specs, out_specs, [j["out"] for j in jobs]


def _cast_blocks(src_refs, dst_refs, transposed):
    for src, dst, valid in zip(src_refs, dst_refs, transposed):
        x = src[...]
        if valid is not None:
            x = x.T
            if valid < x.shape[1]:
                lane = lax.broadcasted_iota(jnp.int32, x.shape, 1)
                x = jnp.where(lane < valid, x, 0.0)
        dst[...] = x.astype(BF)


def _ffn_chunk(h, wi_ref, wo_ref, acc_ref, rows, j):
    cs = slice(j * FF_CHUNK, (j + 1) * FF_CHUNK)
    gate = _dot(h, wi_ref[:, cs])
    up = _dot(h, wi_ref[:, D_FF + j * FF_CHUNK:D_FF + (j + 1) * FF_CHUNK])
    act = (_silu(gate) * up).astype(BF)
    part = _dot(act, wo_ref[cs, :])
    if j == 0:
        acc_ref[rows, :] = part
    else:
        acc_ref[rows, :] += part


def _ffn_streamed_tile(x_ref, g_ref, wi_hbm, wo_hbm, wi_ref, wo_ref, acc_ref, stage, sem, finish):
    n_slots = acc_ref.shape[0] // W_IN_SLOT_ROWS
    pieces = [(pl.ds(r0, W_IN_SLOT_ROWS), pl.ds(c0, min(D_MODEL, 2 * D_FF - c0)))
              for r0 in range(0, D_MODEL, W_IN_SLOT_ROWS) for c0 in range(0, 2 * D_FF, D_MODEL)]

    def slot(k):
        return acc_ref.at[pl.ds((k % n_slots) * W_IN_SLOT_ROWS, W_IN_SLOT_ROWS),
                          pl.ds(0, pieces[k][1].size)]

    def in_copy(k):
        return pltpu.make_async_copy(wi_hbm.at[pieces[k]], slot(k), sem.at[k % n_slots])

    n_part = FF_CHUNK // STREAM_PARTS

    def out_copy(j, p):
        rows = pl.ds(p * n_part, n_part)
        return pltpu.make_async_copy(
            wo_hbm.at[pl.ds(j * FF_CHUNK + p * n_part, n_part), pl.ds(0, D_MODEL)],
            stage.at[rows, pl.ds(0, D_MODEL)], sem.at[n_slots + p])

    for p in range(STREAM_PARTS):
        out_copy(0, p).start()
    for k in range(n_slots):
        in_copy(k).start()
    halves = [slice(i * FFN_HALF, (i + 1) * FFN_HALF) for i in range(FFN_TILE // FFN_HALF)]
    h = [_rmsnorm(x_ref[rows, :], g_ref[...]).astype(BF) for rows in halves]
    for k in range(len(pieces)):
        in_copy(k).wait()
        wi_ref.at[pieces[k]][...] = slot(k)[...].astype(BF)
        if k + n_slots < len(pieces):
            in_copy(k + n_slots).start()
    for j in range(N_FF_CHUNKS):
        for p in range(STREAM_PARTS):
            out_copy(j, p).wait()
        wo_ref[j * FF_CHUNK:(j + 1) * FF_CHUNK, :] = stage[...].astype(BF)
        if j + 1 < N_FF_CHUNKS:
            for p in range(STREAM_PARTS):
                out_copy(j + 1, p).start()
        for rows, hh in zip(halves, h):
            _ffn_chunk(hh, wi_ref, wo_ref, acc_ref, rows, j)
    for rows in halves:
        finish(rows, x_ref[rows, :], acc_ref[rows, :])


def _ffn_rows(x_ref, g_ref, wi_ref, wo_ref, acc_ref, rows, finish):
    h = _rmsnorm(x_ref[rows, :], g_ref[...]).astype(BF)
    yield
    for j in range(N_FF_CHUNKS):
        _ffn_chunk(h, wi_ref, wo_ref, acc_ref, rows, j)
        yield
    finish(rows, x_ref[rows, :], acc_ref[rows, :])


def _ffn_tile(x_ref, g_ref, wi_ref, wo_ref, acc_ref, finish):
    halves = [_ffn_rows(x_ref, g_ref, wi_ref, wo_ref, acc_ref,
                        slice(i * FFN_HALF, (i + 1) * FFN_HALF), finish)
              for i in range(FFN_TILE // FFN_HALF)]
    for _ in _staggered(halves, 1):
        pass


def _ffn_kernel(*refs, final_norm, stream_weights, cast_modes):
    n_cast = len(cast_modes)
    xa_ref, xb_ref, g_ref, wi_in, wo_in, gf_ref = refs[:6]
    cast_src = refs[6:6 + n_cast]
    oa_ref, ob_ref = refs[6 + n_cast:8 + n_cast]
    cast_dst = refs[8 + n_cast:8 + 2 * n_cast]
    scratch = refs[8 + 2 * n_cast:]
    first = pl.program_id(0) == 0

    def finish_into(o_ref):
        def finish(rows, x, acc):
            y = x + 0.5 * acc
            o_ref[rows, :] = _rmsnorm(y, gf_ref[...]) if final_norm else y
        return finish

    if stream_weights:
        wi_ref, wo_ref, acc_ref, stage, sem = scratch

        @pl.when(first)
        def _():
            _ffn_streamed_tile(xb_ref, g_ref, wi_in, wo_in, wi_ref, wo_ref, acc_ref, stage, sem,
                               finish_into(ob_ref))
    else:
        (acc_ref,) = scratch
        wi_ref, wo_ref = wi_in, wo_in

        @pl.when(first)
        def _():
            _ffn_tile(xb_ref, g_ref, wi_ref, wo_ref, acc_ref, finish_into(ob_ref))

    @pl.when(jnp.logical_not(first))
    def _():
        _ffn_tile(xa_ref, g_ref, wi_ref, wo_ref, acc_ref, finish_into(oa_ref))

    _cast_blocks(cast_src, cast_dst, cast_modes)


def _resident(shape):
    nd = len(shape)
    return pl.BlockSpec(shape, lambda *_: (0,) * nd, pipeline_mode=pl.Buffered(1))


def _ffn_call(xa, xb, g, wi, wo, gf, final_norm, name, cast_jobs=()):
    ta = xa.shape[0]
    n_a = ta // FFN_TILE
    assert ta % FFN_TILE == 0 and xb.shape[0] == FFN_TILE
    assert not cast_jobs or n_a == CAST_BLOCKS
    stream_weights = wi.dtype == F32
    tile_a = lambda i: (jnp.maximum(i - 1, 0), 0)
    cast_in, cast_out, cast_shapes = _cast_specs(cast_jobs, lambda i: jnp.maximum(i - 1, 0))
    if stream_weights:
        w_specs = [pl.BlockSpec(memory_space=pl.ANY)] * 2
        w_scratch = [pltpu.VMEM(wi.shape, BF), pltpu.VMEM(wo.shape, BF)]
        sems = [pltpu.VMEM((FF_CHUNK, D_MODEL), F32),
                pltpu.SemaphoreType.DMA((FFN_TILE // W_IN_SLOT_ROWS + STREAM_PARTS,))]
    else:
        w_specs = [_resident(wi.shape), _resident(wo.shape)]
        w_scratch, sems = [], []
    return pl.pallas_call(
        functools.partial(_ffn_kernel, final_norm=final_norm, stream_weights=stream_weights,
                          cast_modes=tuple(j["transposed"] for j in cast_jobs)),
        grid=(1 + n_a,),
        in_specs=[
            pl.BlockSpec((FFN_TILE, D_MODEL), tile_a),
            _resident((FFN_TILE, D_MODEL)),
            _resident((1, D_MODEL)),
            *w_specs,
            _resident((1, D_MODEL)),
            *cast_in,
        ],
        out_specs=[
            pl.BlockSpec((FFN_TILE, D_MODEL), tile_a),
            pl.BlockSpec((FFN_TILE, D_MODEL), lambda i: (0, 0)),
            *cast_out,
        ],
        out_shape=[
            jax.ShapeDtypeStruct((ta, D_MODEL), F32),
            jax.ShapeDtypeStruct((FFN_TILE, D_MODEL), F32),
            *cast_shapes,
        ],
        scratch_shapes=w_scratch + [pltpu.VMEM((FFN_TILE, D_MODEL), F32)] + sems,
        compiler_params=pltpu.CompilerParams(
            dimension_semantics=("arbitrary",), vmem_limit_bytes=VMEM_LIMIT),
        name=name,
    )(xa, xb, g, wi, wo, gf, *[j["w"] for j in cast_jobs])


def _mix_block(cols, chunk, consts, get_state, put_state, emit):
    (wmix_ref, bias_ref, vnorm_ref, anorm_ref, wa2_ref, ba_ref, onorm_ref, cum_ref) = consts
    n_chunks = ROWS // chunk
    shift = int(np.log2(chunk))

    z = _dot(cols(OFF_LR, LANES).astype(BF), wa2_ref[...]) + ba_ref[...]
    la = _log_sigmoid(z) * (1.0 / B_GATE_NORMALIZER)
    la_hi = la.astype(BF)
    la_lo = (la - la_hi.astype(F32)).astype(BF)
    sums = _dot(cum_ref[...], jnp.concatenate([la_hi, la_lo], axis=0))
    b = sums[0:ROWS]
    b_ref = sums[ROWS:2 * ROWS]
    b_last = sums[2 * ROWS:3 * ROWS]
    yield

    ya = []
    va = []
    ssq = None
    for g in range(A_GROUPS):
        gs = slice(g * LANES, (g + 1) * LANES)
        u = _gelu_tanh(cols(OFF_U + g * LANES, LANES))
        v = _rmsnorm(_gelu_tanh(cols(OFF_V + g * LANES, LANES)), vnorm_ref[:, gs])
        yield
        zg = _dot(wmix_ref[g], v.astype(BF)) + bias_ref[g]
        yg = u * zg
        s = jnp.sum(yg * yg, axis=-1, keepdims=True)
        ssq = s if ssq is None else ssq + s
        ya.append(yg)
        va.append(v)
        yield
    inv = lax.rsqrt(ssq * (1.0 / A_WIDTH) + EPS)
    ya = [(ya[g] * inv) * anorm_ref[:, g * LANES:(g + 1) * LANES] for g in range(A_GROUPS)]
    yield
    gate = []
    for h in range(B_HEADS):
        gate.append(_silu(cols(OFF_R + h * B_DV, B_DV)))
        if h % 2 == 1:
            yield

    ri = lax.broadcasted_iota(jnp.int32, (ROWS, ROWS), 0)
    ci = lax.broadcasted_iota(jnp.int32, (ROWS, ROWS), 1)
    rchunk = lax.shift_right_logical(ri, shift)
    cchunk = lax.shift_right_logical(ci, shift)
    causal = (rchunk == cchunk) & (ci <= ri)
    lane = lax.broadcasted_iota(jnp.int32, (ROWS, LANES), 1)
    head0 = lane < B_DK

    yb = []
    for pair in range(B_HEADS // 2):
        ls = slice(pair * LANES, (pair + 1) * LANES)
        q = cols(OFF_Q + pair * LANES, LANES) * np.float32(B_DK ** -0.5)
        k = cols(OFF_K + pair * LANES, LANES)
        vv_bf = cols(OFF_VB + pair * 2 * B_DV, 2 * B_DV).astype(BF)
        bp, brp, blp = b[:, ls], b_ref[:, ls], b_last[:, ls]
        qs = q * jnp.exp(bp - brp)
        ks = (k * jnp.exp(brp - bp)).astype(BF)
        kl_t = (k * jnp.exp(blp - bp)).T
        bl_t = blp.T
        qb = (q * jnp.exp(bp)).astype(BF)
        yield

        o_heads = []
        for hh in range(2):
            msk = head0 if hh == 0 else jnp.logical_not(head0)
            qh = jnp.where(msk, qs, 0.0).astype(BF)
            sc = jnp.where(causal, _dot_nt(qh, ks), 0.0).astype(BF)
            o_heads.append(_dot(sc, vv_bf[:, hh * B_DV:(hh + 1) * B_DV]))
            yield
        o = jnp.concatenate(o_heads, axis=1)

        o_inter = []
        for j in range(n_chunks):
            s_prev = get_state(pair, j)
            o_inter.append(_dot(qb[j * chunk:(j + 1) * chunk], _blockdiag_bf16(*s_prev)))
            in_chunk = cchunk == j
            upd = _dot(jnp.where(in_chunk, kl_t, 0.0).astype(BF), vv_bf)
            dec = jnp.exp(bl_t[:, j * chunk:j * chunk + 1])
            put_state(pair, j, tuple(
                dec[hh * B_DK:(hh + 1) * B_DK] * s_prev[hh]
                + upd[hh * B_DK:(hh + 1) * B_DK, hh * B_DV:(hh + 1) * B_DV] for hh in range(2)))
        o = o + jnp.concatenate(o_inter, axis=0)
        yield

        for hh in range(2):
            h_idx = 2 * pair + hh
            hs = slice(h_idx * B_DV, (h_idx + 1) * B_DV)
            oh = _rmsnorm(o[:, hh * B_DV:(hh + 1) * B_DV], onorm_ref[:, hs])
            yb.append(oh * gate[h_idx])
        yield

    emit(jnp.concatenate(ya + yb, axis=1).astype(BF), jnp.concatenate(va, axis=1))


def _blockdiag_bf16(s0, s1):
    s0 = s0.astype(BF)
    s1 = s1.astype(BF)
    zero = jnp.zeros_like(s0)
    return jnp.concatenate(
        [jnp.concatenate([s0, zero], axis=1), jnp.concatenate([zero, s1], axis=1)], axis=0)


def _in_proj(x1_ref, g2_ref, wp_refs, p_scr, g, rows):
    wp_ref, wpt_ref = wp_refs
    h = _rmsnorm(x1_ref[rows, :], g2_ref[...]).astype(BF)
    for c0 in range(0, P_MAIN, MXU_DIM):
        cs = slice(c0, c0 + MXU_DIM)
        p_scr[g, :, cs] = _dot(h, wp_ref[:, cs])
    p_scr[g, :, P_MAIN:P_COLS] = _dot(h, wpt_ref[...])


def _out_proj(x1_ref, y_scr, wout_ref, x2_ref, g, rows):
    y = y_scr[g]
    for c0 in range(0, D_MODEL, MXU_DIM):
        cs = slice(c0, c0 + MXU_DIM)
        x2_ref[rows, cs] = x1_ref[rows, cs] + _dot(y, wout_ref[:, cs])


def _mixproj_tile(x1_ref, g2_ref, wp_refs, wout_ref, x2_ref, p_scr, y_scr, group, n_groups,
                  make_block):
    rows = [slice(g * group, (g + 1) * group) for g in range(n_groups)]
    for g in range(n_groups):
        _in_proj(x1_ref, g2_ref, wp_refs, p_scr, g, rows[g])
    blocks = [make_block(g, i) for g in range(n_groups) for i in range(group // ROWS)]
    for _ in _staggered(blocks, MIX_STAGGER):
        pass
    for g in range(n_groups):
        _out_proj(x1_ref, y_scr, wout_ref, x2_ref, g, rows[g])


def _block_cols(p_scr, g, i):
    return lambda off, width: p_scr[g, i * ROWS:(i + 1) * ROWS, off:off + width]


def _mix_setup(a_ws_ref, a_bs_ref, b_wa2_ref, wmix_scr, bias_scr, wa2_scr, cum_scr, seq_len, chunk):
    rr = lax.broadcasted_iota(jnp.int32, (ROWS, 2 * ROWS), 0)
    kk = lax.broadcasted_iota(jnp.int32, (ROWS, 2 * ROWS), 1) & (ROWS - 1)
    cshift = int(np.log2(chunk))
    same = lax.shift_right_logical(rr, cshift) == lax.shift_right_logical(kk, cshift)
    for i, keep in enumerate((same & (kk <= rr), same & ((kk & (chunk - 1)) <= chunk // 2), same)):
        cum_scr[i * ROWS:(i + 1) * ROWS, :] = jnp.where(keep, 1.0, 0.0).astype(BF)

    r = lax.broadcasted_iota(jnp.int32, (ROWS, ROWS), 0)
    c = lax.broadcasted_iota(jnp.int32, (ROWS, ROWS), 1)
    lane = lax.broadcasted_iota(jnp.int32, (8, LANES), 1)
    shift = int(np.log2(seq_len))
    for g in range(A_GROUPS):
        w = a_ws_ref[g]
        if seq_len == ROWS:
            wm = jnp.where(c <= r, w, 0.0).astype(BF)
        else:
            w8 = jnp.where((r < seq_len) & (c <= r), w, 0.0).astype(BF)
            sel = ((r & (seq_len - 1)) == c).astype(BF)
            tiled = _dot_nt(_dot(sel, w8).astype(BF), sel)
            same = lax.shift_right_logical(r, shift) == lax.shift_right_logical(c, shift)
            wm = jnp.where(same, tiled, 0.0).astype(BF)
        wmix_scr[g] = wm
        b = jnp.broadcast_to(a_bs_ref[g:g + 1, :], (8, LANES))
        if seq_len < ROWS:
            b = jnp.where(lane < seq_len, b, 0.0)
            k = seq_len
            while k < ROWS:
                b = b + pltpu.roll(b, k, axis=1)
                k *= 2
        bias_scr[g] = jnp.broadcast_to(b[0:1, :], (ROWS, LANES)).T
    wa2_scr[...] = jnp.zeros_like(wa2_scr)
    wa2_scr[0:B_LOWRANK, :] = b_wa2_ref[...].astype(BF)


def _mixproj_prompt_kernel(*refs, group, n_groups, cast_modes):
    n_cast = len(cast_modes)
    (x1_ref, g2_ref, wp_ref, wpt_ref, wout_ref, a_ws_ref, a_bs_ref, vnorm_ref, anorm_ref,
     b_wa2_ref, ba_ref, onorm_ref) = refs[:12]
    cast_src = refs[12:12 + n_cast]
    x2_ref, st_ref = refs[12 + n_cast:14 + n_cast]
    cast_dst = refs[14 + n_cast:14 + 2 * n_cast]
    p_scr, y_scr, wmix_scr, bias_scr, wa2_scr, cum_scr, s_scr = refs[14 + 2 * n_cast:]

    @pl.when((pl.program_id(0) == 0) & (pl.program_id(1) == 0))
    def _():
        _mix_setup(a_ws_ref, a_bs_ref, b_wa2_ref, wmix_scr, bias_scr, wa2_scr, cum_scr, A_CHUNK,
                   B_CHUNK)

    @pl.when(pl.program_id(1) == 0)
    def _():
        s_scr[...] = jnp.zeros_like(s_scr)

    consts = (wmix_scr, bias_scr, vnorm_ref, anorm_ref, wa2_scr, ba_ref, onorm_ref, cum_scr)
    cur = {pair: (s_scr[2 * pair], s_scr[2 * pair + 1]) for pair in range(B_HEADS // 2)}

    def get_state(pair, j):
        return cur[pair]

    def put_state(pair, j, s):
        cur[pair] = s

    def make_block(g, i):
        def emit(y, va):
            y_scr[g, i * ROWS:(i + 1) * ROWS, :] = y
        return _mix_block(_block_cols(p_scr, g, i), B_CHUNK, consts, get_state, put_state, emit)

    _mixproj_tile(x1_ref, g2_ref, (wp_ref, wpt_ref), wout_ref, x2_ref, p_scr, y_scr, group,
                  n_groups, make_block)
    _cast_blocks(cast_src, cast_dst, cast_modes)

    for pair in range(B_HEADS // 2):
        for hh in range(2):
            s_scr[2 * pair + hh] = cur[pair][hh]
            st_ref[0, 2 * pair + hh] = cur[pair][hh]


def _mixproj_sample_kernel(x1_ref, s0_ref, g2_ref, wp_ref, wpt_ref, wout_ref, a_ws_ref, a_bs_ref,
                           vnorm_ref, anorm_ref, b_wa2_ref, ba_ref, onorm_ref, x2_ref, va_ref,
                           st_ref, p_scr, y_scr, wmix_scr, bias_scr, wa2_scr, cum_scr, *, chunk,
                           group, n_groups):
    @pl.when(pl.program_id(0) == 0)
    def _():
        _mix_setup(a_ws_ref, a_bs_ref, b_wa2_ref, wmix_scr, bias_scr, wa2_scr, cum_scr, chunk, chunk)

    consts = (wmix_scr, bias_scr, vnorm_ref, anorm_ref, wa2_scr, ba_ref, onorm_ref, cum_scr)
    n_blk = group // ROWS
    per_blk = ROWS // chunk

    def make_block(g, i):
        blk = g * n_blk + i

        def get_state(pair, j):
            n = blk * per_blk + j
            return s0_ref[n, 2 * pair], s0_ref[n, 2 * pair + 1]

        def put_state(pair, j, s):
            n = blk * per_blk + j
            st_ref[n, 2 * pair] = s[0]
            st_ref[n, 2 * pair + 1] = s[1]

        def emit(y, va):
            y_scr[g, i * ROWS:(i + 1) * ROWS, :] = y
            va_ref[blk * ROWS:(blk + 1) * ROWS, :] = va

        return _mix_block(_block_cols(p_scr, g, i), chunk, consts, get_state, put_state, emit)

    _mixproj_tile(x1_ref, g2_ref, (wp_ref, wpt_ref), wout_ref, x2_ref, p_scr, y_scr, group,
                  n_groups, make_block)


def _mix_param_specs():
    return [
        _resident((A_GROUPS, A_CHUNK, A_CHUNK)),
        _resident((A_GROUPS, A_CHUNK)),
        _resident((1, A_WIDTH)),
        _resident((1, A_WIDTH)),
        _resident((B_LOWRANK, B_HEADS * B_DK)),
        _resident((1, B_HEADS * B_DK)),
        _resident((1, B_HEADS * B_DV)),
    ]


def _proj_specs():
    return [
        _resident((1, D_MODEL)),
        _resident((D_MODEL, P_MAIN)),
        _resident((D_MODEL, P_COLS - P_MAIN)),
        _resident((D_MODEL, D_MODEL)),
    ]


def _mixproj_scratch(group, n_groups):
    return [pltpu.VMEM((n_groups, group, P_COLS), F32),
            pltpu.VMEM((n_groups, group, D_MODEL), BF),
            pltpu.VMEM((A_GROUPS, ROWS, ROWS), BF),
            pltpu.VMEM((A_GROUPS, ROWS, LANES), F32),
            pltpu.VMEM((LANES, B_HEADS * B_DK), BF),
            pltpu.VMEM((3 * ROWS, 2 * ROWS), BF)]


def _mixproj_prompt_call(x1, g2, wp, wpt, wout, consts, batch, seq, tile, group, cast_jobs=()):
    n_tiles = seq // tile
    n_groups = tile // group
    assert not cast_jobs or batch * n_tiles == CAST_BLOCKS
    row = lambda b, s: (b * n_tiles + s, 0)
    cast_in, cast_out, cast_shapes = _cast_specs(cast_jobs, lambda b, s: b * n_tiles + s)
    return pl.pallas_call(
        functools.partial(_mixproj_prompt_kernel, group=group, n_groups=n_groups,
                          cast_modes=tuple(j["transposed"] for j in cast_jobs)),
        grid=(batch, n_tiles),
        in_specs=[pl.BlockSpec((tile, D_MODEL), row)] + _proj_specs() + _mix_param_specs()
        + cast_in,
        out_specs=[
            pl.BlockSpec((tile, D_MODEL), row),
            pl.BlockSpec((1, B_HEADS, B_DK, B_DV), lambda b, s: (b, 0, 0, 0)),
            *cast_out,
        ],
        out_shape=[
            jax.ShapeDtypeStruct((batch * seq, D_MODEL), F32),
            jax.ShapeDtypeStruct((batch, B_HEADS, B_DK, B_DV), F32),
            *cast_shapes,
        ],
        scratch_shapes=_mixproj_scratch(group, n_groups)
        + [pltpu.VMEM((B_HEADS, B_DK, B_DV), F32)],
        compiler_params=pltpu.CompilerParams(
            dimension_semantics=("arbitrary", "arbitrary"), vmem_limit_bytes=VMEM_LIMIT),
        name="mixproj_prompt",
    )(x1, g2, wp, wpt, wout, *consts, *[j["w"] for j in cast_jobs])


def _mixproj_sample_call(x1, s0, g2, wp, wpt, wout, consts, n_seq, seq, tile, group):
    seq_per_step = tile // seq
    n_groups = tile // group
    row = lambda i: (i, 0)
    state = lambda i: (i, 0, 0, 0)
    return pl.pallas_call(
        functools.partial(_mixproj_sample_kernel, chunk=seq, group=group, n_groups=n_groups),
        grid=(n_seq // seq_per_step,),
        in_specs=[
            pl.BlockSpec((tile, D_MODEL), row),
            pl.BlockSpec((seq_per_step, B_HEADS, B_DK, B_DV), state),
        ] + _proj_specs() + _mix_param_specs(),
        out_specs=[
            pl.BlockSpec((tile, D_MODEL), row),
            pl.BlockSpec((tile, A_WIDTH), row),
            pl.BlockSpec((seq_per_step, B_HEADS, B_DK, B_DV), state),
        ],
        out_shape=[
            jax.ShapeDtypeStruct((n_seq * seq, D_MODEL), F32),
            jax.ShapeDtypeStruct((n_seq * seq, A_WIDTH), F32),
            jax.ShapeDtypeStruct((n_seq, B_HEADS, B_DK, B_DV), F32),
        ],
        scratch_shapes=_mixproj_scratch(group, n_groups),
        compiler_params=pltpu.CompilerParams(
            dimension_semantics=("arbitrary",), vmem_limit_bytes=VMEM_LIMIT),
        name="mixproj_sample",
    )(x1, s0, g2, wp, wpt, wout, *consts)


def _mix_params(a_ws, a_bs, a_vnorm, a_onorm, b_wa2, b_ba, b_onorm):
    return (
        a_ws.reshape(A_GROUPS, A_CHUNK, A_CHUNK),
        a_bs.reshape(A_GROUPS, A_CHUNK),
        a_vnorm.reshape(1, A_WIDTH),
        a_onorm.reshape(1, A_WIDTH),
        b_wa2.reshape(B_LOWRANK, B_HEADS * B_DK),
        b_ba.reshape(1, B_HEADS * B_DK),
        b_onorm.reshape(1, B_HEADS * B_DV),
    )


def kernel(x_prompt, x_sample, state_gla, ffn1_norm, ffn1_w_in, ffn1_w_out, mix_norm, w_in,
           a_ws, a_bs, a_vnorm, a_onorm, b_wa2, b_ba, b_onorm, w_out, ffn2_norm, ffn2_w_in,
           ffn2_w_out, final_norm):
    batch, seq, _ = x_prompt.shape
    n_dec, dec_seq, _ = x_sample.shape
    assert ffn1_norm.shape[0] == 1, "single-layer kernel"
    assert seq % A_CHUNK == 0 and ROWS % dec_seq == 0 and dec_seq <= B_CHUNK

    g1 = ffn1_norm.reshape(1, D_MODEL)
    wi1 = ffn1_w_in.reshape(D_MODEL, 2 * D_FF)
    wo1 = ffn1_w_out.reshape(D_FF, D_MODEL)
    g2 = mix_norm.reshape(1, D_MODEL)
    w_in_t = jnp.swapaxes(w_in, 1, 2).reshape(IN_COLS, D_MODEL)
    w_out2d = w_out.reshape(D_MODEL, D_MODEL)
    g3 = ffn2_norm.reshape(1, D_MODEL)
    wi2 = ffn2_w_in.reshape(D_MODEL, 2 * D_FF)
    wo2 = ffn2_w_out.reshape(D_FF, D_MODEL)
    gf = final_norm.reshape(1, D_MODEL)
    mix_params = _mix_params(a_ws, a_bs, a_vnorm, a_onorm, b_wa2, b_ba, b_onorm)

    xp = x_prompt.reshape(batch * seq, D_MODEL)
    xs = x_sample.reshape(n_dec * dec_seq, D_MODEL)
    s0 = state_gla.reshape(state_gla.shape[1:])

    x1p, x1s, wp, wpt, wout = _ffn_call(
        xp, xs, g1, wi1, wo1, gf, False, "ffn1",
        cast_jobs=(_cast_transposed(w_in_t, 0, MXU_DIM, P_MAIN),
                   _cast_transposed(w_in_t, P_MAIN, P_COLS - P_MAIN, P_COLS - P_MAIN,
                                    valid_cols=IN_COLS - P_MAIN),
                   _cast_rows(w_out2d)))
    x2p, st_p, wi2_bf, wo2_bf = _mixproj_prompt_call(
        x1p, g2, wp, wpt, wout, mix_params, batch, seq, tile=1024, group=512,
        cast_jobs=(_cast_rows(wi2), _cast_rows(wo2)))
    x2s, va_s, st_s = _mixproj_sample_call(x1s, s0, g2, wp, wpt, wout, mix_params, n_dec, dec_seq,
                                           tile=256, group=128)
    out_p, out_s = _ffn_call(x2p, x2s, g3, wi2_bf, wo2_bf, gf, True, "ffn2")

    return (
        out_p.reshape(batch, seq, D_MODEL),
        out_s.reshape(n_dec, dec_seq, D_MODEL),
        st_p[None],
        st_s[None],
        va_s.reshape(1, n_dec, dec_seq, A_WIDTH),
    )
```

```python
import functools

import numpy as np
import jax
import jax.numpy as jnp
from jax import lax
from jax.experimental import pallas as pl
from jax.experimental.pallas import tpu as pltpu

D_MODEL = 1024
D_FF = 2816
A_WIDTH = 512
A_GROUPS = 4
A_CHUNK = 128
B_HEADS = 4
B_DK = 64
B_DV = 128
B_LOWRANK = 16
B_GATE_NORMALIZER = 16.0
B_CHUNK = 64
EPS = 1e-6

OFF_U, OFF_V, OFF_Q, OFF_K, OFF_VB, OFF_R, OFF_LR = 0, 512, 1024, 1280, 1536, 2048, 2560
IN_COLS = 2576

LANES = 128
P_COLS = -(-IN_COLS // LANES) * LANES
MXU_DIM = 256
ROWS = 128
FF_CHUNK = MXU_DIM
N_FF_CHUNKS = D_FF // FF_CHUNK
VMEM_LIMIT = 56 * 1024 * 1024

FFN_TILE = 1024
FFN_HALF = FFN_TILE // 2
STREAM_PARTS = 2
P_MAIN = (IN_COLS // LANES) * LANES
CAST_BLOCKS = 16
MIX_STAGGER = 10

BF = jnp.bfloat16
F32 = jnp.float32


def _dot(a, b):
    return jnp.dot(a, b, preferred_element_type=F32)


def _dot_nt(a, b):
    return lax.dot_general(a, b, (((1,), (1,)), ((), ())), preferred_element_type=F32)


def _rmsnorm(x, g):
    return (x * lax.rsqrt(jnp.mean(x * x, axis=-1, keepdims=True) + EPS)) * g


def _gelu_tanh(x):
    c = np.float32(np.sqrt(2.0 / np.pi))
    ca = np.float32(np.sqrt(2.0 / np.pi) * 0.044715)
    t = jnp.tanh(x * (c + ca * (x * x)))
    hx = 0.5 * x
    return hx + hx * t


def _silu(x):
    return x * (1.0 / (1.0 + jnp.exp(-x)))


def _log_sigmoid(x):
    return jnp.minimum(x, 0.0) - jnp.log(1.0 + jnp.exp(-jnp.abs(x)))


def _staggered(gens, stagger):
    live = [True] * len(gens)
    t = 0
    while any(live):
        for i, gen in enumerate(gens):
            if live[i] and t >= i * stagger:
                try:
                    next(gen)
                except StopIteration:
                    live[i] = False
        t += 1
        yield


def _cast_rows(w):
    rows, cols = w.shape
    blk = (rows // CAST_BLOCKS, cols)
    return dict(w=w, src_block=blk, src_index=lambda r: (r, 0), dst_block=blk,
                dst_index=lambda r: (r, 0), out=jax.ShapeDtypeStruct(w.shape, BF), transposed=None)


def _cast_transposed(w_t, row0, n_rows, out_cols, valid_cols=None):
    cols = w_t.shape[1]
    first, last = row0 // n_rows, (row0 + out_cols) // n_rows - 1
    return dict(w=w_t, src_block=(n_rows, cols), dst_block=(cols, n_rows),
                src_index=lambda r: (jnp.minimum(first + r, last), 0),
                dst_index=lambda r: (0, jnp.minimum(r, last - first)),
                out=jax.ShapeDtypeStruct((cols, out_cols), BF),
                transposed=n_rows if valid_cols is None else valid_cols)


def _cast_chunked(w):
    rows, cols = w.shape
    n = cols // FF_CHUNK
    return dict(w=w, src_block=(rows // CAST_BLOCKS, cols), src_index=lambda r: (r, 0),
                dst_block=(n, rows // CAST_BLOCKS, FF_CHUNK), dst_index=lambda r: (0, r, 0),
                out=jax.ShapeDtypeStruct((n, rows, FF_CHUNK), BF), transposed="chunks")


def _cast_specs(jobs, row_block):
    in_specs = [pl.BlockSpec(j["src_block"], lambda *ids, j=j: j["src_index"](row_block(*ids)))
                for j in jobs]
    out_specs = [pl.BlockSpec(j["dst_block"], lambda *ids, j=j: j["dst_index"](row_block(*ids)))
                 for j in jobs]
    return in_specs, out_specs, [j["out"] for j in jobs]


def _cast_blocks(src_refs, dst_refs, transposed):
    for src, dst, valid in zip(src_refs, dst_refs, transposed):
        x = src[...]
        if valid == "chunks":
            for c in range(dst.shape[0]):
                dst[c] = x[:, c * FF_CHUNK:(c + 1) * FF_CHUNK].astype(BF)
            continue
        if valid is not None:
            x = x.T
            if valid < x.shape[1]:
                lane = lax.broadcasted_iota(jnp.int32, x.shape, 1)
                x = jnp.where(lane < valid, x, 0.0)
        dst[...] = x.astype(BF)


def _ffn_chunk(h, wi_ref, wo_ref, acc_ref, rows, j):
    cs = slice(j * FF_CHUNK, (j + 1) * FF_CHUNK)
    gate = _dot(h, wi_ref[:, cs])
    up = _dot(h, wi_ref[:, D_FF + j * FF_CHUNK:D_FF + (j + 1) * FF_CHUNK])
    act = (_silu(gate) * up).astype(BF)
    part = _dot(act, wo_ref[cs, :])
    if j == 0:
        acc_ref[rows, :] = part
    else:
        acc_ref[rows, :] += part


def _ffn_streamed_tile(x_ref, g_ref, wi_hbm, wo_hbm, wi_ref, wo_ref, acc_ref, stage, sem, finish):
    def parts(j):
        cols = (pl.ds(j * FF_CHUNK, FF_CHUNK), pl.ds(D_FF + j * FF_CHUNK, FF_CHUNK))
        out = []
        for buf, (w_hbm, n_rows, col) in zip(stage, ((wi_hbm, D_MODEL, cols[0]),
                                                      (wi_hbm, D_MODEL, cols[1]),
                                                      (wo_hbm, FF_CHUNK, None))):
            n = n_rows // STREAM_PARTS
            for p in range(STREAM_PARTS):
                rows = pl.ds(p * n, n)
                src = w_hbm.at[rows, col] if col is not None else \
                    w_hbm.at[pl.ds(j * FF_CHUNK + p * n, n), pl.ds(0, D_MODEL)]
                out.append((src, buf.at[rows, pl.ds(0, buf.shape[1])]))
        return out

    def copy(j, k):
        src, dst = parts(j)[k]
        return pltpu.make_async_copy(src, dst, sem.at[k])

    n_parts = 3 * STREAM_PARTS
    for k in range(n_parts):
        copy(0, k).start()
    halves = [slice(i * FFN_HALF, (i + 1) * FFN_HALF) for i in range(FFN_TILE // FFN_HALF)]
    h = [_rmsnorm(x_ref[rows, :], g_ref[...]).astype(BF) for rows in halves]
    for j in range(N_FF_CHUNKS):
        cs = slice(j * FF_CHUNK, (j + 1) * FF_CHUNK)
        for k in range(n_parts):
            copy(j, k).wait()
        wi_ref[:, cs] = stage[0][...].astype(BF)
        wi_ref[:, D_FF + j * FF_CHUNK:D_FF + (j + 1) * FF_CHUNK] = stage[1][...].astype(BF)
        wo_ref[cs, :] = stage[2][...].astype(BF)
        if j + 1 < N_FF_CHUNKS:
            for k in range(n_parts):
                copy(j + 1, k).start()
        for rows, hh in zip(halves, h):
            _ffn_chunk(hh, wi_ref, wo_ref, acc_ref, rows, j)
    for rows in halves:
        finish(rows, x_ref[rows, :], acc_ref[rows, :])


def _ffn_prefetched_tile(x_ref, g_ref, wi_hbm, wo_hbm, wi_ref, wo_ref, acc_ref, sem, finish):
    def copies(j):
        cs = pl.ds(j * FF_CHUNK, FF_CHUNK)
        cu = pl.ds(D_FF + j * FF_CHUNK, FF_CHUNK)
        all_rows = pl.ds(0, D_MODEL)
        return (pltpu.make_async_copy(wi_hbm.at[j], wi_ref.at[all_rows, cs], sem.at[3 * j]),
                pltpu.make_async_copy(wi_hbm.at[N_FF_CHUNKS + j], wi_ref.at[all_rows, cu],
                                      sem.at[3 * j + 1]),
                pltpu.make_async_copy(wo_hbm.at[cs, all_rows], wo_ref.at[cs, all_rows],
                                      sem.at[3 * j + 2]))

    for j in range(N_FF_CHUNKS):
        for c in copies(j):
            c.start()
    halves = [slice(i * FFN_HALF, (i + 1) * FFN_HALF) for i in range(FFN_TILE // FFN_HALF)]
    h = [_rmsnorm(x_ref[rows, :], g_ref[...]).astype(BF) for rows in halves]
    for j in range(N_FF_CHUNKS):
        for c in copies(j):
            c.wait()
        for rows, hh in zip(halves, h):
            _ffn_chunk(hh, wi_ref, wo_ref, acc_ref, rows, j)
    for rows in halves:
        finish(rows, x_ref[rows, :], acc_ref[rows, :])


def _ffn_rows(x_ref, g_ref, wi_ref, wo_ref, acc_ref, rows, finish):
    h = _rmsnorm(x_ref[rows, :], g_ref[...]).astype(BF)
    yield
    for j in range(N_FF_CHUNKS):
        _ffn_chunk(h, wi_ref, wo_ref, acc_ref, rows, j)
        yield
    finish(rows, x_ref[rows, :], acc_ref[rows, :])


def _ffn_tile(x_ref, g_ref, wi_ref, wo_ref, acc_ref, finish):
    halves = [_ffn_rows(x_ref, g_ref, wi_ref, wo_ref, acc_ref,
                        slice(i * FFN_HALF, (i + 1) * FFN_HALF), finish)
              for i in range(FFN_TILE // FFN_HALF)]
    for _ in _staggered(halves, 1):
        pass


def _ffn_kernel(*refs, final_norm, stream_weights, cast_modes):
    n_cast = len(cast_modes)
    xa_ref, xb_ref, g_ref, wi_in, wo_in, gf_ref = refs[:6]
    cast_src = refs[6:6 + n_cast]
    oa_ref, ob_ref = refs[6 + n_cast:8 + n_cast]
    cast_dst = refs[8 + n_cast:8 + 2 * n_cast]
    scratch = refs[8 + 2 * n_cast:]
    first = pl.program_id(0) == 0

    def finish_into(o_ref):
        def finish(rows, x, acc):
            y = x + 0.5 * acc
            o_ref[rows, :] = _rmsnorm(y, gf_ref[...]) if final_norm else y
        return finish

    if stream_weights:
        wi_ref, wo_ref, acc_ref, *stage, sem = scratch

        @pl.when(first)
        def _():
            _ffn_streamed_tile(xb_ref, g_ref, wi_in, wo_in, wi_ref, wo_ref, acc_ref, stage, sem,
                               finish_into(ob_ref))
    else:
        wi_ref, wo_ref, acc_ref, sem = scratch

        @pl.when(first)
        def _():
            _ffn_prefetched_tile(xb_ref, g_ref, wi_in, wo_in, wi_ref, wo_ref, acc_ref, sem,
                                 finish_into(ob_ref))

    @pl.when(jnp.logical_not(first))
    def _():
        _ffn_tile(xa_ref, g_ref, wi_ref, wo_ref, acc_ref, finish_into(oa_ref))

    _cast_blocks(cast_src, cast_dst, cast_modes)


def _resident(shape):
    nd = len(shape)
    return pl.BlockSpec(shape, lambda *_: (0,) * nd, pipeline_mode=pl.Buffered(1))


def _ffn_call(xa, xb, g, wi, wo, gf, final_norm, name, cast_jobs=()):
    ta = xa.shape[0]
    n_a = ta // FFN_TILE
    assert ta % FFN_TILE == 0 and xb.shape[0] == FFN_TILE
    assert not cast_jobs or n_a == CAST_BLOCKS
    stream_weights = wi.dtype == F32
    tile_a = lambda i: (jnp.maximum(i - 1, 0), 0)
    cast_in, cast_out, cast_shapes = _cast_specs(cast_jobs, lambda i: jnp.maximum(i - 1, 0))
    w_specs = [pl.BlockSpec(memory_space=pl.ANY)] * 2
    w_scratch = [pltpu.VMEM((D_MODEL, 2 * D_FF), BF), pltpu.VMEM((D_FF, D_MODEL), BF)]
    if stream_weights:
        stage = [pltpu.VMEM((D_MODEL, FF_CHUNK), F32), pltpu.VMEM((D_MODEL, FF_CHUNK), F32),
                 pltpu.VMEM((FF_CHUNK, D_MODEL), F32)]
        sems = stage + [pltpu.SemaphoreType.DMA((3 * STREAM_PARTS,))]
    else:
        sems = [pltpu.SemaphoreType.DMA((3 * N_FF_CHUNKS,))]
    return pl.pallas_call(
        functools.partial(_ffn_kernel, final_norm=final_norm, stream_weights=stream_weights,
                          cast_modes=tuple(j["transposed"] for j in cast_jobs)),
        grid=(1 + n_a,),
        in_specs=[
            pl.BlockSpec((FFN_TILE, D_MODEL), tile_a),
            _resident((FFN_TILE, D_MODEL)),
            _resident((1, D_MODEL)),
            *w_specs,
            _resident((1, D_MODEL)),
            *cast_in,
        ],
        out_specs=[
            pl.BlockSpec((FFN_TILE, D_MODEL), tile_a),
            pl.BlockSpec((FFN_TILE, D_MODEL), lambda i: (0, 0)),
            *cast_out,
        ],
        out_shape=[
            jax.ShapeDtypeStruct((ta, D_MODEL), F32),
            jax.ShapeDtypeStruct((FFN_TILE, D_MODEL), F32),
            *cast_shapes,
        ],
        scratch_shapes=w_scratch + [pltpu.VMEM((FFN_TILE, D_MODEL), F32)] + sems,
        compiler_params=pltpu.CompilerParams(
            dimension_semantics=("arbitrary",), vmem_limit_bytes=VMEM_LIMIT),
        name=name,
    )(xa, xb, g, wi, wo, gf, *[j["w"] for j in cast_jobs])


def _mix_block(cols, chunk, consts, get_state, put_state, emit):
    (wmix_ref, bias_ref, vnorm_ref, anorm_ref, wa2_ref, ba_ref, onorm_ref, cum_ref) = consts
    n_chunks = ROWS // chunk
    shift = int(np.log2(chunk))

    z = _dot(cols(OFF_LR, LANES).astype(BF), wa2_ref[...]) + ba_ref[...]
    la = _log_sigmoid(z) * (1.0 / B_GATE_NORMALIZER)
    la_hi = la.astype(BF)
    la_lo = (la - la_hi.astype(F32)).astype(BF)
    sums = _dot(cum_ref[...], jnp.concatenate([la_hi, la_lo], axis=0))
    b = sums[0:ROWS]
    b_ref = sums[ROWS:2 * ROWS]
    b_last = sums[2 * ROWS:3 * ROWS]
    yield

    ya = []
    va = []
    ssq = None
    for g in range(A_GROUPS):
        gs = slice(g * LANES, (g + 1) * LANES)
        u = _gelu_tanh(cols(OFF_U + g * LANES, LANES))
        v = _rmsnorm(_gelu_tanh(cols(OFF_V + g * LANES, LANES)), vnorm_ref[:, gs])
        yield
        zg = _dot(wmix_ref[g], v.astype(BF)) + bias_ref[g]
        yg = u * zg
        s = jnp.sum(yg * yg, axis=-1, keepdims=True)
        ssq = s if ssq is None else ssq + s
        ya.append(yg)
        va.append(v)
        yield
    inv = lax.rsqrt(ssq * (1.0 / A_WIDTH) + EPS)
    ya = [(ya[g] * inv) * anorm_ref[:, g * LANES:(g + 1) * LANES] for g in range(A_GROUPS)]
    yield
    gate = []
    for h in range(B_HEADS):
        gate.append(_silu(cols(OFF_R + h * B_DV, B_DV)))
        if h % 2 == 1:
            yield

    ri = lax.broadcasted_iota(jnp.int32, (ROWS, ROWS), 0)
    ci = lax.broadcasted_iota(jnp.int32, (ROWS, ROWS), 1)
    rchunk = lax.shift_right_logical(ri, shift)
    cchunk = lax.shift_right_logical(ci, shift)
    causal = (rchunk == cchunk) & (ci <= ri)
    lane = lax.broadcasted_iota(jnp.int32, (ROWS, LANES), 1)
    head0 = lane < B_DK

    yb = []
    for pair in range(B_HEADS // 2):
        ls = slice(pair * LANES, (pair + 1) * LANES)
        q = cols(OFF_Q + pair * LANES, LANES) * np.float32(B_DK ** -0.5)
        k = cols(OFF_K + pair * LANES, LANES)
        vv_bf = cols(OFF_VB + pair * 2 * B_DV, 2 * B_DV).astype(BF)
        bp, brp, blp = b[:, ls], b_ref[:, ls], b_last[:, ls]
        qs = q * jnp.exp(bp - brp)
        ks = (k * jnp.exp(brp - bp)).astype(BF)
        kl_t = (k * jnp.exp(blp - bp)).T
        bl_t = blp.T
        qb = (q * jnp.exp(bp)).astype(BF)
        yield

        o_heads = []
        for hh in range(2):
            msk = head0 if hh == 0 else jnp.logical_not(head0)
            qh = jnp.where(msk, qs, 0.0).astype(BF)
            sc = jnp.where(causal, _dot_nt(qh, ks), 0.0).astype(BF)
            o_heads.append(_dot(sc, vv_bf[:, hh * B_DV:(hh + 1) * B_DV]))
            yield
        o = jnp.concatenate(o_heads, axis=1)

        o_inter = []
        for j in range(n_chunks):
            s_prev = get_state(pair, j)
            o_inter.append(_dot(qb[j * chunk:(j + 1) * chunk], _blockdiag_bf16(*s_prev)))
            in_chunk = cchunk == j
            upd = _dot(jnp.where(in_chunk, kl_t, 0.0).astype(BF), vv_bf)
            dec = jnp.exp(bl_t[:, j * chunk:j * chunk + 1])
            put_state(pair, j, tuple(
                dec[hh * B_DK:(hh + 1) * B_DK] * s_prev[hh]
                + upd[hh * B_DK:(hh + 1) * B_DK, hh * B_DV:(hh + 1) * B_DV] for hh in range(2)))
        o = o + jnp.concatenate(o_inter, axis=0)
        yield

        for hh in range(2):
            h_idx = 2 * pair + hh
            hs = slice(h_idx * B_DV, (h_idx + 1) * B_DV)
            oh = _rmsnorm(o[:, hh * B_DV:(hh + 1) * B_DV], onorm_ref[:, hs])
            yb.append(oh * gate[h_idx])
        yield

    emit(jnp.concatenate(ya + yb, axis=1).astype(BF), jnp.concatenate(va, axis=1))


def _blockdiag_bf16(s0, s1):
    s0 = s0.astype(BF)
    s1 = s1.astype(BF)
    zero = jnp.zeros_like(s0)
    return jnp.concatenate(
        [jnp.concatenate([s0, zero], axis=1), jnp.concatenate([zero, s1], axis=1)], axis=0)


def _in_proj(x1_ref, g2_ref, wp_refs, p_scr, g, rows):
    wp_ref, wpt_ref = wp_refs
    h = _rmsnorm(x1_ref[rows, :], g2_ref[...]).astype(BF)
    for c0 in range(0, P_MAIN, MXU_DIM):
        cs = slice(c0, c0 + MXU_DIM)
        p_scr[g, :, cs] = _dot(h, wp_ref[:, cs])
    p_scr[g, :, P_MAIN:P_COLS] = _dot(h, wpt_ref[...])


def _out_proj(x1_ref, y_scr, wout_ref, x2_ref, g, rows):
    y = y_scr[g]
    for c0 in range(0, D_MODEL, MXU_DIM):
        cs = slice(c0, c0 + MXU_DIM)
        x2_ref[rows, cs] = x1_ref[rows, cs] + _dot(y, wout_ref[:, cs])


def _mixproj_tile(x1_ref, g2_ref, wp_refs, wout_ref, x2_ref, p_scr, y_scr, group, n_groups,
                  make_block):
    rows = [slice(g * group, (g + 1) * group) for g in range(n_groups)]
    for g in range(n_groups):
        _in_proj(x1_ref, g2_ref, wp_refs, p_scr, g, rows[g])
    blocks = [make_block(g, i) for g in range(n_groups) for i in range(group // ROWS)]
    for _ in _staggered(blocks, MIX_STAGGER):
        pass
    for g in range(n_groups):
        _out_proj(x1_ref, y_scr, wout_ref, x2_ref, g, rows[g])


def _block_cols(p_scr, g, i):
    return lambda off, width: p_scr[g, i * ROWS:(i + 1) * ROWS, off:off + width]


def _mix_setup(a_ws_ref, a_bs_ref, b_wa2_ref, wmix_scr, bias_scr, wa2_scr, cum_scr, seq_len, chunk):
    rr = lax.broadcasted_iota(jnp.int32, (ROWS, 2 * ROWS), 0)
    kk = lax.broadcasted_iota(jnp.int32, (ROWS, 2 * ROWS), 1) & (ROWS - 1)
    cshift = int(np.log2(chunk))
    same = lax.shift_right_logical(rr, cshift) == lax.shift_right_logical(kk, cshift)
    for i, keep in enumerate((same & (kk <= rr), same & ((kk & (chunk - 1)) <= chunk // 2), same)):
        cum_scr[i * ROWS:(i + 1) * ROWS, :] = jnp.where(keep, 1.0, 0.0).astype(BF)

    r = lax.broadcasted_iota(jnp.int32, (ROWS, ROWS), 0)
    c = lax.broadcasted_iota(jnp.int32, (ROWS, ROWS), 1)
    lane = lax.broadcasted_iota(jnp.int32, (8, LANES), 1)
    shift = int(np.log2(seq_len))
    for g in range(A_GROUPS):
        w = a_ws_ref[g]
        if seq_len == ROWS:
            wm = jnp.where(c <= r, w, 0.0).astype(BF)
        else:
            w8 = jnp.where((r < seq_len) & (c <= r), w, 0.0).astype(BF)
            sel = ((r & (seq_len - 1)) == c).astype(BF)
            tiled = _dot_nt(_dot(sel, w8).astype(BF), sel)
            same = lax.shift_right_logical(r, shift) == lax.shift_right_logical(c, shift)
            wm = jnp.where(same, tiled, 0.0).astype(BF)
        wmix_scr[g] = wm
        b = jnp.broadcast_to(a_bs_ref[g:g + 1, :], (8, LANES))
        if seq_len < ROWS:
            b = jnp.where(lane < seq_len, b, 0.0)
            k = seq_len
            while k < ROWS:
                b = b + pltpu.roll(b, k, axis=1)
                k *= 2
        bias_scr[g] = jnp.broadcast_to(b[0:1, :], (ROWS, LANES)).T
    wa2_scr[...] = jnp.zeros_like(wa2_scr)
    wa2_scr[0:B_LOWRANK, :] = b_wa2_ref[...].astype(BF)


def _mixproj_prompt_kernel(*refs, group, n_groups, cast_modes):
    n_cast = len(cast_modes)
    (x1_ref, g2_ref, wp_ref, wpt_ref, wout_ref, a_ws_ref, a_bs_ref, vnorm_ref, anorm_ref,
     b_wa2_ref, ba_ref, onorm_ref) = refs[:12]
    cast_src = refs[12:12 + n_cast]
    x2_ref, st_ref = refs[12 + n_cast:14 + n_cast]
    cast_dst = refs[14 + n_cast:14 + 2 * n_cast]
    p_scr, y_scr, wmix_scr, bias_scr, wa2_scr, cum_scr, s_scr = refs[14 + 2 * n_cast:]

    @pl.when((pl.program_id(0) == 0) & (pl.program_id(1) == 0))
    def _():
        _mix_setup(a_ws_ref, a_bs_ref, b_wa2_ref, wmix_scr, bias_scr, wa2_scr, cum_scr, A_CHUNK,
                   B_CHUNK)

    @pl.when(pl.program_id(1) == 0)
    def _():
        s_scr[...] = jnp.zeros_like(s_scr)

    consts = (wmix_scr, bias_scr, vnorm_ref, anorm_ref, wa2_scr, ba_ref, onorm_ref, cum_scr)
    cur = {pair: (s_scr[2 * pair], s_scr[2 * pair + 1]) for pair in range(B_HEADS // 2)}

    def get_state(pair, j):
        return cur[pair]

    def put_state(pair, j, s):
        cur[pair] = s

    def make_block(g, i):
        def emit(y, va):
            y_scr[g, i * ROWS:(i + 1) * ROWS, :] = y
        return _mix_block(_block_cols(p_scr, g, i), B_CHUNK, consts, get_state, put_state, emit)

    _mixproj_tile(x1_ref, g2_ref, (wp_ref, wpt_ref), wout_ref, x2_ref, p_scr, y_scr, group,
                  n_groups, make_block)
    _cast_blocks(cast_src, cast_dst, cast_modes)

    for pair in range(B_HEADS // 2):
        for hh in range(2):
            s_scr[2 * pair + hh] = cur[pair][hh]
            st_ref[0, 2 * pair + hh] = cur[pair][hh]


def _mixproj_sample_kernel(x1_ref, s0_ref, g2_ref, wp_ref, wpt_ref, wout_ref, a_ws_ref, a_bs_ref,
                           vnorm_ref, anorm_ref, b_wa2_ref, ba_ref, onorm_ref, x2_ref, va_ref,
                           st_ref, p_scr, y_scr, wmix_scr, bias_scr, wa2_scr, cum_scr, *, chunk,
                           group, n_groups):
    @pl.when(pl.program_id(0) == 0)
    def _():
        _mix_setup(a_ws_ref, a_bs_ref, b_wa2_ref, wmix_scr, bias_scr, wa2_scr, cum_scr, chunk, chunk)

    consts = (wmix_scr, bias_scr, vnorm_ref, anorm_ref, wa2_scr, ba_ref, onorm_ref, cum_scr)
    n_blk = group // ROWS
    per_blk = ROWS // chunk

    def make_block(g, i):
        blk = g * n_blk + i

        def get_state(pair, j):
            n = blk * per_blk + j
            return s0_ref[n, 2 * pair], s0_ref[n, 2 * pair + 1]

        def put_state(pair, j, s):
            n = blk * per_blk + j
            st_ref[n, 2 * pair] = s[0]
            st_ref[n, 2 * pair + 1] = s[1]

        def emit(y, va):
            y_scr[g, i * ROWS:(i + 1) * ROWS, :] = y
            va_ref[blk * ROWS:(blk + 1) * ROWS, :] = va

        return _mix_block(_block_cols(p_scr, g, i), chunk, consts, get_state, put_state, emit)

    _mixproj_tile(x1_ref, g2_ref, (wp_ref, wpt_ref), wout_ref, x2_ref, p_scr, y_scr, group,
                  n_groups, make_block)


def _mix_param_specs():
    return [
        _resident((A_GROUPS, A_CHUNK, A_CHUNK)),
        _resident((A_GROUPS, A_CHUNK)),
        _resident((1, A_WIDTH)),
        _resident((1, A_WIDTH)),
        _resident((B_LOWRANK, B_HEADS * B_DK)),
        _resident((1, B_HEADS * B_DK)),
        _resident((1, B_HEADS * B_DV)),
    ]


def _proj_specs():
    return [
        _resident((1, D_MODEL)),
        _resident((D_MODEL, P_MAIN)),
        _resident((D_MODEL, P_COLS - P_MAIN)),
        _resident((D_MODEL, D_MODEL)),
    ]


def _mixproj_scratch(group, n_groups):
    return [pltpu.VMEM((n_groups, group, P_COLS), F32),
            pltpu.VMEM((n_groups, group, D_MODEL), BF),
            pltpu.VMEM((A_GROUPS, ROWS, ROWS), BF),
            pltpu.VMEM((A_GROUPS, ROWS, LANES), F32),
            pltpu.VMEM((LANES, B_HEADS * B_DK), BF),
            pltpu.VMEM((3 * ROWS, 2 * ROWS), BF)]


def _mixproj_prompt_call(x1, g2, wp, wpt, wout, consts, batch, seq, tile, group, cast_jobs=()):
    n_tiles = seq // tile
    n_groups = tile // group
    assert not cast_jobs or batch * n_tiles == CAST_BLOCKS
    row = lambda b, s: (b * n_tiles + s, 0)
    cast_in, cast_out, cast_shapes = _cast_specs(cast_jobs, lambda b, s: b * n_tiles + s)
    return pl.pallas_call(
        functools.partial(_mixproj_prompt_kernel, group=group, n_groups=n_groups,
                          cast_modes=tuple(j["transposed"] for j in cast_jobs)),
        grid=(batch, n_tiles),
        in_specs=[pl.BlockSpec((tile, D_MODEL), row)] + _proj_specs() + _mix_param_specs()
        + cast_in,
        out_specs=[
            pl.BlockSpec((tile, D_MODEL), row),
            pl.BlockSpec((1, B_HEADS, B_DK, B_DV), lambda b, s: (b, 0, 0, 0)),
            *cast_out,
        ],
        out_shape=[
            jax.ShapeDtypeStruct((batch * seq, D_MODEL), F32),
            jax.ShapeDtypeStruct((batch, B_HEADS, B_DK, B_DV), F32),
            *cast_shapes,
        ],
        scratch_shapes=_mixproj_scratch(group, n_groups)
        + [pltpu.VMEM((B_HEADS, B_DK, B_DV), F32)],
        compiler_params=pltpu.CompilerParams(
            dimension_semantics=("arbitrary", "arbitrary"), vmem_limit_bytes=VMEM_LIMIT),
        name="mixproj_prompt",
    )(x1, g2, wp, wpt, wout, *consts, *[j["w"] for j in cast_jobs])


def _mixproj_sample_call(x1, s0, g2, wp, wpt, wout, consts, n_seq, seq, tile, group):
    seq_per_step = tile // seq
    n_groups = tile // group
    row = lambda i: (i, 0)
    state = lambda i: (i, 0, 0, 0)
    return pl.pallas_call(
        functools.partial(_mixproj_sample_kernel, chunk=seq, group=group, n_groups=n_groups),
        grid=(n_seq // seq_per_step,),
        in_specs=[
            pl.BlockSpec((tile, D_MODEL), row),
            pl.BlockSpec((seq_per_step, B_HEADS, B_DK, B_DV), state),
        ] + _proj_specs() + _mix_param_specs(),
        out_specs=[
            pl.BlockSpec((tile, D_MODEL), row),
            pl.BlockSpec((tile, A_WIDTH), row),
            pl.BlockSpec((seq_per_step, B_HEADS, B_DK, B_DV), state),
        ],
        out_shape=[
            jax.ShapeDtypeStruct((n_seq * seq, D_MODEL), F32),
            jax.ShapeDtypeStruct((n_seq * seq, A_WIDTH), F32),
            jax.ShapeDtypeStruct((n_seq, B_HEADS, B_DK, B_DV), F32),
        ],
        scratch_shapes=_mixproj_scratch(group, n_groups),
        compiler_params=pltpu.CompilerParams(
            dimension_semantics=("arbitrary",), vmem_limit_bytes=VMEM_LIMIT),
        name="mixproj_sample",
    )(x1, s0, g2, wp, wpt, wout, *consts)


def _mix_params(a_ws, a_bs, a_vnorm, a_onorm, b_wa2, b_ba, b_onorm):
    return (
        a_ws.reshape(A_GROUPS, A_CHUNK, A_CHUNK),
        a_bs.reshape(A_GROUPS, A_CHUNK),
        a_vnorm.reshape(1, A_WIDTH),
        a_onorm.reshape(1, A_WIDTH),
        b_wa2.reshape(B_LOWRANK, B_HEADS * B_DK),
        b_ba.reshape(1, B_HEADS * B_DK),
        b_onorm.reshape(1, B_HEADS * B_DV),
    )


def kernel(x_prompt, x_sample, state_gla, ffn1_norm, ffn1_w_in, ffn1_w_out, mix_norm, w_in,
           a_ws, a_bs, a_vnorm, a_onorm, b_wa2, b_ba, b_onorm, w_out, ffn2_norm, ffn2_w_in,
           ffn2_w_out, final_norm):
    batch, seq, _ = x_prompt.shape
    n_dec, dec_seq, _ = x_sample.shape
    assert ffn1_norm.shape[0] == 1, "single-layer kernel"
    assert seq % A_CHUNK == 0 and ROWS % dec_seq == 0 and dec_seq <= B_CHUNK

    g1 = ffn1_norm.reshape(1, D_MODEL)
    wi1 = ffn1_w_in.reshape(D_MODEL, 2 * D_FF)
    wo1 = ffn1_w_out.reshape(D_FF, D_MODEL)
    g2 = mix_norm.reshape(1, D_MODEL)
    w_in_t = jnp.swapaxes(w_in, 1, 2).reshape(IN_COLS, D_MODEL)
    w_out2d = w_out.reshape(D_MODEL, D_MODEL)
    g3 = ffn2_norm.reshape(1, D_MODEL)
    wi2 = ffn2_w_in.reshape(D_MODEL, 2 * D_FF)
    wo2 = ffn2_w_out.reshape(D_FF, D_MODEL)
    gf = final_norm.reshape(1, D_MODEL)
    mix_params = _mix_params(a_ws, a_bs, a_vnorm, a_onorm, b_wa2, b_ba, b_onorm)

    xp = x_prompt.reshape(batch * seq, D_MODEL)
    xs = x_sample.reshape(n_dec * dec_seq, D_MODEL)
    s0 = state_gla.reshape(state_gla.shape[1:])

    x1p, x1s, wp, wpt, wout = _ffn_call(
        xp, xs, g1, wi1, wo1, gf, False, "ffn1",
        cast_jobs=(_cast_transposed(w_in_t, 0, MXU_DIM, P_MAIN),
                   _cast_transposed(w_in_t, P_MAIN, P_COLS - P_MAIN, P_COLS - P_MAIN,
                                    valid_cols=IN_COLS - P_MAIN),
                   _cast_rows(w_out2d)))
    x2p, st_p, wi2_bf, wo2_bf = _mixproj_prompt_call(
        x1p, g2, wp, wpt, wout, mix_params, batch, seq, tile=1024, group=512,
        cast_jobs=(_cast_chunked(wi2), _cast_rows(wo2)))
    x2s, va_s, st_s = _mixproj_sample_call(x1s, s0, g2, wp, wpt, wout, mix_params, n_dec, dec_seq,
                                           tile=256, group=128)
    out_p, out_s = _ffn_call(x2p, x2s, g3, wi2_bf, wo2_bf, gf, True, "ffn2")

    return (
        out_p.reshape(batch, seq, D_MODEL),
        out_s.reshape(n_dec, dec_seq, D_MODEL),
        st_p[None],
        st_s[None],
        va_s.reshape(1, n_dec, dec_seq, A_WIDTH),
    )
```
